```python
import math
import jax, jax.numpy as jnp
from jax import lax
import numpy as np

D_MODEL = 1024
BATCH = 2
SEQ = 8192
DEPTH = 1

MEM_LEN = 256
EPS = 1e-6
CONV_K = 4
GDN_HEADS = 8
GDN_DK = 128
GDN_DV = 128
GDN_DIM = GDN_HEADS * GDN_DV
GDN_CHUNK = 64
GDN_COLS = 4 * GDN_DIM + 2 * GDN_HEADS
SSM_DIM = D_MODEL
SSM_HEADDIM = 64
SSM_HEADS = SSM_DIM // SSM_HEADDIM
SSM_GROUPS = 2
SSM_HPG = SSM_HEADS // SSM_GROUPS
SSM_STATE = 128
SSM_CHUNK = 128
SSM_BC = SSM_GROUPS * SSM_STATE
SSM_COLS = 2 * SSM_DIM + 2 * SSM_BC + SSM_HEADS
MIX_DIM = GDN_DIM + SSM_DIM
IN_COLS = GDN_COLS + SSM_COLS
MEM_HEADS = 4
MEM_HEADDIM = D_MODEL // MEM_HEADS
D_FF = 4 * D_MODEL

kernel_name = "hybrid_gdn_ssd_parallel_heads_memxattn"


def rmsnorm(x, w):
    xf = x.astype(jnp.float32)
    y = xf * lax.rsqrt(jnp.mean(xf * xf, axis=-1, keepdims=True) + EPS)
    return (y * w.astype(jnp.float32)).astype(x.dtype)


def l2norm(x):
    return x * lax.rsqrt(jnp.sum(x * x, axis=-1, keepdims=True) + EPS)


def causal_depthwise_conv(x, w, b=None):
    K, C = w.shape
    xp = jnp.pad(x, ((0, 0), (K - 1, 0), (0, 0)))
    y = lax.conv_general_dilated(xp, w[:, None, :].astype(x.dtype), window_strides=(1,),
                                 padding='VALID', dimension_numbers=('NWC', 'WIO', 'NWC'),
                                 feature_group_count=C)
    if b is not None:
        y = y + b.astype(x.dtype)
    return y


def gated_delta_rule_chunked(q, k, v, g, beta):
    Bsz, T, H, Dk = q.shape
    Dv = v.shape[-1]
    C = GDN_CHUNK
    N = T // C
    q = l2norm(q) * (Dk ** -0.5)
    k = l2norm(k)

    def chunks(t):
        return t.reshape(Bsz, N, C, H, -1).transpose(0, 3, 1, 2, 4)

    q, k, v = chunks(q), chunks(k), chunks(v)
    g = g.reshape(Bsz, N, C, H).transpose(0, 3, 1, 2)
    beta = beta.reshape(Bsz, N, C, H).transpose(0, 3, 1, 2)
    g_cs = jnp.cumsum(g, axis=-1)
    causal = jnp.tril(jnp.ones((C, C), dtype=bool))
    strict = jnp.tril(jnp.ones((C, C), dtype=bool), -1)
    decay = jnp.exp(jnp.where(causal, g_cs[..., :, None] - g_cs[..., None, :], -jnp.inf))
    k_beta = k * beta[..., None]
    v_beta = v * beta[..., None]
    Lmat = jnp.where(strict, jnp.einsum('bhncd,bhnsd->bhncs', k_beta, k) * decay, 0.0)
    Amat = Lmat + jnp.eye(C, dtype=jnp.float32)
    rhs = jnp.concatenate([v_beta, k_beta * jnp.exp(g_cs)[..., None]], axis=-1)
    sol = lax.linalg.triangular_solve(Amat, rhs, left_side=True, lower=True, unit_diagonal=True)
    u = sol[..., :Dv]
    w = sol[..., Dv:]
    attn_intra = jnp.einsum('bhncd,bhnsd->bhncs', q, k) * decay
    q_dec = q * jnp.exp(g_cs)[..., None]
    k_dec = k * jnp.exp(g_cs[..., -1:] - g_cs)[..., None]
    chunk_decay = jnp.exp(g_cs[..., -1])

    def step(S, inp):
        qd, kd, u_c, w_c, a_c, cd = inp
        v_new = u_c - jnp.einsum('bhcd,bhde->bhce', w_c, S)
        o = jnp.einsum('bhcd,bhde->bhce', qd, S) + jnp.einsum('bhcs,bhse->bhce', a_c, v_new)
        S = S * cd[..., None, None] + jnp.einsum('bhcd,bhce->bhde', kd, v_new)
        return S, o

    mv = lambda t: jnp.moveaxis(t, 2, 0)
    S0 = jnp.zeros((Bsz, H, Dk, Dv), jnp.float32)
    _, o = lax.scan(step, S0, (mv(q_dec), mv(k_dec), mv(u), mv(w), mv(attn_intra), mv(chunk_decay)))
    return o.transpose(1, 0, 3, 2, 4).reshape(Bsz, T, H, Dv)


def gdn_mixer(p, conv_w, a_log, dt_bias, norm_w):
    Bsz, T, _ = p.shape
    qkv = jax.nn.silu(causal_depthwise_conv(p[..., :3 * GDN_DIM], conv_w))
    z = p[..., 3 * GDN_DIM:4 * GDN_DIM]
    b_raw = p[..., 4 * GDN_DIM:4 * GDN_DIM + GDN_HEADS].astype(jnp.float32)
    a_raw = p[..., 4 * GDN_DIM + GDN_HEADS:].astype(jnp.float32)
    qkv = qkv.astype(jnp.float32)
    q = qkv[..., :GDN_DIM].reshape(Bsz, T, GDN_HEADS, GDN_DK)
    k = qkv[..., GDN_DIM:2 * GDN_DIM].reshape(Bsz, T, GDN_HEADS, GDN_DK)
    v = qkv[..., 2 * GDN_DIM:].reshape(Bsz, T, GDN_HEADS, GDN_DV)
    beta = jax.nn.sigmoid(b_raw)
    g = -jnp.exp(a_log.astype(jnp.float32)) * jax.nn.softplus(a_raw + dt_bias.astype(jnp.float32))
    o = gated_delta_rule_chunked(q, k, v, g, beta)
    zf = z.astype(jnp.float32).reshape(Bsz, T, GDN_HEADS, GDN_DV)
    o = rmsnorm(o, norm_w) * jax.nn.silu(zf)
    return o.reshape(Bsz, T, GDN_DIM).astype(p.dtype)


def ssd_chunked(x, dt, A, Bm, Cm):
    Bsz, T, H, P = x.shape
    L = SSD_L = SSM_CHUNK
    Nc = T // L
    G, J, N = SSM_GROUPS, SSM_HPG, SSM_STATE
    xdt = (x * dt[..., None]).reshape(Bsz, Nc, L, G, J, P)
    a = (dt * A).reshape(Bsz, Nc, L, H).transpose(0, 3, 1, 2)
    Bc = Bm.reshape(Bsz, Nc, L, G, N)
    Cc = Cm.reshape(Bsz, Nc, L, G, N)
    a_cs = jnp.cumsum(a, axis=-1)
    causal = jnp.tril(jnp.ones((L, L), dtype=bool))
    seg = jnp.exp(jnp.where(causal, a_cs[..., :, None] - a_cs[..., None, :], -jnp.inf))
    CB = jnp.einsum('bclgn,bcsgn->bgcls', Cc, Bc)
    scores = seg.reshape(Bsz, G, J, Nc, L, L) * CB[:, :, None]
    y_diag = jnp.einsum('bgjcls,bcsgjp->bclgjp', scores, xdt)
    decay_states = jnp.exp(a_cs[..., -1:] - a_cs).reshape(Bsz, G, J, Nc, L)
    states = jnp.einsum('bclgn,bgjcl,bclgjp->bcgjpn', Bc, decay_states, xdt)
    chunk_decay = jnp.exp(a_cs[..., -1]).reshape(Bsz, G, J, Nc)

    def step(h, inp):
        st, dec = inp
        return h * dec[..., None, None] + st, h

    h0 = jnp.zeros((Bsz, G, J, P, N), jnp.float32)
    _, prev = lax.scan(step, h0, (jnp.moveaxis(states, 1, 0), jnp.moveaxis(chunk_decay, 3, 0)))
    prev = jnp.moveaxis(prev, 0, 1)
    y_off = jnp.einsum('bclgn,bcgjpn,bgjcl->bclgjp', Cc, prev,
                       jnp.exp(a_cs).reshape(Bsz, G, J, Nc, L))
    return (y_diag + y_off).reshape(Bsz, T, H, P)


def ssd_mixer(p, conv_w, conv_b, a_log, dt_bias, d_skip, norm_w):
    Bsz, T, _ = p.shape
    z = p[..., :SSM_DIM]
    xbc = jax.nn.silu(causal_depthwise_conv(p[..., SSM_DIM:2 * SSM_DIM + 2 * SSM_BC], conv_w, conv_b))
    dt_raw = p[..., 2 * SSM_DIM + 2 * SSM_BC:].astype(jnp.float32)
    xbc = xbc.astype(jnp.float32)
    xs = xbc[..., :SSM_DIM].reshape(Bsz, T, SSM_HEADS, SSM_HEADDIM)
    Bm = xbc[..., SSM_DIM:SSM_DIM + SSM_BC].reshape(Bsz, T, SSM_GROUPS, SSM_STATE)
    Cm = xbc[..., SSM_DIM + SSM_BC:].reshape(Bsz, T, SSM_GROUPS, SSM_STATE)
    dt = jax.nn.softplus(dt_raw + dt_bias.astype(jnp.float32))
    A = -jnp.exp(a_log.astype(jnp.float32))
    y = ssd_chunked(xs, dt, A, Bm, Cm) + xs * d_skip.astype(jnp.float32)[:, None]
    yg = (y.reshape(Bsz, T, SSM_DIM) * jax.nn.silu(z.astype(jnp.float32)))
    yg = yg.reshape(Bsz, T, SSM_GROUPS, SSM_DIM // SSM_GROUPS)
    yg = yg * lax.rsqrt(jnp.mean(yg * yg, axis=-1, keepdims=True) + EPS)
    yg = yg.reshape(Bsz, T, SSM_DIM) * norm_w.astype(jnp.float32)
    return yg.astype(p.dtype)


def memory_cross_attention(h, m, wq, wk, wv, wo):
    Bsz, T, _ = h.shape
    M = m.shape[1]
    q = (h @ wq).reshape(Bsz, T, MEM_HEADS, MEM_HEADDIM)
    k = (m @ wk).reshape(Bsz, M, MEM_HEADS, MEM_HEADDIM)
    v = (m @ wv).reshape(Bsz, M, MEM_HEADS, MEM_HEADDIM)
    s = jnp.einsum('bthd,bmhd->bhtm', q, k).astype(jnp.float32) * (MEM_HEADDIM ** -0.5)
    pr = jax.nn.softmax(s, axis=-1).astype(v.dtype)
    o = jnp.einsum('bhtm,bmhd->bthd', pr, v).reshape(Bsz, T, D_MODEL)
    return o @ wo


def _inv_softplus_dt(key, n):
    dt = jnp.exp(jax.random.uniform(key, (n,), minval=math.log(1e-3), maxval=math.log(1e-1)))
    return dt + jnp.log(-jnp.expm1(-dt))


def setup_inputs(seed: int = 0) -> dict:
    key = jax.random.key(seed)
    ks = jax.random.split(key, 26)
    nrm = lambda k, shape, s: jax.random.normal(k, shape, jnp.float32) * s
    gain = lambda k, n: 1.0 + 0.02 * jax.random.normal(k, (n,), jnp.float32)
    return {
        "x": nrm(ks[0], (BATCH, SEQ, D_MODEL), 1.0),
        "mem": nrm(ks[1], (BATCH, MEM_LEN, D_MODEL), 1.0),
        "norm1_w": gain(ks[2], D_MODEL),
        "w_in": nrm(ks[3], (D_MODEL, IN_COLS), D_MODEL ** -0.5),
        "gdn_conv_w": nrm(ks[4], (CONV_K, 3 * GDN_DIM), CONV_K ** -0.5),
        "gdn_a_log": jnp.log(jax.random.uniform(ks[5], (GDN_HEADS,), minval=1.0, maxval=16.0)),
        "gdn_dt_bias": _inv_softplus_dt(ks[6], GDN_HEADS),
        "gdn_norm_w": gain(ks[7], GDN_DV),
        "ssm_conv_w": nrm(ks[8], (CONV_K, SSM_DIM + 2 * SSM_BC), CONV_K ** -0.5),
        "ssm_conv_b": nrm(ks[9], (SSM_DIM + 2 * SSM_BC,), 0.02),
        "ssm_a_log": jnp.log(jax.random.uniform(ks[10], (SSM_HEADS,), minval=1.0, maxval=16.0)),
        "ssm_dt_bias": _inv_softplus_dt(ks[11], SSM_HEADS),
        "ssm_d": gain(ks[12], SSM_HEADS),
        "ssm_norm_w": gain(ks[13], SSM_DIM),
        "w_out": nrm(ks[14], (MIX_DIM, D_MODEL), MIX_DIM ** -0.5),
        "norm2_w": gain(ks[15], D_MODEL),
        "mem_norm_w": gain(ks[16], D_MODEL),
        "wq_mem": nrm(ks[17], (D_MODEL, D_MODEL), D_MODEL ** -0.5),
        "wk_mem": nrm(ks[18], (D_MODEL, D_MODEL), D_MODEL ** -0.5),
        "wv_mem": nrm(ks[19], (D_MODEL, D_MODEL), D_MODEL ** -0.5),
        "wo_mem": nrm(ks[20], (D_MODEL, D_MODEL), D_MODEL ** -0.5),
        "norm3_w": gain(ks[21], D_MODEL),
        "w_up": nrm(ks[22], (D_MODEL, D_FF), D_MODEL ** -0.5),
        "w_down": nrm(ks[23], (D_FF, D_MODEL), D_FF ** -0.5),
        "final_norm_w": gain(ks[24], D_MODEL),
    }


def reference(x, mem, norm1_w, w_in, gdn_conv_w, gdn_a_log, gdn_dt_bias, gdn_norm_w,
              ssm_conv_w, ssm_conv_b, ssm_a_log, ssm_dt_bias, ssm_d, ssm_norm_w, w_out,
              norm2_w, mem_norm_w, wq_mem, wk_mem, wv_mem, wo_mem, norm3_w, w_up, w_down,
              final_norm_w):
    m = rmsnorm(mem, mem_norm_w)
    for _ in range(DEPTH):
        h = rmsnorm(x, norm1_w)
        p = h @ w_in
        o_a = gdn_mixer(p[..., :GDN_COLS], gdn_conv_w, gdn_a_log, gdn_dt_bias, gdn_norm_w)
        o_b = ssd_mixer(p[..., GDN_COLS:], ssm_conv_w, ssm_conv_b, ssm_a_log, ssm_dt_bias,
                        ssm_d, ssm_norm_w)
        x = x + jnp.concatenate([o_a, o_b], axis=-1) @ w_out
        x = x + memory_cross_attention(rmsnorm(x, norm2_w), m, wq_mem, wk_mem, wv_mem, wo_mem)
        u = jax.nn.relu(rmsnorm(x, norm3_w) @ w_up)
        x = x + (u * u) @ w_down
    return rmsnorm(x, final_norm_w)
```

```python
import functools
import math

import jax
import jax.numpy as jnp
from jax import lax
from jax.experimental import pallas as pl
from jax.experimental.pallas import tpu as pltpu

D_MODEL = 1024
EPS = 1e-6
CONV_K = 4
GDN_HEADS = 8
GDN_DK = 128
GDN_DV = 128
GDN_DIM = GDN_HEADS * GDN_DV
GDN_CHUNK = 64
SSM_DIM = D_MODEL
SSM_HEADDIM = 64
SSM_HEADS = SSM_DIM // SSM_HEADDIM
SSM_GROUPS = 2
SSM_HPG = SSM_HEADS // SSM_GROUPS
SSM_STATE = 128
SSM_CHUNK = 128
SSM_BC = SSM_GROUPS * SSM_STATE
SSM_XBC = SSM_DIM + 2 * SSM_BC
SSM_GROUP_DIM = SSM_DIM // SSM_GROUPS
MEM_HEADS = 4
MEM_HEADDIM = D_MODEL // MEM_HEADS
D_FF = 4 * D_MODEL

F32 = jnp.float32
BF16 = jnp.bfloat16

V7X_VMEM_BYTES = 64 * 1024 * 1024
VMEM_LIMIT_BYTES = V7X_VMEM_BYTES - 8 * 1024 * 1024
SUBLANES = 8

IN_ROWS = 512
GDN_ROWS = 256
SSD_ROWS = 512
TAIL_ROWS = 512
COL_GROUP = 512
FF_GROUP = 1024


def _mm(a, b):
    return jnp.dot(a.astype(BF16), b.astype(BF16), preferred_element_type=F32)


def _mm_nt(a, b):
    return lax.dot_general(a.astype(BF16), b.astype(BF16), (((1,), (1,)), ((), ())),
                           preferred_element_type=F32)


def _mm_tn(a, b):
    return lax.dot_general(a.astype(BF16), b.astype(BF16), (((0,), (0,)), ((), ())),
                           preferred_element_type=F32)


def _mm_f32(a, b):
    return jnp.dot(a, b, preferred_element_type=F32, precision=lax.Precision.HIGHEST)


def _rmsnorm(x, w):
    return x * lax.rsqrt(jnp.mean(x * x, axis=-1, keepdims=True) + EPS) * w


def _sigmoid(x):
    return 1.0 / (1.0 + jnp.exp(-x))


def _silu(x):
    return x * _sigmoid(x)


def _softplus(x):
    return jnp.maximum(x, 0.0) + jnp.log(1.0 + jnp.exp(-jnp.abs(x)))


def _const_spec(shape):
    zeros = (0,) * len(shape)
    return pl.BlockSpec(shape, lambda *_: zeros, pipeline_mode=pl.Buffered(1))


def _conv_silu(p, tail, w, bias):
    rows = p.shape[0]
    row_id = lax.broadcasted_iota(jnp.int32, (SUBLANES, p.shape[1]), 0)
    acc = p * w[CONV_K - 1:CONV_K, :]
    if bias is not None:
        acc = acc + bias
    for k in range(1, CONV_K):
        shifted = pltpu.roll(p, k, 0)
        tail_k = pltpu.roll(tail, k, 0)
        first = jnp.where(row_id < k, tail_k, shifted[:SUBLANES])
        shifted = jnp.concatenate([first, shifted[SUBLANES:]], axis=0)
        acc = acc + shifted * w[CONV_K - 1 - k:CONV_K - k, :]
    del rows
    return _silu(acc)


def _in_proj_kernel(x_ref, n1_ref, wqkv_ref, wgz_ref, wgc_ref, wgr_ref, wsz_ref, wxbc_ref,
                    wdc_ref, wdr_ref, gconv_ref, sconv_ref, sconvb_ref,
                    galog_c_ref, gbias_c_ref, galog_r_ref, gbias_r_ref, dbias_c_ref, dbias_r_ref,
                    qkv_out, gz_out, gcol_out, grow_out, sz_out, xbc_out, dcol_out, drow_out,
                    gtail_ref, stail_ref):
    rows = x_ref.shape[1]

    @pl.when(pl.program_id(1) == 0)
    def _():
        gtail_ref[...] = jnp.zeros_like(gtail_ref)
        stail_ref[...] = jnp.zeros_like(stail_ref)

    h = _rmsnorm(x_ref[0], n1_ref[...]).astype(BF16)

    q_scale = GDN_DK ** -0.5
    for c0 in range(0, 3 * GDN_DIM, COL_GROUP):
        cs = slice(c0, c0 + COL_GROUP)
        p = jnp.dot(h, wqkv_ref[:, cs], preferred_element_type=F32)
        y = _conv_silu(p, gtail_ref[:, cs], gconv_ref[:, cs], None)
        gtail_ref[:, cs] = p[rows - SUBLANES:]
        if c0 < 2 * GDN_DIM:
            scale = q_scale if c0 < GDN_DIM else 1.0
            for j in range(COL_GROUP // GDN_DK):
                yh = y[:, j * GDN_DK:(j + 1) * GDN_DK]
                inv = lax.rsqrt(jnp.sum(yh * yh, axis=-1, keepdims=True) + EPS)
                lo = c0 + j * GDN_DK
                qkv_out[0, :, lo:lo + GDN_DK] = yh * inv * scale if scale != 1.0 else yh * inv
        else:
            qkv_out[0, :, cs] = y

    for c0 in range(0, GDN_DIM, COL_GROUP):
        cs = slice(c0, c0 + COL_GROUP)
        gz_out[0, :, cs] = _silu(jnp.dot(h, wgz_ref[:, cs], preferred_element_type=F32))
        sz_out[0, :, cs] = _silu(jnp.dot(h, wsz_ref[:, cs], preferred_element_type=F32))

    for c0 in range(0, SSM_XBC, COL_GROUP):
        cs = slice(c0, c0 + COL_GROUP)
        p = jnp.dot(h, wxbc_ref[:, cs], preferred_element_type=F32)
        xbc_out[0, :, cs] = _conv_silu(p, stail_ref[:, cs], sconv_ref[:, cs], sconvb_ref[:, cs])
        stail_ref[:, cs] = p[rows - SUBLANES:]

    pg = jnp.dot(h, wgc_ref[...], preferred_element_type=F32)
    lane = lax.broadcasted_iota(jnp.int32, pg.shape, 1)
    g_val = -jnp.exp(galog_c_ref[...]) * _softplus(pg + gbias_c_ref[...])
    gcol_out[0] = jnp.where(lane < GDN_HEADS, _sigmoid(pg), g_val)
    pgt = lax.dot_general(wgr_ref[...], h, (((1,), (1,)), ((), ())), preferred_element_type=F32)
    sub = lax.broadcasted_iota(jnp.int32, pgt.shape, 0)
    g_val_t = -jnp.exp(galog_r_ref[...]) * _softplus(pgt + gbias_r_ref[...])
    grow_out[0] = jnp.where(sub < GDN_HEADS, _sigmoid(pgt), g_val_t)

    pd = jnp.dot(h, wdc_ref[...], preferred_element_type=F32)
    dcol_out[0] = _softplus(pd + dbias_c_ref[...])
    pdt = lax.dot_general(wdr_ref[...], h, (((1,), (1,)), ((), ())), preferred_element_type=F32)
    drow_out[0] = _softplus(pdt + dbias_r_ref[...])


def _in_proj(x, norm1_w, w_in, gdn_conv_w, gdn_a_log, gdn_dt_bias, ssm_conv_w, ssm_conv_b,
             ssm_dt_bias):
    bsz, seq, _ = x.shape
    rows = min(IN_ROWS, seq)
    nh = GDN_HEADS
    o = 0
    w_qkv = w_in[:, o:o + 3 * GDN_DIM].astype(BF16); o += 3 * GDN_DIM
    w_gz = w_in[:, o:o + GDN_DIM].astype(BF16); o += GDN_DIM
    w_gates = w_in[:, o:o + 2 * nh].astype(BF16); o += 2 * nh
    w_sz = w_in[:, o:o + SSM_DIM].astype(BF16); o += SSM_DIM
    w_xbc = w_in[:, o:o + SSM_XBC].astype(BF16); o += SSM_XBC
    w_dt = w_in[:, o:o + SSM_HEADS].astype(BF16); o += SSM_HEADS
    assert o == w_in.shape[1]
    zeros8 = jnp.zeros((nh,), F32)
    galog16 = jnp.concatenate([zeros8, gdn_a_log.astype(F32)])
    gbias16 = jnp.concatenate([zeros8, gdn_dt_bias.astype(F32)])
    dbias = ssm_dt_bias.astype(F32)

    args = [
        x, norm1_w.reshape(1, D_MODEL), w_qkv, w_gz, w_gates, w_gates.T, w_sz, w_xbc, w_dt, w_dt.T,
        gdn_conv_w, ssm_conv_w, ssm_conv_b.reshape(1, SSM_XBC),
        galog16.reshape(1, 2 * nh), gbias16.reshape(1, 2 * nh),
        galog16.reshape(2 * nh, 1), gbias16.reshape(2 * nh, 1),
        dbias.reshape(1, SSM_HEADS), dbias.reshape(SSM_HEADS, 1),
    ]
    in_specs = [pl.BlockSpec((1, rows, D_MODEL), lambda b, t: (b, t, 0))]
    in_specs += [_const_spec(a.shape) for a in args[1:]]

    def row_spec(cols):
        return pl.BlockSpec((1, rows, cols), lambda b, t: (b, t, 0))

    def col_major_spec(nrows):
        return pl.BlockSpec((1, nrows, rows), lambda b, t: (b, 0, t))

    out_shape = [
        jax.ShapeDtypeStruct((bsz, seq, 3 * GDN_DIM), F32),
        jax.ShapeDtypeStruct((bsz, seq, GDN_DIM), F32),
        jax.ShapeDtypeStruct((bsz, seq, 2 * nh), F32),
        jax.ShapeDtypeStruct((bsz, 2 * nh, seq), F32),
        jax.ShapeDtypeStruct((bsz, seq, SSM_DIM), F32),
        jax.ShapeDtypeStruct((bsz, seq, SSM_XBC), F32),
        jax.ShapeDtypeStruct((bsz, seq, SSM_HEADS), F32),
        jax.ShapeDtypeStruct((bsz, SSM_HEADS, seq), F32),
    ]
    out_specs = [row_spec(3 * GDN_DIM), row_spec(GDN_DIM), row_spec(2 * nh), col_major_spec(2 * nh),
                 row_spec(SSM_DIM), row_spec(SSM_XBC), row_spec(SSM_HEADS),
                 col_major_spec(SSM_HEADS)]
    return pl.pallas_call(
        _in_proj_kernel,
        grid=(bsz, seq // rows),
        in_specs=in_specs,
        out_specs=out_specs,
        out_shape=out_shape,
        scratch_shapes=[pltpu.VMEM((SUBLANES, 3 * GDN_DIM), F32),
                        pltpu.VMEM((SUBLANES, SSM_XBC), F32)],
        compiler_params=pltpu.CompilerParams(
            dimension_semantics=("arbitrary", "arbitrary"), vmem_limit_bytes=VMEM_LIMIT_BYTES),
        name="in_proj",
    )(*args)


def _unit_lower_inverse(a_strict):
    n = a_strict.shape[0]
    ri = lax.broadcasted_iota(jnp.int32, (n, n), 0)
    ci = lax.broadcasted_iota(jnp.int32, (n, n), 1)
    same16 = (ri // 16) == (ci // 16)
    same32 = (ri // 32) == (ci // 32)
    eye = (ri == ci).astype(F32)
    d = jnp.where(same16, a_strict, 0.0)
    e32 = jnp.where(jnp.logical_and(same32, jnp.logical_not(same16)), a_strict, 0.0)
    e64 = jnp.where(same32, 0.0, a_strict)
    inv = eye - d
    xp = _mm(d, d)
    inv = inv + _mm(inv, xp)
    xp = _mm(xp, xp)
    inv = inv + _mm(inv, xp)
    xp = _mm(xp, xp)
    inv = inv + _mm(inv, xp)
    inv = inv - _mm(_mm(inv, e32), inv)
    inv = inv - _mm(_mm(inv, e64), inv)
    return inv


def _gdn_kernel(qkv_ref, gz_ref, gcol_ref, grow_ref, nw_ref, o_ref, s_ref):
    rows = qkv_ref.shape[1]
    c = GDN_CHUNK
    nchunks = rows // c

    @pl.when(pl.program_id(1) == 0)
    def _():
        s_ref[...] = jnp.zeros_like(s_ref)

    ri = lax.broadcasted_iota(jnp.int32, (c, c), 0)
    ci = lax.broadcasted_iota(jnp.int32, (c, c), 1)
    causal = ri >= ci
    strict = ri > ci
    tril = causal.astype(F32)
    triu = (ri <= ci).astype(F32)
    norm_w = nw_ref[...]

    def chunk_body(n, carry):
        r0 = pl.multiple_of(n * c, c)
        gcol = gcol_ref[0, pl.ds(r0, c), :]
        grow = grow_ref[0, n]
        gcs_col = _mm_f32(tril, gcol)
        gcs_row = _mm_f32(grow, triu)
        for h in range(GDN_HEADS):
            hs = slice(h * GDN_DK, (h + 1) * GDN_DK)
            q = qkv_ref[0, pl.ds(r0, c), hs]
            k = qkv_ref[0, pl.ds(r0, c), GDN_DIM + h * GDN_DK:GDN_DIM + (h + 1) * GDN_DK]
            v = qkv_ref[0, pl.ds(r0, c), 2 * GDN_DIM + h * GDN_DV:2 * GDN_DIM + (h + 1) * GDN_DV]
            beta = gcol[:, h:h + 1]
            gc = gcs_col[:, GDN_HEADS + h:GDN_HEADS + h + 1]
            gr = gcs_row[GDN_HEADS + h:GDN_HEADS + h + 1, :]
            g_last = gc[c - 1:c, :]
            decay = jnp.exp(jnp.where(causal, gc - gr, -jnp.inf))
            exp_g = jnp.exp(gc)
            kb = k * beta
            vb = v * beta
            gram = _mm_nt(jnp.concatenate([kb, q], axis=0), k)
            a_strict = jnp.where(strict, gram[:c] * decay, 0.0)
            attn = gram[c:] * decay
            t_inv = _unit_lower_inverse(a_strict)
            sol = _mm(t_inv, jnp.concatenate([vb, kb * exp_g], axis=1))
            u = sol[:, :GDN_DV]
            w = sol[:, GDN_DV:]
            state = s_ref[h]
            ws_qs = _mm(jnp.concatenate([w, q * exp_g], axis=0), state)
            v_new = u - ws_qs[:c]
            o = ws_qs[c:] + _mm(attn, v_new)
            k_dec = k * jnp.exp(g_last - gc)
            s_ref[h] = state * jnp.exp(g_last) + _mm_tn(k_dec, v_new)
            o = _rmsnorm(o, norm_w) * gz_ref[0, pl.ds(r0, c), hs]
            o_ref[0, pl.ds(r0, c), hs] = o.astype(o_ref.dtype)
        return carry

    lax.fori_loop(0, nchunks, chunk_body, 0)


def _gdn(qkv, gz, gcol, grow, norm_w):
    bsz, seq, _ = qkv.shape
    rows = min(GDN_ROWS, seq)
    cpb = rows // GDN_CHUNK
    nh2 = 2 * GDN_HEADS
    grow_c = grow.reshape(bsz, nh2, seq // GDN_CHUNK, GDN_CHUNK).transpose(0, 2, 1, 3)
    return pl.pallas_call(
        _gdn_kernel,
        grid=(bsz, seq // rows),
        in_specs=[
            pl.BlockSpec((1, rows, 3 * GDN_DIM), lambda b, t: (b, t, 0)),
            pl.BlockSpec((1, rows, GDN_DIM), lambda b, t: (b, t, 0)),
            pl.BlockSpec((1, rows, nh2), lambda b, t: (b, t, 0)),
            pl.BlockSpec((1, cpb, nh2, GDN_CHUNK), lambda b, t: (b, t, 0, 0)),
            _const_spec((1, GDN_DV)),
        ],
        out_specs=pl.BlockSpec((1, rows, GDN_DIM), lambda b, t: (b, t, 0)),
        out_shape=jax.ShapeDtypeStruct((bsz, seq, GDN_DIM), BF16),
        scratch_shapes=[pltpu.VMEM((GDN_HEADS, GDN_DK, GDN_DV), F32)],
        compiler_params=pltpu.CompilerParams(
            dimension_semantics=("arbitrary", "arbitrary"), vmem_limit_bytes=VMEM_LIMIT_BYTES),
        name="gdn",
    )(qkv, gz, gcol, grow_c, norm_w.reshape(1, GDN_DV).astype(F32))


def _ssd_kernel(xbc_ref, sz_ref, dcol_ref, drow_ref, a_c_ref, a_r_ref, expand_ref, dskip_ref,
                nw_ref, o_ref, h_ref):
    rows = xbc_ref.shape[1]
    c = SSM_CHUNK
    nchunks = rows // c

    @pl.when(pl.program_id(1) == 0)
    def _():
        h_ref[...] = jnp.zeros_like(h_ref)

    ri = lax.broadcasted_iota(jnp.int32, (c, c), 0)
    ci = lax.broadcasted_iota(jnp.int32, (c, c), 1)
    causal = ri >= ci
    tril = causal.astype(F32)
    triu = (ri <= ci).astype(F32)
    lane = lax.broadcasted_iota(jnp.int32, (c, 2 * SSM_HEADDIM), 1)
    low_half = lane < SSM_HEADDIM
    expand = expand_ref[...]
    gd = SSM_GROUP_DIM

    def chunk_body(n, carry):
        r0 = pl.multiple_of(n * c, c)
        xs = xbc_ref[0, pl.ds(r0, c), 0:SSM_DIM]
        dt_col = dcol_ref[0, pl.ds(r0, c), :]
        dt_row = drow_ref[0, n]
        acs_col = _mm_f32(tril, dt_col * a_c_ref[...])
        acs_row = _mm_f32(dt_row * a_r_ref[...], triu)
        a_last = acs_col[c - 1:c, :]
        stacked = jnp.concatenate([dt_col, jnp.exp(acs_col), jnp.exp(a_last - acs_col)], axis=0)
        full = _mm_f32(stacked, expand)
        chunk_decay = _mm_f32(jnp.exp(a_last), expand)
        xdt = xs * full[:c]
        exp_a = full[c:2 * c]
        x_dec = xdt * full[2 * c:]
        y_parts = []
        for g in range(SSM_GROUPS):
            bg = xbc_ref[0, pl.ds(r0, c), SSM_DIM + g * SSM_STATE:SSM_DIM + (g + 1) * SSM_STATE]
            cg = xbc_ref[0, pl.ds(r0, c),
                         SSM_DIM + SSM_BC + g * SSM_STATE:SSM_DIM + SSM_BC + (g + 1) * SSM_STATE]
            cb = _mm_nt(cg, bg)
            gs = slice(g * gd, (g + 1) * gd)
            h_prev = h_ref[g]
            y_g = _mm(cg, h_prev) * exp_a[:, gs]
            h_ref[g] = h_prev * chunk_decay[:, gs] + _mm_tn(bg, x_dec[:, gs])
            pair_outs = []
            for jp in range(SSM_HPG // 2):
                lo = g * gd + jp * 2 * SSM_HEADDIM
                x_pair = xdt[:, lo:lo + 2 * SSM_HEADDIM]
                acc = None
                for half in range(2):
                    hd = g * SSM_HPG + 2 * jp + half
                    ac = acs_col[:, hd:hd + 1]
                    ar = acs_row[hd:hd + 1, :]
                    seg = jnp.exp(jnp.where(causal, ac - ar, -jnp.inf))
                    rhs = jnp.where(low_half if half == 0 else jnp.logical_not(low_half),
                                    x_pair, 0.0)
                    part = _mm(seg * cb, rhs)
                    acc = part if acc is None else acc + part
                pair_outs.append(acc)
            y_parts.append(y_g + jnp.concatenate(pair_outs, axis=1))
        xs_skip = xs * dskip_ref[...]
        for g in range(SSM_GROUPS):
            gs = slice(g * gd, (g + 1) * gd)
            yg = (y_parts[g] + xs_skip[:, gs]) * sz_ref[0, pl.ds(r0, c), gs]
            yg = yg * lax.rsqrt(jnp.mean(yg * yg, axis=-1, keepdims=True) + EPS)
            o_ref[0, pl.ds(r0, c), gs] = (yg * nw_ref[:, gs]).astype(o_ref.dtype)
        return carry

    lax.fori_loop(0, nchunks, chunk_body, 0)


def _ssd(xbc, sz, dcol, drow, a_log, d_skip, norm_w):
    bsz, seq, _ = xbc.shape
    rows = min(SSD_ROWS, seq)
    cpb = rows // SSM_CHUNK
    nh = SSM_HEADS
    drow_c = drow.reshape(bsz, nh, seq // SSM_CHUNK, SSM_CHUNK).transpose(0, 2, 1, 3)
    a_neg = -jnp.exp(a_log.astype(F32))
    expand = jnp.repeat(jnp.eye(nh, dtype=F32), SSM_HEADDIM, axis=1)
    dskip = jnp.repeat(d_skip.astype(F32), SSM_HEADDIM).reshape(1, SSM_DIM)
    return pl.pallas_call(
        _ssd_kernel,
        grid=(bsz, seq // rows),
        in_specs=[
            pl.BlockSpec((1, rows, SSM_XBC), lambda b, t: (b, t, 0)),
            pl.BlockSpec((1, rows, SSM_DIM), lambda b, t: (b, t, 0)),
            pl.BlockSpec((1, rows, nh), lambda b, t: (b, t, 0)),
            pl.BlockSpec((1, cpb, nh, SSM_CHUNK), lambda b, t: (b, t, 0, 0)),
            _const_spec((1, nh)),
            _const_spec((nh, 1)),
            _const_spec((nh, SSM_DIM)),
            _const_spec((1, SSM_DIM)),
            _const_spec((1, SSM_DIM)),
        ],
        out_specs=pl.BlockSpec((1, rows, SSM_DIM), lambda b, t: (b, t, 0)),
        out_shape=jax.ShapeDtypeStruct((bsz, seq, SSM_DIM), BF16),
        scratch_shapes=[pltpu.VMEM((SSM_GROUPS, SSM_STATE, SSM_GROUP_DIM), F32)],
        compiler_params=pltpu.CompilerParams(
            dimension_semantics=("arbitrary", "arbitrary"), vmem_limit_bytes=VMEM_LIMIT_BYTES),
        name="ssd",
    )(xbc, sz, dcol, drow_c, a_neg.reshape(1, nh), a_neg.reshape(nh, 1), expand, dskip,
      norm_w.reshape(1, SSM_DIM).astype(F32))


def _mem_kv_kernel(mem_ref, nw_ref, wk_ref, wv_ref, k_out, v_out):
    m = _rmsnorm(mem_ref[0], nw_ref[...]).astype(BF16)
    k_out[0] = jnp.dot(m, wk_ref[...], preferred_element_type=F32).astype(k_out.dtype)
    v_out[0] = jnp.dot(m, wv_ref[...], preferred_element_type=F32).astype(v_out.dtype)


def _mem_kv(mem, mem_norm_w, wk, wv):
    bsz, mlen, _ = mem.shape
    kv_shape = jax.ShapeDtypeStruct((bsz, mlen, D_MODEL), BF16)
    return pl.pallas_call(
        _mem_kv_kernel,
        grid=(bsz,),
        in_specs=[pl.BlockSpec((1, mlen, D_MODEL), lambda b: (b, 0, 0)),
                  _const_spec((1, D_MODEL)),
                  _const_spec((D_MODEL, D_MODEL)),
                  _const_spec((D_MODEL, D_MODEL))],
        out_specs=[pl.BlockSpec((1, mlen, D_MODEL), lambda b: (b, 0, 0))] * 2,
        out_shape=[kv_shape, kv_shape],
        compiler_params=pltpu.CompilerParams(
            dimension_semantics=("arbitrary",), vmem_limit_bytes=VMEM_LIMIT_BYTES),
        name="mem_kv",
    )(mem, mem_norm_w.reshape(1, D_MODEL), wk.astype(BF16), wv.astype(BF16))


def _tail_kernel(x_ref, oa_ref, ob_ref, k_ref, v_ref, wout_ref, n2_ref, wq_ref, wo_ref, n3_ref,
                 wup_ref, wdown_ref, nf_ref, y_ref):
    x = x_ref[0]
    x = x + jnp.dot(oa_ref[0], wout_ref[:GDN_DIM, :], preferred_element_type=F32)
    x = x + jnp.dot(ob_ref[0], wout_ref[GDN_DIM:, :], preferred_element_type=F32)

    h = _rmsnorm(x, n2_ref[...]).astype(BF16)
    q = jnp.dot(h, wq_ref[...], preferred_element_type=F32)
    scale = MEM_HEADDIM ** -0.5
    heads = []
    for hd in range(MEM_HEADS):
        hs = slice(hd * MEM_HEADDIM, (hd + 1) * MEM_HEADDIM)
        s = _mm_nt(q[:, hs], k_ref[0, :, hs]) * scale
        s = s - jnp.max(s, axis=-1, keepdims=True)
        p = jnp.exp(s)
        p = p / jnp.sum(p, axis=-1, keepdims=True)
        heads.append(_mm(p, v_ref[0, :, hs]))
    attn = jnp.concatenate(heads, axis=1)
    x = x + _mm(attn, wo_ref[...])

    h = _rmsnorm(x, n3_ref[...]).astype(BF16)
    acc = x
    for c0 in range(0, D_FF, FF_GROUP):
        u = jnp.maximum(jnp.dot(h, wup_ref[:, c0:c0 + FF_GROUP], preferred_element_type=F32), 0.0)
        acc = acc + _mm(u * u, wdown_ref[c0:c0 + FF_GROUP, :])
    y_ref[0] = _rmsnorm(acc, nf_ref[...])


def _tail(x, o_a, o_b, k_mem, v_mem, w_out, norm2_w, wq, wo, norm3_w, w_up, w_down, final_w):
    bsz, seq, _ = x.shape
    rows = min(TAIL_ROWS, seq)
    mlen = k_mem.shape[1]
    row_spec = pl.BlockSpec((1, rows, D_MODEL), lambda b, t: (b, t, 0))
    mem_spec = pl.BlockSpec((1, mlen, D_MODEL), lambda b, t: (b, 0, 0))
    vec = lambda w: w.reshape(1, D_MODEL).astype(F32)
    return pl.pallas_call(
        _tail_kernel,
        grid=(bsz, seq // rows),
        in_specs=[row_spec, row_spec, row_spec, mem_spec, mem_spec,
                  _const_spec((GDN_DIM + SSM_DIM, D_MODEL)), _const_spec((1, D_MODEL)),
                  _const_spec((D_MODEL, D_MODEL)), _const_spec((D_MODEL, D_MODEL)),
                  _const_spec((1, D_MODEL)), _const_spec((D_MODEL, D_FF)),
                  _const_spec((D_FF, D_MODEL)), _const_spec((1, D_MODEL))],
        out_specs=row_spec,
        out_shape=jax.ShapeDtypeStruct((bsz, seq, D_MODEL), x.dtype),
        compiler_params=pltpu.CompilerParams(
            dimension_semantics=("arbitrary", "arbitrary"), vmem_limit_bytes=VMEM_LIMIT_BYTES),
        name="tail",
    )(x, o_a, o_b, k_mem, v_mem, w_out.astype(BF16), vec(norm2_w), wq.astype(BF16),
      wo.astype(BF16), vec(norm3_w), w_up.astype(BF16), w_down.astype(BF16), vec(final_w))


def kernel(x, mem, norm1_w, w_in, gdn_conv_w, gdn_a_log, gdn_dt_bias, gdn_norm_w, ssm_conv_w,
           ssm_conv_b, ssm_a_log, ssm_dt_bias, ssm_d, ssm_norm_w, w_out, norm2_w, mem_norm_w,
           wq_mem, wk_mem, wv_mem, wo_mem, norm3_w, w_up, w_down, final_norm_w):
    qkv, gz, gcol, grow, sz, xbc, dcol, drow = _in_proj(
        x, norm1_w, w_in, gdn_conv_w, gdn_a_log, gdn_dt_bias, ssm_conv_w, ssm_conv_b, ssm_dt_bias)
    o_a = _gdn(qkv, gz, gcol, grow, gdn_norm_w)
    o_b = _ssd(xbc, sz, dcol, drow, ssm_a_log, ssm_d, ssm_norm_w)
    k_mem, v_mem = _mem_kv(mem, mem_norm_w, wk_mem, wv_mem)
    return _tail(x, o_a, o_b, k_mem, v_mem, w_out, norm2_w, wq_mem, wo_mem, norm3_w, w_up, w_down,
                 final_norm_w)
```

```python
import functools
import math

import jax
import jax.numpy as jnp
from jax import lax
from jax.experimental import pallas as pl
from jax.experimental.pallas import tpu as pltpu

D_MODEL = 1024
EPS = 1e-6
CONV_K = 4
GDN_HEADS = 8
GDN_DK = 128
GDN_DV = 128
GDN_DIM = GDN_HEADS * GDN_DV
GDN_CHUNK = 64
SSM_DIM = D_MODEL
SSM_HEADDIM = 64
SSM_HEADS = SSM_DIM // SSM_HEADDIM
SSM_GROUPS = 2
SSM_HPG = SSM_HEADS // SSM_GROUPS
SSM_STATE = 128
SSM_CHUNK = 128
SSM_BC = SSM_GROUPS * SSM_STATE
SSM_XBC = SSM_DIM + 2 * SSM_BC
SSM_GROUP_DIM = SSM_DIM // SSM_GROUPS
MEM_HEADS = 4
MEM_HEADDIM = D_MODEL // MEM_HEADS
D_FF = 4 * D_MODEL

F32 = jnp.float32
BF16 = jnp.bfloat16

V7X_VMEM_BYTES = 64 * 1024 * 1024
VMEM_LIMIT_BYTES = V7X_VMEM_BYTES - 8 * 1024 * 1024
SUBLANES = 8

IN_ROWS = 512
GDN_ROWS = 256
SSD_ROWS = 512
TAIL_ROWS = 512
COL_GROUP = 512
FF_GROUP = 1024


def _mm(a, b):
    return jnp.dot(a.astype(BF16), b.astype(BF16), preferred_element_type=F32)


def _mm_nt(a, b):
    return lax.dot_general(a.astype(BF16), b.astype(BF16), (((1,), (1,)), ((), ())),
                           preferred_element_type=F32)


def _mm_tn(a, b):
    return lax.dot_general(a.astype(BF16), b.astype(BF16), (((0,), (0,)), ((), ())),
                           preferred_element_type=F32)


def _mm_f32(a, b):
    return jnp.dot(a, b, preferred_element_type=F32, precision=lax.Precision.HIGHEST)


def _rmsnorm(x, w):
    return x * lax.rsqrt(jnp.mean(x * x, axis=-1, keepdims=True) + EPS) * w


def _sigmoid(x):
    return 1.0 / (1.0 + jnp.exp(-x))


def _silu(x):
    return x * _sigmoid(x)


def _softplus(x):
    return jnp.maximum(x, 0.0) + jnp.log(1.0 + jnp.exp(-jnp.abs(x)))


def _const_spec(shape):
    zeros = (0,) * len(shape)
    return pl.BlockSpec(shape, lambda *_: zeros, pipeline_mode=pl.Buffered(1))


def _conv_silu(p, tail, w, bias):
    rows = p.shape[0]
    row_id = lax.broadcasted_iota(jnp.int32, (SUBLANES, p.shape[1]), 0)
    acc = p * w[CONV_K - 1:CONV_K, :]
    if bias is not None:
        acc = acc + bias
    for k in range(1, CONV_K):
        shifted = pltpu.roll(p, k, 0)
        tail_k = pltpu.roll(tail, k, 0)
        first = jnp.where(row_id < k, tail_k, shifted[:SUBLANES])
        shifted = jnp.concatenate([first, shifted[SUBLANES:]], axis=0)
        acc = acc + shifted * w[CONV_K - 1 - k:CONV_K - k, :]
    del rows
    return _silu(acc)


def _in_proj_kernel(x_ref, n1_ref, wqkv_ref, wgz_ref, wgc_ref, wgr_ref, wsz_ref, wxbc_ref,
                    wdc_ref, wdr_ref, gconv_ref, sconv_ref, sconvb_ref,
                    galog_c_ref, gbias_c_ref, galog_r_ref, gbias_r_ref, dbias_c_ref, dbias_r_ref,
                    ssm_a_c_ref, ssm_a_r_ref,
                    qkv_out, gz_out, gcol_out, grow_out, sz_out, xbc_out, dcol_out, drow_out,
                    acol_out, arow_out, gtail_ref, stail_ref):
    rows = x_ref.shape[1]

    @pl.when(pl.program_id(1) == 0)
    def _():
        gtail_ref[...] = jnp.zeros_like(gtail_ref)
        stail_ref[...] = jnp.zeros_like(stail_ref)

    h = _rmsnorm(x_ref[0], n1_ref[...]).astype(BF16)

    q_scale = GDN_DK ** -0.5
    for c0 in range(0, 3 * GDN_DIM, COL_GROUP):
        cs = slice(c0, c0 + COL_GROUP)
        p = jnp.dot(h, wqkv_ref[:, cs], preferred_element_type=F32)
        y = _conv_silu(p, gtail_ref[:, cs], gconv_ref[:, cs], None)
        gtail_ref[:, cs] = p[rows - SUBLANES:]
        if c0 < 2 * GDN_DIM:
            scale = q_scale if c0 < GDN_DIM else 1.0
            for j in range(COL_GROUP // GDN_DK):
                yh = y[:, j * GDN_DK:(j + 1) * GDN_DK]
                inv = lax.rsqrt(jnp.sum(yh * yh, axis=-1, keepdims=True) + EPS)
                lo = c0 + j * GDN_DK
                qkv_out[0, :, lo:lo + GDN_DK] = yh * inv * scale if scale != 1.0 else yh * inv
        else:
            qkv_out[0, :, cs] = y

    for c0 in range(0, GDN_DIM, COL_GROUP):
        cs = slice(c0, c0 + COL_GROUP)
        gz_out[0, :, cs] = _silu(jnp.dot(h, wgz_ref[:, cs], preferred_element_type=F32))
        sz_out[0, :, cs] = _silu(jnp.dot(h, wsz_ref[:, cs], preferred_element_type=F32))

    for c0 in range(0, SSM_XBC, COL_GROUP):
        cs = slice(c0, c0 + COL_GROUP)
        p = jnp.dot(h, wxbc_ref[:, cs], preferred_element_type=F32)
        xbc_out[0, :, cs] = _conv_silu(p, stail_ref[:, cs], sconv_ref[:, cs], sconvb_ref[:, cs])
        stail_ref[:, cs] = p[rows - SUBLANES:]

    blk = SSM_CHUNK
    ri = lax.broadcasted_iota(jnp.int32, (blk, blk), 0)
    ci = lax.broadcasted_iota(jnp.int32, (blk, blk), 1)
    same_gdn_chunk = (ri // GDN_CHUNK) == (ci // GDN_CHUNK)
    tril_ssd = (ri >= ci).astype(F32)
    triu_ssd = (ri <= ci).astype(F32)
    tril_gdn = jnp.logical_and(ri >= ci, same_gdn_chunk).astype(F32)
    triu_gdn = jnp.logical_and(ri <= ci, same_gdn_chunk).astype(F32)

    pg = jnp.dot(h, wgc_ref[...], preferred_element_type=F32)
    gates = jnp.where(lax.broadcasted_iota(jnp.int32, pg.shape, 1) < GDN_HEADS, _sigmoid(pg),
                      -jnp.exp(galog_c_ref[...]) * _softplus(pg + gbias_c_ref[...]))
    pgt = lax.dot_general(wgr_ref[...], h, (((1,), (1,)), ((), ())), preferred_element_type=F32)
    gates_t = jnp.where(lax.broadcasted_iota(jnp.int32, pgt.shape, 0) < GDN_HEADS, _sigmoid(pgt),
                        -jnp.exp(galog_r_ref[...]) * _softplus(pgt + gbias_r_ref[...]))
    dt = _softplus(jnp.dot(h, wdc_ref[...], preferred_element_type=F32) + dbias_c_ref[...])
    pdt = lax.dot_general(wdr_ref[...], h, (((1,), (1,)), ((), ())), preferred_element_type=F32)
    dt_t = _softplus(pdt + dbias_r_ref[...])
    dcol_out[0] = dt
    drow_out[0] = dt_t
    a_col = dt * ssm_a_c_ref[...]
    a_row = dt_t * ssm_a_r_ref[...]
    lane = lax.broadcasted_iota(jnp.int32, (blk, 2 * GDN_HEADS), 1)
    sub = lax.broadcasted_iota(jnp.int32, (2 * GDN_HEADS, blk), 0)
    for r0 in range(0, rows, blk):
        rs = slice(r0, r0 + blk)
        g_blk = gates[rs]
        gcol_out[0, rs, :] = jnp.where(lane < GDN_HEADS, g_blk, _mm_f32(tril_gdn, g_blk))
        gt_blk = gates_t[:, rs]
        grow_out[0, :, rs] = jnp.where(sub < GDN_HEADS, gt_blk, _mm_f32(gt_blk, triu_gdn))
        acol_out[0, rs, :] = _mm_f32(tril_ssd, a_col[rs])
        arow_out[0, :, rs] = _mm_f32(a_row[:, rs], triu_ssd)


def _in_proj(x, norm1_w, w_in, gdn_conv_w, gdn_a_log, gdn_dt_bias, ssm_conv_w, ssm_conv_b,
             ssm_dt_bias, ssm_a_log):
    bsz, seq, _ = x.shape
    rows = min(IN_ROWS, seq)
    nh = GDN_HEADS
    o = 0
    w_qkv = w_in[:, o:o + 3 * GDN_DIM].astype(BF16); o += 3 * GDN_DIM
    w_gz = w_in[:, o:o + GDN_DIM].astype(BF16); o += GDN_DIM
    w_gates = w_in[:, o:o + 2 * nh].astype(BF16); o += 2 * nh
    w_sz = w_in[:, o:o + SSM_DIM].astype(BF16); o += SSM_DIM
    w_xbc = w_in[:, o:o + SSM_XBC].astype(BF16); o += SSM_XBC
    w_dt = w_in[:, o:o + SSM_HEADS].astype(BF16); o += SSM_HEADS
    assert o == w_in.shape[1]
    zeros8 = jnp.zeros((nh,), F32)
    galog16 = jnp.concatenate([zeros8, gdn_a_log.astype(F32)])
    gbias16 = jnp.concatenate([zeros8, gdn_dt_bias.astype(F32)])
    dbias = ssm_dt_bias.astype(F32)
    ssm_a = -jnp.exp(ssm_a_log.astype(F32))

    args = [
        x, norm1_w.reshape(1, D_MODEL), w_qkv, w_gz, w_gates, w_gates.T, w_sz, w_xbc, w_dt, w_dt.T,
        gdn_conv_w, ssm_conv_w, ssm_conv_b.reshape(1, SSM_XBC),
        galog16.reshape(1, 2 * nh), gbias16.reshape(1, 2 * nh),
        galog16.reshape(2 * nh, 1), gbias16.reshape(2 * nh, 1),
        dbias.reshape(1, SSM_HEADS), dbias.reshape(SSM_HEADS, 1),
        ssm_a.reshape(1, SSM_HEADS), ssm_a.reshape(SSM_HEADS, 1),
    ]
    in_specs = [pl.BlockSpec((1, rows, D_MODEL), lambda b, t: (b, t, 0))]
    in_specs += [_const_spec(a.shape) for a in args[1:]]

    def row_spec(cols):
        return pl.BlockSpec((1, rows, cols), lambda b, t: (b, t, 0))

    def col_major_spec(nrows):
        return pl.BlockSpec((1, nrows, rows), lambda b, t: (b, 0, t))

    out_shape = [
        jax.ShapeDtypeStruct((bsz, seq, 3 * GDN_DIM), F32),
        jax.ShapeDtypeStruct((bsz, seq, GDN_DIM), F32),
        jax.ShapeDtypeStruct((bsz, seq, 2 * nh), F32),
        jax.ShapeDtypeStruct((bsz, 2 * nh, seq), F32),
        jax.ShapeDtypeStruct((bsz, seq, SSM_DIM), F32),
        jax.ShapeDtypeStruct((bsz, seq, SSM_XBC), F32),
        jax.ShapeDtypeStruct((bsz, seq, SSM_HEADS), F32),
        jax.ShapeDtypeStruct((bsz, SSM_HEADS, seq), F32),
        jax.ShapeDtypeStruct((bsz, seq, SSM_HEADS), F32),
        jax.ShapeDtypeStruct((bsz, SSM_HEADS, seq), F32),
    ]
    out_specs = [row_spec(3 * GDN_DIM), row_spec(GDN_DIM), row_spec(2 * nh), col_major_spec(2 * nh),
                 row_spec(SSM_DIM), row_spec(SSM_XBC), row_spec(SSM_HEADS),
                 col_major_spec(SSM_HEADS), row_spec(SSM_HEADS), col_major_spec(SSM_HEADS)]
    return pl.pallas_call(
        _in_proj_kernel,
        grid=(bsz, seq // rows),
        in_specs=in_specs,
        out_specs=out_specs,
        out_shape=out_shape,
        scratch_shapes=[pltpu.VMEM((SUBLANES, 3 * GDN_DIM), F32),
                        pltpu.VMEM((SUBLANES, SSM_XBC), F32)],
        compiler_params=pltpu.CompilerParams(
            dimension_semantics=("arbitrary", "arbitrary"), vmem_limit_bytes=VMEM_LIMIT_BYTES),
        name="in_proj",
    )(*args)


def _unit_lower_inverses(a_list):
    n = a_list[0].shape[0]
    ri = lax.broadcasted_iota(jnp.int32, (n, n), 0)
    ci = lax.broadcasted_iota(jnp.int32, (n, n), 1)
    same16 = (ri // 16) == (ci // 16)
    same32 = (ri // 32) == (ci // 32)
    only32 = jnp.logical_and(same32, jnp.logical_not(same16))
    eye = (ri == ci).astype(F32)
    d = [jnp.where(same16, a, 0.0) for a in a_list]
    inv = [eye - x for x in d]
    xp = [_mm(x, x) for x in d]
    for step in range(3):
        inv = [i + _mm(i, x) for i, x in zip(inv, xp)]
        if step < 2:
            xp = [_mm(x, x) for x in xp]
    for mask in (only32, jnp.logical_not(same32)):
        tmp = [_mm(i, jnp.where(mask, a, 0.0)) for i, a in zip(inv, a_list)]
        inv = [i - _mm(t, i) for i, t in zip(inv, tmp)]
    return inv


def _gdn_kernel(qkv_ref, gz_ref, gcol_ref, grow_ref, nw_ref, o_ref, s_ref):
    rows = qkv_ref.shape[1]
    c = GDN_CHUNK
    nchunks = rows // c

    @pl.when(pl.program_id(1) == 0)
    def _():
        s_ref[...] = jnp.zeros_like(s_ref)

    ri = lax.broadcasted_iota(jnp.int32, (c, c), 0)
    ci = lax.broadcasted_iota(jnp.int32, (c, c), 1)
    causal = ri >= ci
    strict = ri > ci
    norm_w = nw_ref[...]
    heads = range(GDN_HEADS)

    def chunk_body(n, carry):
        r0 = pl.multiple_of(n * c, c)
        rs = pl.ds(r0, c)
        gcol = gcol_ref[0, rs, :]
        grow = grow_ref[0, n]
        q = [qkv_ref[0, rs, h * GDN_DK:(h + 1) * GDN_DK] for h in heads]
        k = [qkv_ref[0, rs, GDN_DIM + h * GDN_DK:GDN_DIM + (h + 1) * GDN_DK] for h in heads]
        v = [qkv_ref[0, rs, 2 * GDN_DIM + h * GDN_DV:2 * GDN_DIM + (h + 1) * GDN_DV] for h in heads]
        beta = [jnp.broadcast_to(gcol[:, h:h + 1], (c, GDN_DK)) for h in heads]
        gc = [gcol[:, GDN_HEADS + h:GDN_HEADS + h + 1] for h in heads]
        gr = [grow[GDN_HEADS + h:GDN_HEADS + h + 1, :] for h in heads]
        g_last = [x[c - 1:c, :] for x in gc]
        decay = [jnp.exp(jnp.where(causal, a - b, -jnp.inf)) for a, b in zip(gc, gr)]
        exp_g = [jnp.broadcast_to(jnp.exp(x), (c, GDN_DK)) for x in gc]
        kb = [a * b for a, b in zip(k, beta)]
        gram = [_mm_nt(jnp.concatenate([a, b], axis=0), kk) for a, b, kk in zip(kb, q, k)]
        a_strict = [jnp.where(strict, g[:c] * d, 0.0) for g, d in zip(gram, decay)]
        attn = [g[c:] * d for g, d in zip(gram, decay)]
        t_inv = _unit_lower_inverses(a_strict)
        sol = [_mm(t, jnp.concatenate([vv * b, kk * e], axis=1))
               for t, vv, b, kk, e in zip(t_inv, v, beta, kb, exp_g)]
        state = [s_ref[h] for h in heads]
        ws_qs = [_mm(jnp.concatenate([s[:, GDN_DV:], qq * e], axis=0), st)
                 for s, qq, e, st in zip(sol, q, exp_g, state)]
        v_new = [s[:, :GDN_DV] - x[:c] for s, x in zip(sol, ws_qs)]
        o = [x[c:] + _mm(a, vn) for x, a, vn in zip(ws_qs, attn, v_new)]
        for h in heads:
            k_dec = k[h] * jnp.exp(g_last[h] - gc[h])
            s_ref[h] = state[h] * jnp.exp(g_last[h]) + _mm_tn(k_dec, v_new[h])
        for h in heads:
            hs = slice(h * GDN_DV, (h + 1) * GDN_DV)
            o_ref[0, rs, hs] = (_rmsnorm(o[h], norm_w) * gz_ref[0, rs, hs]).astype(o_ref.dtype)
        return carry

    lax.fori_loop(0, nchunks, chunk_body, 0)


def _gdn(qkv, gz, gcol, grow, norm_w):
    bsz, seq, _ = qkv.shape
    rows = min(GDN_ROWS, seq)
    cpb = rows // GDN_CHUNK
    nh2 = 2 * GDN_HEADS
    grow_c = grow.reshape(bsz, nh2, seq // GDN_CHUNK, GDN_CHUNK).transpose(0, 2, 1, 3)
    return pl.pallas_call(
        _gdn_kernel,
        grid=(bsz, seq // rows),
        in_specs=[
            pl.BlockSpec((1, rows, 3 * GDN_DIM), lambda b, t: (b, t, 0)),
            pl.BlockSpec((1, rows, GDN_DIM), lambda b, t: (b, t, 0)),
            pl.BlockSpec((1, rows, nh2), lambda b, t: (b, t, 0)),
            pl.BlockSpec((1, cpb, nh2, GDN_CHUNK), lambda b, t: (b, t, 0, 0)),
            _const_spec((1, GDN_DV)),
        ],
        out_specs=pl.BlockSpec((1, rows, GDN_DIM), lambda b, t: (b, t, 0)),
        out_shape=jax.ShapeDtypeStruct((bsz, seq, GDN_DIM), BF16),
        scratch_shapes=[pltpu.VMEM((GDN_HEADS, GDN_DK, GDN_DV), F32)],
        compiler_params=pltpu.CompilerParams(
            dimension_semantics=("arbitrary", "arbitrary"), vmem_limit_bytes=VMEM_LIMIT_BYTES),
        name="gdn",
    )(qkv, gz, gcol, grow_c, norm_w.reshape(1, GDN_DV).astype(F32))


def _ssd_kernel(xbc_ref, sz_ref, dcol_ref, acol_ref, drow_ref, arow_ref, dskip_ref, nw_ref, o_ref,
                h_ref):
    rows = xbc_ref.shape[1]
    c = SSM_CHUNK
    nchunks = rows // c

    @pl.when(pl.program_id(1) == 0)
    def _():
        h_ref[...] = jnp.zeros_like(h_ref)

    ri = lax.broadcasted_iota(jnp.int32, (c, c), 0)
    ci = lax.broadcasted_iota(jnp.int32, (c, c), 1)
    causal = ri >= ci
    pair_w = 2 * SSM_HEADDIM
    low_half = lax.broadcasted_iota(jnp.int32, (c, pair_w), 1) < SSM_HEADDIM
    high_half = jnp.logical_not(low_half)
    gd = SSM_GROUP_DIM

    def pair_lanes(cols, hd):
        r = cols.shape[0]
        lo = jnp.broadcast_to(cols[:, hd:hd + 1], (r, pair_w))
        hi = jnp.broadcast_to(cols[:, hd + 1:hd + 2], (r, pair_w))
        return jnp.where(low_half[:r], lo, hi)

    def chunk_body(n, carry):
        r0 = pl.multiple_of(n * c, c)
        rs = pl.ds(r0, c)
        dt_col = dcol_ref[0, rs, :]
        acs_col = acol_ref[0, rs, :]
        dt_row = drow_ref[0, n]
        acs_row = arow_ref[0, n]
        a_last = acs_col[c - 1:c, :]
        exp_a = jnp.exp(acs_col)
        coef = jnp.exp(a_last - acs_col) * dt_col
        chunk_decay = jnp.exp(a_last)
        for g in range(SSM_GROUPS):
            bg = xbc_ref[0, rs, SSM_DIM + g * SSM_STATE:SSM_DIM + (g + 1) * SSM_STATE]
            cg = xbc_ref[0, rs,
                         SSM_DIM + SSM_BC + g * SSM_STATE:SSM_DIM + SSM_BC + (g + 1) * SSM_STATE]
            cb = _mm_nt(cg, bg)
            h_prev = h_ref[g]
            y_off = _mm(cg, h_prev)
            x_pairs = []
            x_dec = []
            cd = []
            for jp in range(SSM_HPG // 2):
                hd = g * SSM_HPG + 2 * jp
                lo = g * gd + jp * pair_w
                x_pairs.append(xbc_ref[0, rs, lo:lo + pair_w])
                x_dec.append(x_pairs[jp] * pair_lanes(coef, hd))
                cd.append(pair_lanes(chunk_decay, hd))
            h_ref[g] = (h_prev * jnp.concatenate(cd, axis=1)
                        + _mm_tn(bg, jnp.concatenate(x_dec, axis=1)))
            y_pairs = []
            for jp in range(SSM_HPG // 2):
                hd = g * SSM_HPG + 2 * jp
                acc = y_off[:, jp * pair_w:(jp + 1) * pair_w] * pair_lanes(exp_a, hd)
                for half, mask in enumerate((low_half, high_half)):
                    ac = acs_col[:, hd + half:hd + half + 1]
                    ar = acs_row[hd + half:hd + half + 1, :]
                    seg = jnp.exp(jnp.where(causal, ac - ar, -jnp.inf))
                    scores = seg * (cb * dt_row[hd + half:hd + half + 1, :])
                    acc = acc + _mm(scores, jnp.where(mask, x_pairs[jp], 0.0))
                y_pairs.append(acc)
            gs = slice(g * gd, (g + 1) * gd)
            y = jnp.concatenate(y_pairs, axis=1) + xbc_ref[0, rs, gs] * dskip_ref[:, gs]
            yg = y * sz_ref[0, rs, gs]
            yg = yg * lax.rsqrt(jnp.mean(yg * yg, axis=-1, keepdims=True) + EPS)
            o_ref[0, rs, gs] = (yg * nw_ref[:, gs]).astype(o_ref.dtype)
        return carry

    lax.fori_loop(0, nchunks, chunk_body, 0)


def _ssd(xbc, sz, dcol, acol, drow, arow, d_skip, norm_w):
    bsz, seq, _ = xbc.shape
    rows = min(SSD_ROWS, seq)
    cpb = rows // SSM_CHUNK
    nh = SSM_HEADS
    per_chunk = lambda a: a.reshape(bsz, nh, seq // SSM_CHUNK, SSM_CHUNK).transpose(0, 2, 1, 3)
    dskip = jnp.repeat(d_skip.astype(F32), SSM_HEADDIM).reshape(1, SSM_DIM)
    return pl.pallas_call(
        _ssd_kernel,
        grid=(bsz, seq // rows),
        in_specs=[
            pl.BlockSpec((1, rows, SSM_XBC), lambda b, t: (b, t, 0)),
            pl.BlockSpec((1, rows, SSM_DIM), lambda b, t: (b, t, 0)),
            pl.BlockSpec((1, rows, nh), lambda b, t: (b, t, 0)),
            pl.BlockSpec((1, rows, nh), lambda b, t: (b, t, 0)),
            pl.BlockSpec((1, cpb, nh, SSM_CHUNK), lambda b, t: (b, t, 0, 0)),
            pl.BlockSpec((1, cpb, nh, SSM_CHUNK), lambda b, t: (b, t, 0, 0)),
            _const_spec((1, SSM_DIM)),
            _const_spec((1, SSM_DIM)),
        ],
        out_specs=pl.BlockSpec((1, rows, SSM_DIM), lambda b, t: (b, t, 0)),
        out_shape=jax.ShapeDtypeStruct((bsz, seq, SSM_DIM), BF16),
        scratch_shapes=[pltpu.VMEM((SSM_GROUPS, SSM_STATE, SSM_GROUP_DIM), F32)],
        compiler_params=pltpu.CompilerParams(
            dimension_semantics=("arbitrary", "arbitrary"), vmem_limit_bytes=VMEM_LIMIT_BYTES),
        name="ssd",
    )(xbc, sz, dcol, acol, per_chunk(drow), per_chunk(arow), dskip,
      norm_w.reshape(1, SSM_DIM).astype(F32))


def _mem_kv_kernel(mem_ref, nw_ref, wk_ref, wv_ref, k_out, v_out):
    m = _rmsnorm(mem_ref[0], nw_ref[...]).astype(BF16)
    k_out[0] = jnp.dot(m, wk_ref[...], preferred_element_type=F32).astype(k_out.dtype)
    v_out[0] = jnp.dot(m, wv_ref[...], preferred_element_type=F32).astype(v_out.dtype)


def _mem_kv(mem, mem_norm_w, wk, wv):
    bsz, mlen, _ = mem.shape
    kv_shape = jax.ShapeDtypeStruct((bsz, mlen, D_MODEL), BF16)
    return pl.pallas_call(
        _mem_kv_kernel,
        grid=(bsz,),
        in_specs=[pl.BlockSpec((1, mlen, D_MODEL), lambda b: (b, 0, 0)),
                  _const_spec((1, D_MODEL)),
                  _const_spec((D_MODEL, D_MODEL)),
                  _const_spec((D_MODEL, D_MODEL))],
        out_specs=[pl.BlockSpec((1, mlen, D_MODEL), lambda b: (b, 0, 0))] * 2,
        out_shape=[kv_shape, kv_shape],
        compiler_params=pltpu.CompilerParams(
            dimension_semantics=("arbitrary",), vmem_limit_bytes=VMEM_LIMIT_BYTES),
        name="mem_kv",
    )(mem, mem_norm_w.reshape(1, D_MODEL), wk.astype(BF16), wv.astype(BF16))


def _tail_kernel(x_ref, oa_ref, ob_ref, k_ref, v_ref, wout_ref, n2_ref, wq_ref, wo_ref, n3_ref,
                 wup_ref, wdown_ref, nf_ref, y_ref):
    x = x_ref[0]
    x = x + jnp.dot(oa_ref[0], wout_ref[:GDN_DIM, :], preferred_element_type=F32)
    x = x + jnp.dot(ob_ref[0], wout_ref[GDN_DIM:, :], preferred_element_type=F32)

    h = _rmsnorm(x, n2_ref[...]).astype(BF16)
    q = jnp.dot(h, wq_ref[...], preferred_element_type=F32)
    scale = MEM_HEADDIM ** -0.5
    heads = []
    for hd in range(MEM_HEADS):
        hs = slice(hd * MEM_HEADDIM, (hd + 1) * MEM_HEADDIM)
        s = _mm_nt(q[:, hs], k_ref[0, :, hs]) * scale
        s = s - jnp.max(s, axis=-1, keepdims=True)
        p = jnp.exp(s)
        p = p / jnp.sum(p, axis=-1, keepdims=True)
        heads.append(_mm(p, v_ref[0, :, hs]))
    attn = jnp.concatenate(heads, axis=1)
    x = x + _mm(attn, wo_ref[...])

    h = _rmsnorm(x, n3_ref[...]).astype(BF16)
    acc = x
    for c0 in range(0, D_FF, FF_GROUP):
        u = jnp.maximum(jnp.dot(h, wup_ref[:, c0:c0 + FF_GROUP], preferred_element_type=F32), 0.0)
        acc = acc + _mm(u * u, wdown_ref[c0:c0 + FF_GROUP, :])
    y_ref[0] = _rmsnorm(acc, nf_ref[...])


def _tail(x, o_a, o_b, k_mem, v_mem, w_out, norm2_w, wq, wo, norm3_w, w_up, w_down, final_w):
    bsz, seq, _ = x.shape
    rows = min(TAIL_ROWS, seq)
    mlen = k_mem.shape[1]
    row_spec = pl.BlockSpec((1, rows, D_MODEL), lambda b, t: (b, t, 0))
    mem_spec = pl.BlockSpec((1, mlen, D_MODEL), lambda b, t: (b, 0, 0))
    vec = lambda w: w.reshape(1, D_MODEL).astype(F32)
    return pl.pallas_call(
        _tail_kernel,
        grid=(bsz, seq // rows),
        in_specs=[row_spec, row_spec, row_spec, mem_spec, mem_spec,
                  _const_spec((GDN_DIM + SSM_DIM, D_MODEL)), _const_spec((1, D_MODEL)),
                  _const_spec((D_MODEL, D_MODEL)), _const_spec((D_MODEL, D_MODEL)),
                  _const_spec((1, D_MODEL)), _const_spec((D_MODEL, D_FF)),
                  _const_spec((D_FF, D_MODEL)), _const_spec((1, D_MODEL))],
        out_specs=row_spec,
        out_shape=jax.ShapeDtypeStruct((bsz, seq, D_MODEL), x.dtype),
        compiler_params=pltpu.CompilerParams(
            dimension_semantics=("arbitrary", "arbitrary"), vmem_limit_bytes=VMEM_LIMIT_BYTES),
        name="tail",
    )(x, o_a, o_b, k_mem, v_mem, w_out.astype(BF16), vec(norm2_w), wq.astype(BF16),
      wo.astype(BF16), vec(norm3_w), w_up.astype(BF16), w_down.astype(BF16), vec(final_w))


def kernel(x, mem, norm1_w, w_in, gdn_conv_w, gdn_a_log, gdn_dt_bias, gdn_norm_w, ssm_conv_w,
           ssm_conv_b, ssm_a_log, ssm_dt_bias, ssm_d, ssm_norm_w, w_out, norm2_w, mem_norm_w,
           wq_mem, wk_mem, wv_mem, wo_mem, norm3_w, w_up, w_down, final_norm_w):
    qkv, gz, gcol, grow, sz, xbc, dcol, drow, acol, arow = _in_proj(
        x, norm1_w, w_in, gdn_conv_w, gdn_a_log, gdn_dt_bias, ssm_conv_w, ssm_conv_b, ssm_dt_bias,
        ssm_a_log)
    o_a = _gdn(qkv, gz, gcol, grow, gdn_norm_w)
    o_b = _ssd(xbc, sz, dcol, acol, drow, arow, ssm_d, ssm_norm_w)
    k_mem, v_mem = _mem_kv(mem, mem_norm_w, wk_mem, wv_mem)
    return _tail(x, o_a, o_b, k_mem, v_mem, w_out, norm2_w, wq_mem, wo_mem, norm3_w, w_up, w_down,
                 final_norm_w)
```

```python
import jax
import jax.numpy as jnp
from jax import lax
from jax.experimental import pallas as pl
from jax.experimental.pallas import tpu as pltpu

D_MODEL = 1024
EPS = 1e-6
CONV_K = 4
GDN_HEADS = 8
GDN_DK = 128
GDN_DV = 128
GDN_DIM = GDN_HEADS * GDN_DV
GDN_CHUNK = 64
SSM_DIM = D_MODEL
SSM_HEADDIM = 64
SSM_HEADS = SSM_DIM // SSM_HEADDIM
SSM_GROUPS = 2
SSM_HPG = SSM_HEADS // SSM_GROUPS
SSM_STATE = 128
SSM_CHUNK = 128
SSM_BC = SSM_GROUPS * SSM_STATE
SSM_XBC = SSM_DIM + 2 * SSM_BC
SSM_GROUP_DIM = SSM_DIM // SSM_GROUPS
MEM_HEADS = 4
MEM_HEADDIM = D_MODEL // MEM_HEADS
D_FF = 4 * D_MODEL

F32 = jnp.float32
BF16 = jnp.bfloat16

V7X_VMEM_BYTES = 64 * 1024 * 1024
VMEM_LIMIT_BYTES = V7X_VMEM_BYTES - 8 * 1024 * 1024
SUBLANES = 8
LANES = 128

ROW_TILE = 512
ROW_GROUPS = ROW_TILE // SUBLANES
assert ROW_GROUPS == GDN_CHUNK and SSM_CHUNK == 2 * GDN_CHUNK
TAIL_ROWS = 512
COL_GROUP = 512
FF_GROUP = 1024
HALO = (CONV_K - 1) * SUBLANES


def _mm(a, b):
    return jnp.dot(a.astype(BF16), b.astype(BF16), preferred_element_type=F32)


def _mm_nt(a, b):
    return lax.dot_general(a.astype(BF16), b.astype(BF16), (((1,), (1,)), ((), ())),
                           preferred_element_type=F32)


def _mm_tn(a, b):
    return lax.dot_general(a.astype(BF16), b.astype(BF16), (((0,), (0,)), ((), ())),
                           preferred_element_type=F32)


def _mm_f32(a, b):
    return jnp.dot(a, b, preferred_element_type=F32, precision=lax.Precision.HIGHEST)


def _rmsnorm(x, w):
    return x * lax.rsqrt(jnp.mean(x * x, axis=-1, keepdims=True) + EPS) * w


def _sigmoid(x):
    return 1.0 / (1.0 + jnp.exp(-x))


def _silu(x):
    half = 0.5 * x
    return half * (1.0 + jnp.tanh(half))


def _softplus(x):
    return jnp.maximum(x, 0.0) + jnp.log(1.0 + jnp.exp(-jnp.abs(x)))


def _const_spec(shape):
    zeros = (0,) * len(shape)
    return pl.BlockSpec(shape, lambda *_: zeros, pipeline_mode=pl.Buffered(1))


def _time_rows(chunk, size):
    start = chunk * size
    return pl.ds(start if isinstance(start, int) else pl.multiple_of(start, size), size)


def _chunk_rows(ref, lead, s, n_sub=1):
    parts = [ref[lead + (pl.ds(s + i, ROW_GROUPS, stride=SUBLANES), slice(None))]
             for i in range(n_sub)]
    return parts[0] if n_sub == 1 else jnp.concatenate(parts, axis=0)


def _conv_silu(p, halo, w, bias):
    rows = p.shape[0]
    sub0 = lax.broadcasted_iota(jnp.int32, (SUBLANES, p.shape[1]), 0) == 0
    wrapped = []
    for j in range(CONV_K - 1):
        cur = p[rows - HALO + j * SUBLANES:rows - HALO + (j + 1) * SUBLANES]
        prev = halo[j * SUBLANES:(j + 1) * SUBLANES]
        wrapped.append(jnp.where(sub0, pltpu.roll(prev, 1, 0), pltpu.roll(cur, 1, 0)))
    acc = p * w[CONV_K - 1:CONV_K, :]
    if bias is not None:
        acc = acc + bias
    for k in range(1, CONV_K):
        shifted = jnp.concatenate(wrapped[CONV_K - 1 - k:] + [p[:rows - k * SUBLANES]], axis=0)
        acc = acc + shifted * w[CONV_K - 1 - k:CONV_K - k, :]
    return _silu(acc)


def _in_proj_kernel(x_ref, n1_ref, perm_ref, wqkv_ref, wgz_ref, wgc_ref, wgr_ref, wsz_ref,
                    wxbc_ref, wdc_ref, wdr_ref, gconv_ref, sconv_ref, sconvb_ref,
                    galog_c_ref, gbias_c_ref, galog_r_ref, gbias_r_ref, dbias_c_ref, dbias_r_ref,
                    ssm_a_c_ref, ssm_a_r_ref,
                    qkv_out, gz_out, gcol_out, grow_out, sz_out, xbc_out, dcol_out, drow_out,
                    acol_out, arow_out, ghalo_ref, shalo_ref):
    rows = x_ref.shape[1]

    @pl.when(pl.program_id(1) == 0)
    def _():
        ghalo_ref[...] = jnp.zeros_like(ghalo_ref)
        shalo_ref[...] = jnp.zeros_like(shalo_ref)

    h = _rmsnorm(x_ref[0], n1_ref[...]).astype(BF16)
    hp = jnp.dot(perm_ref[...], h, preferred_element_type=F32).astype(BF16)

    q_scale = GDN_DK ** -0.5
    for c0 in range(0, 3 * GDN_DIM, COL_GROUP):
        cs = slice(c0, c0 + COL_GROUP)
        p = jnp.dot(hp, wqkv_ref[:, cs], preferred_element_type=F32)
        y = _conv_silu(p, ghalo_ref[:, cs], gconv_ref[:, cs], None)
        ghalo_ref[:, cs] = p[rows - HALO:]
        for j in range(COL_GROUP // LANES):
            yh = y[:, j * LANES:(j + 1) * LANES]
            if c0 < 2 * GDN_DIM:
                yh = yh * lax.rsqrt(jnp.sum(yh * yh, axis=-1, keepdims=True) + EPS)
                if c0 < GDN_DIM:
                    yh = yh * q_scale
            qkv_out[0, c0 // LANES + j] = yh

    for c0 in range(0, GDN_DIM, COL_GROUP):
        cs = slice(c0, c0 + COL_GROUP)
        gz = _silu(jnp.dot(hp, wgz_ref[:, cs], preferred_element_type=F32))
        sz = _silu(jnp.dot(hp, wsz_ref[:, cs], preferred_element_type=F32))
        for j in range(COL_GROUP // LANES):
            gz_out[0, c0 // LANES + j] = gz[:, j * LANES:(j + 1) * LANES]
            sz_out[0, c0 // LANES + j] = sz[:, j * LANES:(j + 1) * LANES]

    for c0 in range(0, SSM_XBC, COL_GROUP):
        cs = slice(c0, c0 + COL_GROUP)
        p = jnp.dot(hp, wxbc_ref[:, cs], preferred_element_type=F32)
        y = _conv_silu(p, shalo_ref[:, cs], sconv_ref[:, cs], sconvb_ref[:, cs])
        shalo_ref[:, cs] = p[rows - HALO:]
        for j in range(COL_GROUP // LANES):
            xbc_out[0, c0 // LANES + j] = y[:, j * LANES:(j + 1) * LANES]

    blk = SSM_CHUNK
    ri = lax.broadcasted_iota(jnp.int32, (blk, blk), 0)
    ci = lax.broadcasted_iota(jnp.int32, (blk, blk), 1)
    same_gdn_chunk = (ri // GDN_CHUNK) == (ci // GDN_CHUNK)
    tril_ssd = (ri >= ci).astype(F32)
    triu_ssd = (ri <= ci).astype(F32)
    tril_gdn = jnp.logical_and(ri >= ci, same_gdn_chunk).astype(F32)
    triu_gdn = jnp.logical_and(ri <= ci, same_gdn_chunk).astype(F32)

    pg = jnp.dot(h, wgc_ref[...], preferred_element_type=F32)
    gates = jnp.where(lax.broadcasted_iota(jnp.int32, pg.shape, 1) < GDN_HEADS, _sigmoid(pg),
                      -jnp.exp(galog_c_ref[...]) * _softplus(pg + gbias_c_ref[...]))
    pgt = lax.dot_general(wgr_ref[...], h, (((1,), (1,)), ((), ())), preferred_element_type=F32)
    gates_t = jnp.where(lax.broadcasted_iota(jnp.int32, pgt.shape, 0) < GDN_HEADS, _sigmoid(pgt),
                        -jnp.exp(galog_r_ref[...]) * _softplus(pgt + gbias_r_ref[...]))
    dt = _softplus(jnp.dot(h, wdc_ref[...], preferred_element_type=F32) + dbias_c_ref[...])
    pdt = lax.dot_general(wdr_ref[...], h, (((1,), (1,)), ((), ())), preferred_element_type=F32)
    dt_t = _softplus(pdt + dbias_r_ref[...])
    dcol_out[0] = dt
    drow_out[0] = dt_t
    a_col = dt * ssm_a_c_ref[...]
    a_row = dt_t * ssm_a_r_ref[...]
    lane = lax.broadcasted_iota(jnp.int32, (blk, 2 * GDN_HEADS), 1)
    sub = lax.broadcasted_iota(jnp.int32, (2 * GDN_HEADS, blk), 0)
    for r0 in range(0, rows, blk):
        rs = slice(r0, r0 + blk)
        g_blk = gates[rs]
        gcol_out[0, rs, :] = jnp.where(lane < GDN_HEADS, g_blk, _mm_f32(tril_gdn, g_blk))
        gt_blk = gates_t[:, rs]
        grow_out[0, :, rs] = jnp.where(sub < GDN_HEADS, gt_blk, _mm_f32(gt_blk, triu_gdn))
        acol_out[0, rs, :] = _mm_f32(tril_ssd, a_col[rs])
        arow_out[0, :, rs] = _mm_f32(a_row[:, rs], triu_ssd)


def _in_proj(x, norm1_w, w_in, gdn_conv_w, gdn_a_log, gdn_dt_bias, ssm_conv_w, ssm_conv_b,
             ssm_dt_bias, ssm_a_log):
    bsz, seq, _ = x.shape
    rows = ROW_TILE
    assert seq % rows == 0
    nh = GDN_HEADS
    o = 0
    w_qkv = w_in[:, o:o + 3 * GDN_DIM].astype(BF16); o += 3 * GDN_DIM
    w_gz = w_in[:, o:o + GDN_DIM].astype(BF16); o += GDN_DIM
    w_gates = w_in[:, o:o + 2 * nh].astype(BF16); o += 2 * nh
    w_sz = w_in[:, o:o + SSM_DIM].astype(BF16); o += SSM_DIM
    w_xbc = w_in[:, o:o + SSM_XBC].astype(BF16); o += SSM_XBC
    w_dt = w_in[:, o:o + SSM_HEADS].astype(BF16); o += SSM_HEADS
    assert o == w_in.shape[1]
    zeros8 = jnp.zeros((nh,), F32)
    galog16 = jnp.concatenate([zeros8, gdn_a_log.astype(F32)])
    gbias16 = jnp.concatenate([zeros8, gdn_dt_bias.astype(F32)])
    dbias = ssm_dt_bias.astype(F32)
    ssm_a = -jnp.exp(ssm_a_log.astype(F32))
    tile_row = jnp.arange(rows)
    src_time = ROW_GROUPS * (tile_row % SUBLANES) + tile_row // SUBLANES
    perm = (src_time[:, None] == jnp.arange(rows)[None, :]).astype(BF16)

    args = [
        x, norm1_w.reshape(1, D_MODEL), perm, w_qkv, w_gz, w_gates, w_gates.T, w_sz, w_xbc,
        jnp.tile(w_dt, (1, SPLIT)), w_dt.T, gdn_conv_w, ssm_conv_w, ssm_conv_b.reshape(1, SSM_XBC),
        galog16.reshape(1, 2 * nh), gbias16.reshape(1, 2 * nh),
        galog16.reshape(2 * nh, 1), gbias16.reshape(2 * nh, 1),
        jnp.tile(dbias, SPLIT).reshape(1, SPLIT * SSM_HEADS), dbias.reshape(SSM_HEADS, 1),
        jnp.tile(ssm_a, SPLIT).reshape(1, SPLIT * SSM_HEADS), ssm_a.reshape(SSM_HEADS, 1),
    ]
    in_specs = [pl.BlockSpec((1, rows, D_MODEL), lambda b, t: (b, t, 0))]
    in_specs += [_const_spec(a.shape) for a in args[1:]]

    def slab_spec(cols):
        return pl.BlockSpec((1, cols // LANES, rows, LANES), lambda b, t: (b, 0, t, 0))

    def slab_shape(cols):
        return jax.ShapeDtypeStruct((bsz, cols // LANES, seq, LANES), F32)

    def row_spec(cols):
        return pl.BlockSpec((1, rows, cols), lambda b, t: (b, t, 0))

    def col_major_spec(nrows):
        return pl.BlockSpec((1, nrows, rows), lambda b, t: (b, 0, t))

    out_shape = [
        slab_shape(3 * GDN_DIM),
        slab_shape(GDN_DIM),
        jax.ShapeDtypeStruct((bsz, seq, 2 * nh), F32),
        jax.ShapeDtypeStruct((bsz, 2 * nh, seq), F32),
        slab_shape(SSM_DIM),
        slab_shape(SSM_XBC),
        jax.ShapeDtypeStruct((bsz, seq, SPLIT * SSM_HEADS), F32),
        jax.ShapeDtypeStruct((bsz, SSM_HEADS, seq), F32),
        jax.ShapeDtypeStruct((bsz, seq, SPLIT * SSM_HEADS), F32),
        jax.ShapeDtypeStruct((bsz, SSM_HEADS, seq), F32),
    ]
    out_specs = [slab_spec(3 * GDN_DIM), slab_spec(GDN_DIM), row_spec(2 * nh),
                 col_major_spec(2 * nh), slab_spec(SSM_DIM), slab_spec(SSM_XBC),
                 row_spec(SPLIT * SSM_HEADS), col_major_spec(SSM_HEADS),
                 row_spec(SPLIT * SSM_HEADS), col_major_spec(SSM_HEADS)]
    return pl.pallas_call(
        _in_proj_kernel,
        grid=(bsz, seq // rows),
        in_specs=in_specs,
        out_specs=out_specs,
        out_shape=out_shape,
        scratch_shapes=[pltpu.VMEM((HALO, 3 * GDN_DIM), F32),
                        pltpu.VMEM((HALO, SSM_XBC), F32)],
        compiler_params=pltpu.CompilerParams(
            dimension_semantics=("arbitrary", "arbitrary"), vmem_limit_bytes=VMEM_LIMIT_BYTES),
        name="in_proj",
    )(*args)


GDN_PAIR = 2
GDN_CHAINS = GDN_PAIR * GDN_HEADS


def _gdn_prepare(qkv_ref, gcol_ref, grow_ref, pair):
    c = GDN_CHUNK
    ri = lax.broadcasted_iota(jnp.int32, (c, c), 0)
    ci = lax.broadcasted_iota(jnp.int32, (c, c), 1)
    causal = ri >= ci
    strict = ri > ci
    same16 = (ri // 16) == (ci // 16)
    same32 = (ri // 32) == (ci // 32)
    only32 = jnp.logical_and(same32, jnp.logical_not(same16))
    eye = (ri == ci).astype(F32)

    q, k, v, beta, gc, decay = [], [], [], [], [], []
    for j in range(GDN_PAIR):
        s = GDN_PAIR * pair + j
        gcol = gcol_ref[0, _time_rows(s, c), :]
        grow = grow_ref[0, s]
        for h in range(GDN_HEADS):
            q.append(_chunk_rows(qkv_ref, (0, h), s))
            k.append(_chunk_rows(qkv_ref, (0, GDN_HEADS + h), s))
            v.append(_chunk_rows(qkv_ref, (0, 2 * GDN_HEADS + h), s))
            beta.append(jnp.broadcast_to(gcol[:, h:h + 1], (c, GDN_DK)))
            gc.append(gcol[:, GDN_HEADS + h:GDN_HEADS + h + 1])
            gr = grow[GDN_HEADS + h:GDN_HEADS + h + 1, :]
            decay.append(jnp.exp(jnp.where(causal, gc[-1] - gr, -jnp.inf)))
    g_last = [x[c - 1:c, :] for x in gc]
    exp_g = [jnp.broadcast_to(jnp.exp(x), (c, GDN_DK)) for x in gc]
    kb = [a * b for a, b in zip(k, beta)]
    gram = [_mm_nt(jnp.concatenate([a, b], axis=0), kk) for a, b, kk in zip(kb, q, k)]
    yield None
    a_strict = [jnp.where(strict, g[:c] * d, 0.0) for g, d in zip(gram, decay)]
    attn = [(g[c:] * d).astype(BF16) for g, d in zip(gram, decay)]
    d16 = [jnp.where(same16, a, 0.0) for a in a_strict]
    inv = [eye - x for x in d16]
    xp = [_mm(x, x) for x in d16]
    yield None
    for step in range(3):
        inv = [i + _mm(i, x) for i, x in zip(inv, xp)]
        if step < 2:
            xp = [_mm(x, x) for x in xp]
        yield None
    for mask in (only32, jnp.logical_not(same32)):
        tmp = [_mm(i, jnp.where(mask, a, 0.0)) for i, a in zip(inv, a_strict)]
        yield None
        inv = [i - _mm(t, i) for i, t in zip(inv, tmp)]
        yield None
    sol = [_mm(t, jnp.concatenate([vv * b, kk * e], axis=1))
           for t, vv, b, kk, e in zip(inv, v, beta, kb, exp_g)]
    u = [x[:, :GDN_DV] for x in sol]
    wq = [jnp.concatenate([x[:, GDN_DV:], qq * e], axis=0).astype(BF16)
          for x, qq, e in zip(sol, q, exp_g)]
    kdt = [(kk * jnp.exp(gl - g)).T.astype(BF16) for kk, gl, g in zip(k, g_last, gc)]
    cdec = [jnp.broadcast_to(jnp.exp(gl), (SUBLANES, GDN_DV)) for gl in g_last]
    yield u, wq, attn, kdt, cdec


def _gdn_recur(u_ref, wq_ref, at_ref, kdt_ref, cd_ref, s_ref, gz_ref, nw_ref, o_ref, pair, j):
    c = GDN_CHUNK
    s = GDN_PAIR * pair + j
    chains = [j * GDN_HEADS + h for h in range(GDN_HEADS)]
    state = [s_ref[h] for h in range(GDN_HEADS)]
    ws_qs = [jnp.dot(wq_ref[ch], st.astype(BF16), preferred_element_type=F32)
             for ch, st in zip(chains, state)]
    yield None
    v_new = [(u_ref[ch] - x[:c]).astype(BF16) for ch, x in zip(chains, ws_qs)]
    o = [x[c:] + jnp.dot(at_ref[ch], vn, preferred_element_type=F32)
         for ch, x, vn in zip(chains, ws_qs, v_new)]
    for h, ch in enumerate(chains):
        s_ref[h] = (state[h] * cd_ref[ch][:1]
                    + jnp.dot(kdt_ref[ch], v_new[h], preferred_element_type=F32))
    rows = _time_rows(s, c)
    for h in range(GDN_HEADS):
        gate = _chunk_rows(gz_ref, (0, h), s)
        o_ref[0, rows, h * GDN_DV:(h + 1) * GDN_DV] = (
            _rmsnorm(o[h], nw_ref[...]) * gate).astype(o_ref.dtype)
    yield None


def _gdn_kernel(qkv_ref, gz_ref, gcol_ref, grow_ref, nw_ref, o_ref,
                s_ref, u_ref, wq_ref, at_ref, kdt_ref, cd_ref):
    npairs = qkv_ref.shape[2] // (GDN_PAIR * GDN_CHUNK)

    @pl.when(pl.program_id(1) == 0)
    def _():
        s_ref[...] = jnp.zeros_like(s_ref)

    def store_prepared(prepared):
        u, wq, attn, kdt, cdec = prepared
        for ch in range(GDN_CHAINS):
            u_ref[ch] = u[ch]
            wq_ref[ch] = wq[ch]
            at_ref[ch] = attn[ch]
            kdt_ref[ch] = kdt[ch]
            cd_ref[ch] = cdec[ch]

    def recur(pair, j):
        return _gdn_recur(u_ref, wq_ref, at_ref, kdt_ref, cd_ref, s_ref, gz_ref, nw_ref, o_ref,
                          pair, j)

    prepared = None
    for prepared in _gdn_prepare(qkv_ref, gcol_ref, grow_ref, 0):
        pass
    store_prepared(prepared)

    def body(pair, carry):
        prep = _gdn_prepare(qkv_ref, gcol_ref, grow_ref, pair + 1)
        rec = [recur(pair, j) for j in range(GDN_PAIR)]
        next(prep)
        next(rec[0])
        next(prep)
        next(prep)
        next(rec[0])
        next(prep)
        next(prep)
        next(rec[1])
        next(prep)
        next(prep)
        next(rec[1])
        next(prep)
        next(prep)
        store_prepared(next(prep))
        return carry

    lax.fori_loop(0, npairs - 1, body, 0)

    for j in range(GDN_PAIR):
        for _ in recur(npairs - 1, j):
            pass


def _gdn(qkv, gz, gcol, grow, norm_w):
    bsz, _, seq, _ = qkv.shape
    rows = ROW_TILE
    cpb = rows // GDN_CHUNK
    nh2 = 2 * GDN_HEADS
    grow_c = grow.reshape(bsz, nh2, seq // GDN_CHUNK, GDN_CHUNK).transpose(0, 2, 1, 3)
    c = GDN_CHUNK
    return pl.pallas_call(
        _gdn_kernel,
        grid=(bsz, seq // rows),
        in_specs=[
            pl.BlockSpec((1, 3 * GDN_HEADS, rows, LANES), lambda b, t: (b, 0, t, 0)),
            pl.BlockSpec((1, GDN_HEADS, rows, LANES), lambda b, t: (b, 0, t, 0)),
            pl.BlockSpec((1, rows, nh2), lambda b, t: (b, t, 0)),
            pl.BlockSpec((1, cpb, nh2, GDN_CHUNK), lambda b, t: (b, t, 0, 0)),
            _const_spec((1, GDN_DV)),
        ],
        out_specs=pl.BlockSpec((1, rows, GDN_DIM), lambda b, t: (b, t, 0)),
        out_shape=jax.ShapeDtypeStruct((bsz, seq, GDN_DIM), BF16),
        scratch_shapes=[
            pltpu.VMEM((GDN_HEADS, GDN_DK, GDN_DV), F32),
            pltpu.VMEM((GDN_CHAINS, c, GDN_DV), F32),
            pltpu.VMEM((GDN_CHAINS, 2 * c, GDN_DK), BF16),
            pltpu.VMEM((GDN_CHAINS, c, c), BF16),
            pltpu.VMEM((GDN_CHAINS, GDN_DK, c), BF16),
            pltpu.VMEM((GDN_CHAINS, SUBLANES, GDN_DV), F32),
        ],
        compiler_params=pltpu.CompilerParams(
            dimension_semantics=("arbitrary", "arbitrary"), vmem_limit_bytes=VMEM_LIMIT_BYTES),
        name="gdn",
    )(qkv, gz, gcol, grow_c, norm_w.reshape(1, GDN_DV).astype(F32))


SPLIT = 3


def _spread_lanes(cols, onehot):
    heads = cols.shape[1] // SPLIT
    replica = lax.broadcasted_iota(jnp.int32, cols.shape, 1) // heads
    hi = cols.astype(BF16)
    rest = cols - hi.astype(F32)
    mid = rest.astype(BF16)
    lo = (rest - mid.astype(F32)).astype(BF16)
    terms = jnp.where(replica == 0, hi, jnp.where(replica == 1, mid, lo))
    return jnp.dot(terms, onehot, preferred_element_type=F32)


def _ssd_kernel(xbc_ref, sz_ref, dcol_ref, acol_ref, drow_ref, arow_ref, dskip_ref, nw_ref,
                e64_ref, e128_ref, o_ref, h_ref):
    rows = xbc_ref.shape[2]
    c = SSM_CHUNK
    nchunks = rows // c
    sub_per_chunk = c // GDN_CHUNK
    x_slabs = SSM_DIM // LANES

    @pl.when(pl.program_id(1) == 0)
    def _():
        h_ref[...] = jnp.zeros_like(h_ref)

    ri = lax.broadcasted_iota(jnp.int32, (c, c), 0)
    ci = lax.broadcasted_iota(jnp.int32, (c, c), 1)
    causal = ri >= ci
    pair_w = 2 * SSM_HEADDIM
    low_half = lax.broadcasted_iota(jnp.int32, (c, pair_w), 1) < SSM_HEADDIM
    high_half = jnp.logical_not(low_half)
    gd = SSM_GROUP_DIM
    pairs_per_group = SSM_HPG // 2

    def chunk_body(n, carry):
        rs = _time_rows(n, c)
        s0 = n * sub_per_chunk

        def slab(ref, idx):
            return _chunk_rows(ref, (0, idx), s0, sub_per_chunk)

        dt_col = dcol_ref[0, rs, :]
        acs_col = acol_ref[0, rs, :]
        dt_row = drow_ref[0, n]
        acs_row = arow_ref[0, n]
        a_last = acs_col[c - 1:c, :]
        coef = jnp.exp(a_last - acs_col) * dt_col
        chunk_decay = jnp.broadcast_to(jnp.exp(a_last), (SUBLANES, a_last.shape[1]))
        acs_wide = _spread_lanes(acs_col, e128_ref[...])
        wide = _spread_lanes(jnp.concatenate([coef, jnp.exp(acs_col), chunk_decay], axis=0),
                             e64_ref[...])
        coef_wide, exp_a_wide, cd_wide = wide[:c], wide[c:2 * c], wide[2 * c:2 * c + 1]
        for g in range(SSM_GROUPS):
            gs = slice(g * gd, (g + 1) * gd)
            bg = slab(xbc_ref, x_slabs + g)
            cg = slab(xbc_ref, x_slabs + SSM_GROUPS + g)
            cb = _mm_nt(cg, bg)
            h_prev = h_ref[g]
            y_off = _mm(cg, h_prev) * exp_a_wide[:, gs]
            x_pairs = [slab(xbc_ref, g * pairs_per_group + jp) for jp in range(pairs_per_group)]
            x_g = jnp.concatenate(x_pairs, axis=1)
            h_ref[g] = h_prev * cd_wide[:, gs] + _mm_tn(bg, x_g * coef_wide[:, gs])
            y_pairs = []
            for jp in range(pairs_per_group):
                hd = g * SSM_HPG + 2 * jp
                acc = y_off[:, jp * pair_w:(jp + 1) * pair_w]
                for half, mask in enumerate((low_half, high_half)):
                    ac = acs_wide[:, (hd + half) * LANES:(hd + half + 1) * LANES]
                    ar = acs_row[hd + half:hd + half + 1, :]
                    seg = jnp.exp(jnp.where(causal, ac - ar, -jnp.inf))
                    scores = seg * (cb * dt_row[hd + half:hd + half + 1, :])
                    acc = acc + _mm(scores, jnp.where(mask, x_pairs[jp], 0.0))
                y_pairs.append(acc)
            y = jnp.concatenate(y_pairs, axis=1) + x_g * dskip_ref[:, gs]
            gate = jnp.concatenate(
                [slab(sz_ref, g * pairs_per_group + jp) for jp in range(pairs_per_group)], axis=1)
            yg = y * gate
            yg = yg * lax.rsqrt(jnp.mean(yg * yg, axis=-1, keepdims=True) + EPS)
            o_ref[0, rs, gs] = (yg * nw_ref[:, gs]).astype(o_ref.dtype)
        return carry

    lax.fori_loop(0, nchunks, chunk_body, 0)


def _ssd(xbc, sz, dcol, acol, drow, arow, d_skip, norm_w):
    bsz, _, seq, _ = xbc.shape
    rows = ROW_TILE
    cpb = rows // SSM_CHUNK
    nh = SSM_HEADS
    per_chunk = lambda a: a.reshape(bsz, nh, seq // SSM_CHUNK, SSM_CHUNK).transpose(0, 2, 1, 3)
    dskip = jnp.repeat(d_skip.astype(F32), SSM_HEADDIM).reshape(1, SSM_DIM)
    eye = jnp.tile(jnp.eye(nh, dtype=BF16), (SPLIT, 1))
    e64 = jnp.repeat(eye, SSM_HEADDIM, axis=1)
    e128 = jnp.repeat(eye, LANES, axis=1)
    return pl.pallas_call(
        _ssd_kernel,
        grid=(bsz, seq // rows),
        in_specs=[
            pl.BlockSpec((1, SSM_XBC // LANES, rows, LANES), lambda b, t: (b, 0, t, 0)),
            pl.BlockSpec((1, SSM_DIM // LANES, rows, LANES), lambda b, t: (b, 0, t, 0)),
            pl.BlockSpec((1, rows, SPLIT * nh), lambda b, t: (b, t, 0)),
            pl.BlockSpec((1, rows, SPLIT * nh), lambda b, t: (b, t, 0)),
            pl.BlockSpec((1, cpb, nh, SSM_CHUNK), lambda b, t: (b, t, 0, 0)),
            pl.BlockSpec((1, cpb, nh, SSM_CHUNK), lambda b, t: (b, t, 0, 0)),
            _const_spec((1, SSM_DIM)),
            _const_spec((1, SSM_DIM)),
            _const_spec((SPLIT * nh, SSM_DIM)),
            _const_spec((SPLIT * nh, nh * LANES)),
        ],
        out_specs=pl.BlockSpec((1, rows, SSM_DIM), lambda b, t: (b, t, 0)),
        out_shape=jax.ShapeDtypeStruct((bsz, seq, SSM_DIM), BF16),
        scratch_shapes=[pltpu.VMEM((SSM_GROUPS, SSM_STATE, SSM_GROUP_DIM), F32)],
        compiler_params=pltpu.CompilerParams(
            dimension_semantics=("arbitrary", "arbitrary"), vmem_limit_bytes=VMEM_LIMIT_BYTES),
        name="ssd",
    )(xbc, sz, dcol, acol, per_chunk(drow), per_chunk(arow), dskip,
      norm_w.reshape(1, SSM_DIM).astype(F32), e64, e128)


def _mem_kv_kernel(mem_ref, nw_ref, wk_ref, wv_ref, k_out, v_out):
    m = _rmsnorm(mem_ref[0], nw_ref[...]).astype(BF16)
    k_out[0] = jnp.dot(m, wk_ref[...], preferred_element_type=F32).astype(k_out.dtype)
    v_out[0] = jnp.dot(m, wv_ref[...], preferred_element_type=F32).astype(v_out.dtype)


def _mem_kv(mem, mem_norm_w, wk, wv):
    bsz, mlen, _ = mem.shape
    kv_shape = jax.ShapeDtypeStruct((bsz, mlen, D_MODEL), BF16)
    return pl.pallas_call(
        _mem_kv_kernel,
        grid=(bsz,),
        in_specs=[pl.BlockSpec((1, mlen, D_MODEL), lambda b: (b, 0, 0)),
                  _const_spec((1, D_MODEL)),
                  _const_spec((D_MODEL, D_MODEL)),
                  _const_spec((D_MODEL, D_MODEL))],
        out_specs=[pl.BlockSpec((1, mlen, D_MODEL), lambda b: (b, 0, 0))] * 2,
        out_shape=[kv_shape, kv_shape],
        compiler_params=pltpu.CompilerParams(
            dimension_semantics=("arbitrary",), vmem_limit_bytes=VMEM_LIMIT_BYTES),
        name="mem_kv",
    )(mem, mem_norm_w.reshape(1, D_MODEL), wk.astype(BF16), wv.astype(BF16))


def _tail_kernel(x_ref, oa_ref, ob_ref, k_ref, v_ref, wout_ref, n2_ref, wq_ref, wo_ref, n3_ref,
                 wup_ref, wdown_ref, nf_ref, y_ref):
    x = x_ref[0]
    x = x + jnp.dot(oa_ref[0], wout_ref[:GDN_DIM, :], preferred_element_type=F32)
    x = x + jnp.dot(ob_ref[0], wout_ref[GDN_DIM:, :], preferred_element_type=F32)

    h = _rmsnorm(x, n2_ref[...]).astype(BF16)
    q = jnp.dot(h, wq_ref[...], preferred_element_type=F32)
    scale = MEM_HEADDIM ** -0.5
    heads = []
    for hd in range(MEM_HEADS):
        hs = slice(hd * MEM_HEADDIM, (hd + 1) * MEM_HEADDIM)
        s = _mm_nt(q[:, hs], k_ref[0, :, hs]) * scale
        s = s - jnp.max(s, axis=-1, keepdims=True)
        p = jnp.exp(s)
        p = p / jnp.sum(p, axis=-1, keepdims=True)
        heads.append(_mm(p, v_ref[0, :, hs]))
    attn = jnp.concatenate(heads, axis=1)
    x = x + _mm(attn, wo_ref[...])

    h = _rmsnorm(x, n3_ref[...]).astype(BF16)
    acc = x
    for c0 in range(0, D_FF, FF_GROUP):
        u = jnp.maximum(jnp.dot(h, wup_ref[:, c0:c0 + FF_GROUP], preferred_element_type=F32), 0.0)
        acc = acc + _mm(u * u, wdown_ref[c0:c0 + FF_GROUP, :])
    y_ref[0] = _rmsnorm(acc, nf_ref[...])


def _tail(x, o_a, o_b, k_mem, v_mem, w_out, norm2_w, wq, wo, norm3_w, w_up, w_down, final_w):
    bsz, seq, _ = x.shape
    rows = min(TAIL_ROWS, seq)
    mlen = k_mem.shape[1]
    row_spec = pl.BlockSpec((1, rows, D_MODEL), lambda b, t: (b, t, 0))
    mem_spec = pl.BlockSpec((1, mlen, D_MODEL), lambda b, t: (b, 0, 0))
    vec = lambda w: w.reshape(1, D_MODEL).astype(F32)
    return pl.pallas_call(
        _tail_kernel,
        grid=(bsz, seq // rows),
        in_specs=[row_spec, row_spec, row_spec, mem_spec, mem_spec,
                  _const_spec((GDN_DIM + SSM_DIM, D_MODEL)), _const_spec((1, D_MODEL)),
                  _const_spec((D_MODEL, D_MODEL)), _const_spec((D_MODEL, D_MODEL)),
                  _const_spec((1, D_MODEL)), _const_spec((D_MODEL, D_FF)),
                  _const_spec((D_FF, D_MODEL)), _const_spec((1, D_MODEL))],
        out_specs=row_spec,
        out_shape=jax.ShapeDtypeStruct((bsz, seq, D_MODEL), x.dtype),
        compiler_params=pltpu.CompilerParams(
            dimension_semantics=("arbitrary", "arbitrary"), vmem_limit_bytes=VMEM_LIMIT_BYTES),
        name="tail",
    )(x, o_a, o_b, k_mem, v_mem, w_out.astype(BF16), vec(norm2_w), wq.astype(BF16),
      wo.astype(BF16), vec(norm3_w), w_up.astype(BF16), w_down.astype(BF16), vec(final_w))


def kernel(x, mem, norm1_w, w_in, gdn_conv_w, gdn_a_log, gdn_dt_bias, gdn_norm_w, ssm_conv_w,
           ssm_conv_b, ssm_a_log, ssm_dt_bias, ssm_d, ssm_norm_w, w_out, norm2_w, mem_norm_w,
           wq_mem, wk_mem, wv_mem, wo_mem, norm3_w, w_up, w_down, final_norm_w):
    qkv, gz, gcol, grow, sz, xbc, dcol, drow, acol, arow = _in_proj(
        x, norm1_w, w_in, gdn_conv_w, gdn_a_log, gdn_dt_bias, ssm_conv_w, ssm_conv_b, ssm_dt_bias,
        ssm_a_log)
    o_a = _gdn(qkv, gz, gcol, grow, gdn_norm_w)
    o_b = _ssd(xbc, sz, dcol, acol, drow, arow, ssm_d, ssm_norm_w)
    k_mem, v_mem = _mem_kv(mem, mem_norm_w, wk_mem, wv_mem)
    return _tail(x, o_a, o_b, k_mem, v_mem, w_out, norm2_w, wq_mem, wo_mem, norm3_w, w_up, w_down,
                 final_norm_w)
```

```python
import jax
import jax.numpy as jnp
from jax import lax
from jax.experimental import pallas as pl
from jax.experimental.pallas import tpu as pltpu

D_MODEL = 1024
EPS = 1e-6
CONV_K = 4
GDN_HEADS = 8
GDN_DK = 128
GDN_DV = 128
GDN_DIM = GDN_HEADS * GDN_DV
GDN_CHUNK = 64
SSM_DIM = D_MODEL
SSM_HEADDIM = 64
SSM_HEADS = SSM_DIM // SSM_HEADDIM
SSM_GROUPS = 2
SSM_HPG = SSM_HEADS // SSM_GROUPS
SSM_STATE = 128
SSM_CHUNK = 128
SSM_BC = SSM_GROUPS * SSM_STATE
SSM_XBC = SSM_DIM + 2 * SSM_BC
SSM_GROUP_DIM = SSM_DIM // SSM_GROUPS
MEM_HEADS = 4
MEM_HEADDIM = D_MODEL // MEM_HEADS
D_FF = 4 * D_MODEL

F32 = jnp.float32
BF16 = jnp.bfloat16

V7X_VMEM_BYTES = 64 * 1024 * 1024
VMEM_LIMIT_BYTES = V7X_VMEM_BYTES - 8 * 1024 * 1024
SUBLANES = 8
LANES = 128

ROW_TILE = 512
ROW_GROUPS = ROW_TILE // SUBLANES
assert ROW_GROUPS == GDN_CHUNK and SSM_CHUNK == 2 * GDN_CHUNK
TAIL_ROWS = 512
COL_GROUP = 512
FF_GROUP = 1024
HALO = (CONV_K - 1) * SUBLANES


def _mm(a, b):
    return jnp.dot(a.astype(BF16), b.astype(BF16), preferred_element_type=F32)


def _mm_nt(a, b):
    return lax.dot_general(a.astype(BF16), b.astype(BF16), (((1,), (1,)), ((), ())),
                           preferred_element_type=F32)


def _mm_tn(a, b):
    return lax.dot_general(a.astype(BF16), b.astype(BF16), (((0,), (0,)), ((), ())),
                           preferred_element_type=F32)


SPLIT = 3


def _split(x):
    hi = x.astype(BF16)
    rest = x - hi.astype(F32)
    mid = rest.astype(BF16)
    lo = (rest - mid.astype(F32)).astype(BF16)
    return hi, mid, lo


def _tri_dot_left(tri_wide, x):
    return jnp.dot(tri_wide, jnp.concatenate(_split(x), axis=0), preferred_element_type=F32)


def _tri_dot_right(x, tri_tall):
    return jnp.dot(jnp.concatenate(_split(x), axis=1), tri_tall, preferred_element_type=F32)


def _rmsnorm(x, w):
    return x * lax.rsqrt(jnp.mean(x * x, axis=-1, keepdims=True) + EPS) * w


def _sigmoid(x):
    return 1.0 / (1.0 + jnp.exp(-x))


def _silu(x):
    half = 0.5 * x
    return half * (1.0 + jnp.tanh(half))


def _softplus(x):
    return jnp.maximum(x, 0.0) + jnp.log(1.0 + jnp.exp(-jnp.abs(x)))


def _const_spec(shape):
    zeros = (0,) * len(shape)
    return pl.BlockSpec(shape, lambda *_: zeros, pipeline_mode=pl.Buffered(1))


def _time_rows(chunk, size):
    start = chunk * size
    return pl.ds(start if isinstance(start, int) else pl.multiple_of(start, size), size)


def _chunk_rows(ref, lead, s, n_sub=1):
    parts = [ref[lead + (pl.ds(s + i, ROW_GROUPS, stride=SUBLANES), slice(None))]
             for i in range(n_sub)]
    return parts[0] if n_sub == 1 else jnp.concatenate(parts, axis=0)


def _conv_silu(p, halo, w, bias):
    rows = p.shape[0]
    sub0 = lax.broadcasted_iota(jnp.int32, (SUBLANES, p.shape[1]), 0) == 0
    wrapped = []
    for j in range(CONV_K - 1):
        cur = p[rows - HALO + j * SUBLANES:rows - HALO + (j + 1) * SUBLANES]
        prev = halo[j * SUBLANES:(j + 1) * SUBLANES]
        wrapped.append(jnp.where(sub0, pltpu.roll(prev, 1, 0), pltpu.roll(cur, 1, 0)))
    acc = p * w[CONV_K - 1:CONV_K, :]
    if bias is not None:
        acc = acc + bias
    for k in range(1, CONV_K):
        shifted = jnp.concatenate(wrapped[CONV_K - 1 - k:] + [p[:rows - k * SUBLANES]], axis=0)
        acc = acc + shifted * w[CONV_K - 1 - k:CONV_K - k, :]
    return _silu(acc)


def _in_proj_kernel(x_ref, n1_ref, perm_ref, wqkv_ref, wgz_ref, wgc_ref, wgr_ref, wsz_ref,
                    wxbc_ref, wdc_ref, wdr_ref, gconv_ref, sconv_ref, sconvb_ref,
                    galog_c_ref, gbias_c_ref, galog_r_ref, gbias_r_ref, dbias_c_ref, dbias_r_ref,
                    ssm_a_c_ref, ssm_a_r_ref,
                    qkv_out, gz_out, gcol_out, grow_out, sz_out, xbc_out, dcol_out, drow_out,
                    acol_out, arow_out, ghalo_ref, shalo_ref):
    rows = x_ref.shape[1]

    @pl.when(pl.program_id(1) == 0)
    def _():
        ghalo_ref[...] = jnp.zeros_like(ghalo_ref)
        shalo_ref[...] = jnp.zeros_like(shalo_ref)

    h = _rmsnorm(x_ref[0], n1_ref[...]).astype(BF16)
    hp = jnp.dot(perm_ref[...], h, preferred_element_type=F32).astype(BF16)

    q_scale = GDN_DK ** -0.5
    for c0 in range(0, 3 * GDN_DIM, COL_GROUP):
        cs = slice(c0, c0 + COL_GROUP)
        p = jnp.dot(hp, wqkv_ref[:, cs], preferred_element_type=F32)
        y = _conv_silu(p, ghalo_ref[:, cs], gconv_ref[:, cs], None)
        ghalo_ref[:, cs] = p[rows - HALO:]
        for j in range(COL_GROUP // LANES):
            yh = y[:, j * LANES:(j + 1) * LANES]
            if c0 < 2 * GDN_DIM:
                yh = yh * lax.rsqrt(jnp.sum(yh * yh, axis=-1, keepdims=True) + EPS)
                if c0 < GDN_DIM:
                    yh = yh * q_scale
            qkv_out[0, c0 // LANES + j] = yh

    for c0 in range(0, GDN_DIM, COL_GROUP):
        cs = slice(c0, c0 + COL_GROUP)
        gz = _silu(jnp.dot(hp, wgz_ref[:, cs], preferred_element_type=F32))
        sz = _silu(jnp.dot(hp, wsz_ref[:, cs], preferred_element_type=F32))
        for j in range(COL_GROUP // LANES):
            gz_out[0, c0 // LANES + j] = gz[:, j * LANES:(j + 1) * LANES]
            sz_out[0, c0 // LANES + j] = sz[:, j * LANES:(j + 1) * LANES]

    for c0 in range(0, SSM_XBC, COL_GROUP):
        cs = slice(c0, c0 + COL_GROUP)
        p = jnp.dot(hp, wxbc_ref[:, cs], preferred_element_type=F32)
        y = _conv_silu(p, shalo_ref[:, cs], sconv_ref[:, cs], sconvb_ref[:, cs])
        shalo_ref[:, cs] = p[rows - HALO:]
        for j in range(COL_GROUP // LANES):
            xbc_out[0, c0 // LANES + j] = y[:, j * LANES:(j + 1) * LANES]

    blk = SSM_CHUNK
    ri = lax.broadcasted_iota(jnp.int32, (blk, blk), 0)
    ci = lax.broadcasted_iota(jnp.int32, (blk, blk), 1)
    same_gdn_chunk = (ri // GDN_CHUNK) == (ci // GDN_CHUNK)
    wide = lambda m: jnp.concatenate([m.astype(BF16)] * SPLIT, axis=1)
    tall = lambda m: jnp.concatenate([m.astype(BF16)] * SPLIT, axis=0)
    tril_ssd = wide((ri >= ci).astype(F32))
    triu_ssd = tall((ri <= ci).astype(F32))
    tril_gdn = wide(jnp.logical_and(ri >= ci, same_gdn_chunk).astype(F32))
    triu_gdn = tall(jnp.logical_and(ri <= ci, same_gdn_chunk).astype(F32))

    pg = jnp.dot(h, wgc_ref[...], preferred_element_type=F32)
    gates = jnp.where(lax.broadcasted_iota(jnp.int32, pg.shape, 1) < GDN_HEADS, _sigmoid(pg),
                      -jnp.exp(galog_c_ref[...]) * _softplus(pg + gbias_c_ref[...]))
    pgt = lax.dot_general(wgr_ref[...], h, (((1,), (1,)), ((), ())), preferred_element_type=F32)
    gates_t = jnp.where(lax.broadcasted_iota(jnp.int32, pgt.shape, 0) < GDN_HEADS, _sigmoid(pgt),
                        -jnp.exp(galog_r_ref[...]) * _softplus(pgt + gbias_r_ref[...]))
    dt = _softplus(jnp.dot(h, wdc_ref[...], preferred_element_type=F32) + dbias_c_ref[...])
    pdt = lax.dot_general(wdr_ref[...], h, (((1,), (1,)), ((), ())), preferred_element_type=F32)
    dt_t = _softplus(pdt + dbias_r_ref[...])
    dcol_out[0] = dt
    drow_out[0] = dt_t
    a_col = dt * ssm_a_c_ref[...]
    a_row = dt_t * ssm_a_r_ref[...]
    lane = lax.broadcasted_iota(jnp.int32, (blk, 2 * GDN_HEADS), 1)
    sub = lax.broadcasted_iota(jnp.int32, (2 * GDN_HEADS, blk), 0)
    for r0 in range(0, rows, blk):
        rs = slice(r0, r0 + blk)
        g_blk = gates[rs]
        gcol_out[0, rs, :] = jnp.where(lane < GDN_HEADS, g_blk, _tri_dot_left(tril_gdn, g_blk))
        gt_blk = gates_t[:, rs]
        grow_out[0, :, rs] = jnp.where(sub < GDN_HEADS, gt_blk, _tri_dot_right(gt_blk, triu_gdn))
        acol_out[0, rs, :] = _tri_dot_left(tril_ssd, a_col[rs])
        arow_out[0, :, rs] = _tri_dot_right(a_row[:, rs], triu_ssd)


def _in_proj(x, norm1_w, w_in, gdn_conv_w, gdn_a_log, gdn_dt_bias, ssm_conv_w, ssm_conv_b,
             ssm_dt_bias, ssm_a_log):
    bsz, seq, _ = x.shape
    rows = ROW_TILE
    assert seq % rows == 0
    nh = GDN_HEADS
    o = 0
    w_qkv = w_in[:, o:o + 3 * GDN_DIM].astype(BF16); o += 3 * GDN_DIM
    w_gz = w_in[:, o:o + GDN_DIM].astype(BF16); o += GDN_DIM
    w_gates = w_in[:, o:o + 2 * nh].astype(BF16); o += 2 * nh
    w_sz = w_in[:, o:o + SSM_DIM].astype(BF16); o += SSM_DIM
    w_xbc = w_in[:, o:o + SSM_XBC].astype(BF16); o += SSM_XBC
    w_dt = w_in[:, o:o + SSM_HEADS].astype(BF16); o += SSM_HEADS
    assert o == w_in.shape[1]
    zeros8 = jnp.zeros((nh,), F32)
    galog16 = jnp.concatenate([zeros8, gdn_a_log.astype(F32)])
    gbias16 = jnp.concatenate([zeros8, gdn_dt_bias.astype(F32)])
    dbias = ssm_dt_bias.astype(F32)
    ssm_a = -jnp.exp(ssm_a_log.astype(F32))
    tile_row = jnp.arange(rows)
    src_time = ROW_GROUPS * (tile_row % SUBLANES) + tile_row // SUBLANES
    perm = (src_time[:, None] == jnp.arange(rows)[None, :]).astype(BF16)

    args = [
        x, norm1_w.reshape(1, D_MODEL), perm, w_qkv, w_gz, w_gates, w_gates.T, w_sz, w_xbc,
        jnp.tile(w_dt, (1, SPLIT)), w_dt.T, gdn_conv_w, ssm_conv_w, ssm_conv_b.reshape(1, SSM_XBC),
        galog16.reshape(1, 2 * nh), gbias16.reshape(1, 2 * nh),
        galog16.reshape(2 * nh, 1), gbias16.reshape(2 * nh, 1),
        jnp.tile(dbias, SPLIT).reshape(1, SPLIT * SSM_HEADS), dbias.reshape(SSM_HEADS, 1),
        jnp.tile(ssm_a, SPLIT).reshape(1, SPLIT * SSM_HEADS), ssm_a.reshape(SSM_HEADS, 1),
    ]
    in_specs = [pl.BlockSpec((1, rows, D_MODEL), lambda b, t: (b, t, 0))]
    in_specs += [_const_spec(a.shape) for a in args[1:]]

    def slab_spec(cols):
        return pl.BlockSpec((1, cols // LANES, rows, LANES), lambda b, t: (b, 0, t, 0))

    def slab_shape(cols):
        return jax.ShapeDtypeStruct((bsz, cols // LANES, seq, LANES), F32)

    def row_spec(cols):
        return pl.BlockSpec((1, rows, cols), lambda b, t: (b, t, 0))

    def col_major_spec(nrows):
        return pl.BlockSpec((1, nrows, rows), lambda b, t: (b, 0, t))

    out_shape = [
        slab_shape(3 * GDN_DIM),
        slab_shape(GDN_DIM),
        jax.ShapeDtypeStruct((bsz, seq, 2 * nh), F32),
        jax.ShapeDtypeStruct((bsz, 2 * nh, seq), F32),
        slab_shape(SSM_DIM),
        slab_shape(SSM_XBC),
        jax.ShapeDtypeStruct((bsz, seq, SPLIT * SSM_HEADS), F32),
        jax.ShapeDtypeStruct((bsz, SSM_HEADS, seq), F32),
        jax.ShapeDtypeStruct((bsz, seq, SPLIT * SSM_HEADS), F32),
        jax.ShapeDtypeStruct((bsz, SSM_HEADS, seq), F32),
    ]
    out_specs = [slab_spec(3 * GDN_DIM), slab_spec(GDN_DIM), row_spec(2 * nh),
                 col_major_spec(2 * nh), slab_spec(SSM_DIM), slab_spec(SSM_XBC),
                 row_spec(SPLIT * SSM_HEADS), col_major_spec(SSM_HEADS),
                 row_spec(SPLIT * SSM_HEADS), col_major_spec(SSM_HEADS)]
    return pl.pallas_call(
        _in_proj_kernel,
        grid=(bsz, seq // rows),
        in_specs=in_specs,
        out_specs=out_specs,
        out_shape=out_shape,
        scratch_shapes=[pltpu.VMEM((HALO, 3 * GDN_DIM), F32),
                        pltpu.VMEM((HALO, SSM_XBC), F32)],
        compiler_params=pltpu.CompilerParams(
            dimension_semantics=("arbitrary", "arbitrary"), vmem_limit_bytes=VMEM_LIMIT_BYTES),
        name="in_proj",
    )(*args)


GDN_PAIR = 2
GDN_CHAINS = GDN_PAIR * GDN_HEADS


def _gdn_prepare(qkv_ref, gcol_ref, grow_ref, pair):
    c = GDN_CHUNK
    ri = lax.broadcasted_iota(jnp.int32, (c, 2 * c), 0)
    lane = lax.broadcasted_iota(jnp.int32, (c, 2 * c), 1)
    ci = lane % c
    low = lane < c
    causal = ri >= ci
    strict = ri > ci
    same16 = (ri // 16) == (ci // 16)
    same32 = (ri // 32) == (ci // 32)
    only32 = jnp.logical_and(same32, jnp.logical_not(same16))
    eye = (ri == ci).astype(F32)
    zeros_k = jnp.zeros((c, GDN_DK), BF16)
    zeros_uw = jnp.zeros((c, GDN_DV + GDN_DK), BF16)

    def blockdiag(x_p):
        return jnp.concatenate([jnp.where(low, x_p, 0.0), jnp.where(low, 0.0, x_p)],
                               axis=0).astype(BF16)

    def mm_packed(l_p, x_p):
        return jnp.dot(l_p.astype(BF16), blockdiag(x_p), preferred_element_type=F32)

    q, k, kbs, rhs, exp_g, kdt, cdec = [], [], [], [], [], [], []
    gram_lhs, gram_rhs, decay = [], [], []
    for j in range(GDN_PAIR):
        s = GDN_PAIR * pair + j
        gcol = gcol_ref[0, _time_rows(s, c), :]
        grow = grow_ref[0, s]
        gc_wide = []
        for h in range(GDN_HEADS):
            q.append(_chunk_rows(qkv_ref, (0, h), s))
            k.append(_chunk_rows(qkv_ref, (0, GDN_HEADS + h), s))
            vv = _chunk_rows(qkv_ref, (0, 2 * GDN_HEADS + h), s)
            beta = jnp.broadcast_to(gcol[:, h:h + 1], (c, GDN_DK))
            gcw = jnp.broadcast_to(gcol[:, GDN_HEADS + h:GDN_HEADS + h + 1], (c, GDN_DK))
            g_last = gcw[c - 1:c, :]
            gc_wide.append(gcw)
            exp_g.append(jnp.exp(gcw))
            kbs.append(k[-1] * beta)
            rhs.append(jnp.concatenate([vv * beta, kbs[-1] * exp_g[-1]], axis=1).astype(BF16))
            kdt.append((k[-1] * jnp.exp(g_last - gcw)).T.astype(BF16))
            cdec.append(jnp.broadcast_to(jnp.exp(g_last), (SUBLANES, GDN_DV)))
            if h % 2 == 1:
                a, b = len(q) - 2, len(q) - 1
                gram_lhs.append(jnp.concatenate(
                    [jnp.concatenate([kbs[a], kbs[b]], axis=1),
                     jnp.concatenate([q[a], q[b]], axis=1)], axis=0).astype(BF16))
                gram_rhs.append(jnp.concatenate(
                    [jnp.concatenate([k[a].astype(BF16), zeros_k], axis=1),
                     jnp.concatenate([zeros_k, k[b].astype(BF16)], axis=1)], axis=0))
                gc_p = jnp.where(low, gc_wide[h - 1], gc_wide[h])
                gr_p = grow[h // 2:h // 2 + 1, :]
                decay.append(jnp.exp(jnp.where(causal, gc_p - gr_p, -jnp.inf)))
    gram = [lax.dot_general(l, r, (((1,), (1,)), ((), ())), preferred_element_type=F32)
            for l, r in zip(gram_lhs, gram_rhs)]
    yield None
    a_strict = [jnp.where(strict, g[:c] * d, 0.0) for g, d in zip(gram, decay)]
    attn = [(g[c:] * d).astype(BF16) for g, d in zip(gram, decay)]
    d16 = [jnp.where(same16, a, 0.0) for a in a_strict]
    inv = [eye - x for x in d16]
    xp = [mm_packed(x, x) for x in d16]
    yield None
    for step in range(3):
        inv = [i + mm_packed(i, x) for i, x in zip(inv, xp)]
        if step < 2:
            xp = [mm_packed(x, x) for x in xp]
        yield None
    for mask in (only32, jnp.logical_not(same32)):
        tmp = [mm_packed(i, jnp.where(mask, a, 0.0)) for i, a in zip(inv, a_strict)]
        yield None
        inv = [i - mm_packed(t, i) for i, t in zip(inv, tmp)]
        yield None
    sol = [jnp.dot(t.astype(BF16),
                   jnp.concatenate([jnp.concatenate([rhs[2 * i], zeros_uw], axis=1),
                                    jnp.concatenate([zeros_uw, rhs[2 * i + 1]], axis=1)], axis=0),
                   preferred_element_type=F32) for i, t in enumerate(inv)]
    width = GDN_DV + GDN_DK
    sol = [x[:, half * width:(half + 1) * width] for x in sol for half in range(2)]
    u = [x[:, :GDN_DV] for x in sol]
    wq = [jnp.concatenate([x[:, GDN_DV:], qq * e], axis=0).astype(BF16)
          for x, qq, e in zip(sol, q, exp_g)]
    yield u, wq, attn, kdt, cdec


def _gdn_recur(u_ref, wq_ref, at_ref, kdt_ref, cd_ref, s_ref, gz_ref, nw_ref, o_ref, pair, j):
    c = GDN_CHUNK
    s = GDN_PAIR * pair + j
    chains = [j * GDN_HEADS + h for h in range(GDN_HEADS)]
    state = [s_ref[h] for h in range(GDN_HEADS)]
    ws_qs = [jnp.dot(wq_ref[ch], st.astype(BF16), preferred_element_type=F32)
             for ch, st in zip(chains, state)]
    yield None
    v_new = [(u_ref[ch] - x[:c]).astype(BF16) for ch, x in zip(chains, ws_qs)]
    zeros_v = jnp.zeros((c, GDN_DV), BF16)
    o = []
    for p in range(GDN_HEADS // 2):
        vn = jnp.concatenate([jnp.concatenate([v_new[2 * p], zeros_v], axis=1),
                              jnp.concatenate([zeros_v, v_new[2 * p + 1]], axis=1)], axis=0)
        o_p = jnp.dot(at_ref[j * (GDN_HEADS // 2) + p], vn, preferred_element_type=F32)
        o.append(ws_qs[2 * p][c:] + o_p[:, :GDN_DV])
        o.append(ws_qs[2 * p + 1][c:] + o_p[:, GDN_DV:])
    for h, ch in enumerate(chains):
        s_ref[h] = (state[h] * cd_ref[ch][:1]
                    + jnp.dot(kdt_ref[ch], v_new[h], preferred_element_type=F32))
    rows = _time_rows(s, c)
    for h in range(GDN_HEADS):
        gate = _chunk_rows(gz_ref, (0, h), s)
        o_ref[0, rows, h * GDN_DV:(h + 1) * GDN_DV] = (
            _rmsnorm(o[h], nw_ref[...]) * gate).astype(o_ref.dtype)
    yield None


def _spread_lanes(cols, onehot):
    heads = cols.shape[1] // SPLIT
    replica = lax.broadcasted_iota(jnp.int32, cols.shape, 1) // heads
    hi, mid, lo = _split(cols)
    terms = jnp.where(replica == 0, hi, jnp.where(replica == 1, mid, lo))
    return jnp.dot(terms, onehot, preferred_element_type=F32)


def _ssd_chunk(xbc_ref, sz_ref, dcol_ref, acol_ref, drow_ref, arow_ref, dskip_ref, nw_ref,
               e64_ref, e128_ref, o_ref, h_ref, n):
    c = SSM_CHUNK
    sub_per_chunk = c // GDN_CHUNK
    x_slabs = SSM_DIM // LANES
    ri = lax.broadcasted_iota(jnp.int32, (c, c), 0)
    ci = lax.broadcasted_iota(jnp.int32, (c, c), 1)
    causal = ri >= ci
    pair_w = 2 * SSM_HEADDIM
    low_half = lax.broadcasted_iota(jnp.int32, (c, pair_w), 1) < SSM_HEADDIM
    high_half = jnp.logical_not(low_half)
    gd = SSM_GROUP_DIM
    pairs_per_group = SSM_HPG // 2
    groups = range(SSM_GROUPS)
    rs = _time_rows(n, c)
    s0 = n * sub_per_chunk

    def slab(ref, idx):
        return _chunk_rows(ref, (0, idx), s0, sub_per_chunk)

    dt_col = dcol_ref[0, rs, :]
    acs_col = acol_ref[0, rs, :]
    dt_row = drow_ref[0, n]
    acs_row = arow_ref[0, n]
    a_last = acs_col[c - 1:c, :]
    coef = jnp.exp(a_last - acs_col) * dt_col
    chunk_decay = jnp.broadcast_to(jnp.exp(a_last), (SUBLANES, a_last.shape[1]))
    acs_wide = _spread_lanes(acs_col, e128_ref[...])
    wide = _spread_lanes(jnp.concatenate([coef, jnp.exp(acs_col), chunk_decay], axis=0),
                         e64_ref[...])
    coef_wide, exp_a_wide, cd_wide = wide[:c], wide[c:2 * c], wide[2 * c:2 * c + 1]
    yield None
    bg = [slab(xbc_ref, x_slabs + g) for g in groups]
    cg = [slab(xbc_ref, x_slabs + SSM_GROUPS + g) for g in groups]
    cb = [_mm_nt(cg[g], bg[g]) for g in groups]
    h_prev = [h_ref[g] for g in groups]
    y_off = [_mm(cg[g], h_prev[g]) * exp_a_wide[:, g * gd:(g + 1) * gd] for g in groups]
    yield None
    x_pairs = [[slab(xbc_ref, g * pairs_per_group + jp) for jp in range(pairs_per_group)]
               for g in groups]
    x_g = [jnp.concatenate(x_pairs[g], axis=1) for g in groups]
    for g in groups:
        gs = slice(g * gd, (g + 1) * gd)
        h_ref[g] = h_prev[g] * cd_wide[:, gs] + _mm_tn(bg[g], x_g[g] * coef_wide[:, gs])
    yield None
    y_pairs = [[], []]
    for g in groups:
        for jp in range(pairs_per_group):
            hd = g * SSM_HPG + 2 * jp
            acc = y_off[g][:, jp * pair_w:(jp + 1) * pair_w]
            for half, mask in enumerate((low_half, high_half)):
                ac = acs_wide[:, (hd + half) * LANES:(hd + half + 1) * LANES]
                ar = acs_row[hd + half:hd + half + 1, :]
                seg = jnp.exp(jnp.where(causal, ac - ar, -jnp.inf))
                scores = seg * (cb[g] * dt_row[hd + half:hd + half + 1, :])
                acc = acc + _mm(scores, jnp.where(mask, x_pairs[g][jp], 0.0))
            y_pairs[g].append(acc)
            if jp % 2 == 1:
                yield None
    for g in groups:
        gs = slice(g * gd, (g + 1) * gd)
        y = jnp.concatenate(y_pairs[g], axis=1) + x_g[g] * dskip_ref[:, gs]
        gate = jnp.concatenate(
            [slab(sz_ref, g * pairs_per_group + jp) for jp in range(pairs_per_group)], axis=1)
        yg = y * gate
        yg = yg * lax.rsqrt(jnp.mean(yg * yg, axis=-1, keepdims=True) + EPS)
        o_ref[0, rs, gs] = (yg * nw_ref[:, gs]).astype(o_ref.dtype)
    yield None


def _interleave(*gens):
    last = [None] * len(gens)
    live = list(range(len(gens)))
    while live:
        for i in list(live):
            try:
                last[i] = next(gens[i])
            except StopIteration:
                live.remove(i)
    return last


def _spaced(gen, gap):
    for value in gen:
        yield value
        for _ in range(gap):
            yield value


def _mixers_kernel(qkv_ref, gz_ref, gcol_ref, grow_ref, gnw_ref,
                   xbc_ref, sz_ref, dcol_ref, acol_ref, drow_ref, arow_ref, dskip_ref, snw_ref,
                   e64_ref, e128_ref, oa_ref, ob_ref,
                   s_ref, u_ref, wq_ref, at_ref, kdt_ref, cd_ref, h_ref):
    npairs = qkv_ref.shape[2] // (GDN_PAIR * GDN_CHUNK)
    assert npairs == xbc_ref.shape[2] // SSM_CHUNK

    @pl.when(pl.program_id(1) == 0)
    def _():
        s_ref[...] = jnp.zeros_like(s_ref)
        h_ref[...] = jnp.zeros_like(h_ref)

    def store_prepared(prepared):
        u, wq, attn, kdt, cdec = prepared
        for ch in range(GDN_CHAINS):
            u_ref[ch] = u[ch]
            wq_ref[ch] = wq[ch]
            kdt_ref[ch] = kdt[ch]
            cd_ref[ch] = cdec[ch]
        for pc in range(GDN_CHAINS // 2):
            at_ref[pc] = attn[pc]

    def recur(pair):
        for j in range(GDN_PAIR):
            yield from _gdn_recur(u_ref, wq_ref, at_ref, kdt_ref, cd_ref, s_ref, gz_ref, gnw_ref,
                                  oa_ref, pair, j)

    def ssd(n):
        return _ssd_chunk(xbc_ref, sz_ref, dcol_ref, acol_ref, drow_ref, arow_ref, dskip_ref,
                          snw_ref, e64_ref, e128_ref, ob_ref, h_ref, n)

    prepared, _ = _interleave(_gdn_prepare(qkv_ref, gcol_ref, grow_ref, 0), ssd(0))
    store_prepared(prepared)

    def body(pair, carry):
        prepared, _, _ = _interleave(_gdn_prepare(qkv_ref, gcol_ref, grow_ref, pair + 1),
                                     _spaced(recur(pair), 1), ssd(pair + 1))
        store_prepared(prepared)
        return carry

    lax.fori_loop(0, npairs - 1, body, 0)

    _interleave(recur(npairs - 1))


def _mixers(qkv, gz, gcol, grow, gdn_norm_w, xbc, sz, dcol, acol, drow, arow, d_skip, ssm_norm_w):
    bsz, _, seq, _ = qkv.shape
    rows = ROW_TILE
    nh2 = 2 * GDN_HEADS
    nh = SSM_HEADS
    c = GDN_CHUNK
    grow_c = grow[:, GDN_HEADS:].reshape(bsz, GDN_HEADS, seq // c, c).transpose(0, 2, 1, 3)
    grow_c = grow_c.reshape(bsz, seq // c, GDN_HEADS // 2, 2 * c)
    per_chunk = lambda a: a.reshape(bsz, nh, seq // SSM_CHUNK, SSM_CHUNK).transpose(0, 2, 1, 3)
    dskip = jnp.repeat(d_skip.astype(F32), SSM_HEADDIM).reshape(1, SSM_DIM)
    eye = jnp.tile(jnp.eye(nh, dtype=BF16), (SPLIT, 1))
    e64 = jnp.repeat(eye, SSM_HEADDIM, axis=1)
    e128 = jnp.repeat(eye, LANES, axis=1)

    def slab_spec(cols):
        return pl.BlockSpec((1, cols // LANES, rows, LANES), lambda b, t: (b, 0, t, 0))

    def row_spec(cols):
        return pl.BlockSpec((1, rows, cols), lambda b, t: (b, t, 0))

    def per_chunk_spec(nrows, chunk):
        return pl.BlockSpec((1, rows // chunk, nrows, chunk), lambda b, t: (b, t, 0, 0))

    out_shape = jax.ShapeDtypeStruct((bsz, seq, D_MODEL), BF16)
    return pl.pallas_call(
        _mixers_kernel,
        grid=(bsz, seq // rows),
        in_specs=[
            slab_spec(3 * GDN_DIM), slab_spec(GDN_DIM), row_spec(nh2),
            pl.BlockSpec((1, rows // c, GDN_HEADS // 2, 2 * c), lambda b, t: (b, t, 0, 0)),
            _const_spec((1, GDN_DV)),
            slab_spec(SSM_XBC), slab_spec(SSM_DIM), row_spec(SPLIT * nh), row_spec(SPLIT * nh),
            per_chunk_spec(nh, SSM_CHUNK), per_chunk_spec(nh, SSM_CHUNK),
            _const_spec((1, SSM_DIM)), _const_spec((1, SSM_DIM)),
            _const_spec((SPLIT * nh, SSM_DIM)), _const_spec((SPLIT * nh, nh * LANES)),
        ],
        out_specs=[row_spec(GDN_DIM), row_spec(SSM_DIM)],
        out_shape=[out_shape, out_shape],
        scratch_shapes=[
            pltpu.VMEM((GDN_HEADS, GDN_DK, GDN_DV), F32),
            pltpu.VMEM((GDN_CHAINS, c, GDN_DV), F32),
            pltpu.VMEM((GDN_CHAINS, 2 * c, GDN_DK), BF16),
            pltpu.VMEM((GDN_CHAINS // 2, c, 2 * c), BF16),
            pltpu.VMEM((GDN_CHAINS, GDN_DK, c), BF16),
            pltpu.VMEM((GDN_CHAINS, SUBLANES, GDN_DV), F32),
            pltpu.VMEM((SSM_GROUPS, SSM_STATE, SSM_GROUP_DIM), F32),
        ],
        compiler_params=pltpu.CompilerParams(
            dimension_semantics=("arbitrary", "arbitrary"), vmem_limit_bytes=VMEM_LIMIT_BYTES),
        name="mixers",
    )(qkv, gz, gcol, grow_c, gdn_norm_w.reshape(1, GDN_DV).astype(F32),
      xbc, sz, dcol, acol, per_chunk(drow), per_chunk(arow), dskip,
      ssm_norm_w.reshape(1, SSM_DIM).astype(F32), e64, e128)


def _mem_kv_kernel(mem_ref, nw_ref, wk_ref, wv_ref, k_out, v_out):
    m = _rmsnorm(mem_ref[0], nw_ref[...]).astype(BF16)
    k_out[0] = jnp.dot(m, wk_ref[...], preferred_element_type=F32).astype(k_out.dtype)
    v_out[0] = jnp.dot(m, wv_ref[...], preferred_element_type=F32).astype(v_out.dtype)


def _mem_kv(mem, mem_norm_w, wk, wv):
    bsz, mlen, _ = mem.shape
    kv_shape = jax.ShapeDtypeStruct((bsz, mlen, D_MODEL), BF16)
    return pl.pallas_call(
        _mem_kv_kernel,
        grid=(bsz,),
        in_specs=[pl.BlockSpec((1, mlen, D_MODEL), lambda b: (b, 0, 0)),
                  _const_spec((1, D_MODEL)),
                  _const_spec((D_MODEL, D_MODEL)),
                  _const_spec((D_MODEL, D_MODEL))],
        out_specs=[pl.BlockSpec((1, mlen, D_MODEL), lambda b: (b, 0, 0))] * 2,
        out_shape=[kv_shape, kv_shape],
        compiler_params=pltpu.CompilerParams(
            dimension_semantics=("arbitrary",), vmem_limit_bytes=VMEM_LIMIT_BYTES),
        name="mem_kv",
    )(mem, mem_norm_w.reshape(1, D_MODEL), wk.astype(BF16), wv.astype(BF16))


def _tail_kernel(x_ref, oa_ref, ob_ref, k_ref, v_ref, wout_ref, n2_ref, wq_ref, wo_ref, n3_ref,
                 wup_ref, wdown_ref, nf_ref, y_ref):
    x = x_ref[0]
    x = x + jnp.dot(oa_ref[0], wout_ref[:GDN_DIM, :], preferred_element_type=F32)
    x = x + jnp.dot(ob_ref[0], wout_ref[GDN_DIM:, :], preferred_element_type=F32)

    h = _rmsnorm(x, n2_ref[...]).astype(BF16)
    q = jnp.dot(h, wq_ref[...], preferred_element_type=F32)
    scale = MEM_HEADDIM ** -0.5
    heads = []
    for hd in range(MEM_HEADS):
        hs = slice(hd * MEM_HEADDIM, (hd + 1) * MEM_HEADDIM)
        s = _mm_nt(q[:, hs], k_ref[0, :, hs]) * scale
        s = s - jnp.max(s, axis=-1, keepdims=True)
        p = jnp.exp(s)
        p = p / jnp.sum(p, axis=-1, keepdims=True)
        heads.append(_mm(p, v_ref[0, :, hs]))
    attn = jnp.concatenate(heads, axis=1)
    x = x + _mm(attn, wo_ref[...])

    h = _rmsnorm(x, n3_ref[...]).astype(BF16)
    acc = x
    for c0 in range(0, D_FF, FF_GROUP):
        u = jnp.maximum(jnp.dot(h, wup_ref[:, c0:c0 + FF_GROUP], preferred_element_type=F32), 0.0)
        acc = acc + _mm(u * u, wdown_ref[c0:c0 + FF_GROUP, :])
    y_ref[0] = _rmsnorm(acc, nf_ref[...])


def _tail(x, o_a, o_b, k_mem, v_mem, w_out, norm2_w, wq, wo, norm3_w, w_up, w_down, final_w):
    bsz, seq, _ = x.shape
    rows = min(TAIL_ROWS, seq)
    mlen = k_mem.shape[1]
    row_spec = pl.BlockSpec((1, rows, D_MODEL), lambda b, t: (b, t, 0))
    mem_spec = pl.BlockSpec((1, mlen, D_MODEL), lambda b, t: (b, 0, 0))
    vec = lambda w: w.reshape(1, D_MODEL).astype(F32)
    return pl.pallas_call(
        _tail_kernel,
        grid=(bsz, seq // rows),
        in_specs=[row_spec, row_spec, row_spec, mem_spec, mem_spec,
                  _const_spec((GDN_DIM + SSM_DIM, D_MODEL)), _const_spec((1, D_MODEL)),
                  _const_spec((D_MODEL, D_MODEL)), _const_spec((D_MODEL, D_MODEL)),
                  _const_spec((1, D_MODEL)), _const_spec((D_MODEL, D_FF)),
                  _const_spec((D_FF, D_MODEL)), _const_spec((1, D_MODEL))],
        out_specs=row_spec,
        out_shape=jax.ShapeDtypeStruct((bsz, seq, D_MODEL), x.dtype),
        compiler_params=pltpu.CompilerParams(
            dimension_semantics=("arbitrary", "arbitrary"), vmem_limit_bytes=VMEM_LIMIT_BYTES),
        name="tail",
    )(x, o_a, o_b, k_mem, v_mem, w_out.astype(BF16), vec(norm2_w), wq.astype(BF16),
      wo.astype(BF16), vec(norm3_w), w_up.astype(BF16), w_down.astype(BF16), vec(final_w))


def kernel(x, mem, norm1_w, w_in, gdn_conv_w, gdn_a_log, gdn_dt_bias, gdn_norm_w, ssm_conv_w,
           ssm_conv_b, ssm_a_log, ssm_dt_bias, ssm_d, ssm_norm_w, w_out, norm2_w, mem_norm_w,
           wq_mem, wk_mem, wv_mem, wo_mem, norm3_w, w_up, w_down, final_norm_w):
    qkv, gz, gcol, grow, sz, xbc, dcol, drow, acol, arow = _in_proj(
        x, norm1_w, w_in, gdn_conv_w, gdn_a_log, gdn_dt_bias, ssm_conv_w, ssm_conv_b, ssm_dt_bias,
        ssm_a_log)
    o_a, o_b = _mixers(qkv, gz, gcol, grow, gdn_norm_w, xbc, sz, dcol, acol, drow, arow, ssm_d,
                       ssm_norm_w)
    k_mem, v_mem = _mem_kv(mem, mem_norm_w, wk_mem, wv_mem)
    return _tail(x, o_a, o_b, k_mem, v_mem, w_out, norm2_w, wq_mem, wo_mem, norm3_w, w_up, w_down,
                 final_norm_w)
```

```python
import jax
import jax.numpy as jnp
from jax import lax
from jax.experimental import pallas as pl
from jax.experimental.pallas import tpu as pltpu

D_MODEL = 1024
EPS = 1e-6
CONV_K = 4
GDN_HEADS = 8
GDN_DK = 128
GDN_DV = 128
GDN_DIM = GDN_HEADS * GDN_DV
GDN_CHUNK = 64
SSM_DIM = D_MODEL
SSM_HEADDIM = 64
SSM_HEADS = SSM_DIM // SSM_HEADDIM
SSM_GROUPS = 2
SSM_HPG = SSM_HEADS // SSM_GROUPS
SSM_STATE = 128
SSM_CHUNK = 128
SSM_BC = SSM_GROUPS * SSM_STATE
SSM_XBC = SSM_DIM + 2 * SSM_BC
SSM_GROUP_DIM = SSM_DIM // SSM_GROUPS
MEM_HEADS = 4
MEM_HEADDIM = D_MODEL // MEM_HEADS
D_FF = 4 * D_MODEL

F32 = jnp.float32
BF16 = jnp.bfloat16

V7X_VMEM_BYTES = 64 * 1024 * 1024
VMEM_LIMIT_BYTES = V7X_VMEM_BYTES - 8 * 1024 * 1024
SUBLANES = 8
LANES = 128

ROW_TILE = 512
ROW_GROUPS = ROW_TILE // SUBLANES
assert ROW_GROUPS == GDN_CHUNK and SSM_CHUNK == 2 * GDN_CHUNK
TAIL_ROWS = 512
COL_GROUP = 512
FF_GROUP = 1024
HALO = (CONV_K - 1) * SUBLANES


def _mm(a, b):
    return jnp.dot(a.astype(BF16), b.astype(BF16), preferred_element_type=F32)


def _mm_nt(a, b):
    return lax.dot_general(a.astype(BF16), b.astype(BF16), (((1,), (1,)), ((), ())),
                           preferred_element_type=F32)


def _mm_tn(a, b):
    return lax.dot_general(a.astype(BF16), b.astype(BF16), (((0,), (0,)), ((), ())),
                           preferred_element_type=F32)


SPLIT = 3


def _split(x):
    hi = x.astype(BF16)
    rest = x - hi.astype(F32)
    mid = rest.astype(BF16)
    lo = (rest - mid.astype(F32)).astype(BF16)
    return hi, mid, lo


def _tri_dot_left(tri_wide, x):
    return jnp.dot(tri_wide, jnp.concatenate(_split(x), axis=0), preferred_element_type=F32)


def _tri_dot_right(x, tri_tall):
    return jnp.dot(jnp.concatenate(_split(x), axis=1), tri_tall, preferred_element_type=F32)


def _rmsnorm(x, w):
    return x * lax.rsqrt(jnp.mean(x * x, axis=-1, keepdims=True) + EPS) * w


def _sigmoid(x):
    return 1.0 / (1.0 + jnp.exp(-x))


def _silu(x):
    half = 0.5 * x
    return half * (1.0 + jnp.tanh(half))


def _softplus(x):
    return jnp.maximum(x, 0.0) + jnp.log(1.0 + jnp.exp(-jnp.abs(x)))


def _const_spec(shape):
    zeros = (0,) * len(shape)
    return pl.BlockSpec(shape, lambda *_: zeros, pipeline_mode=pl.Buffered(1))


def _time_rows(chunk, size):
    start = chunk * size
    return pl.ds(start if isinstance(start, int) else pl.multiple_of(start, size), size)


def _row_pair_words(y):
    return pltpu.bitcast(y.astype(BF16), jnp.int32)


def _chunk_pair(ref, lead, pair):
    words = ref[lead + (pl.ds(pair, ROW_GROUPS, stride=SUBLANES // 2), slice(None))]
    even = pltpu.bitcast(lax.shift_left(words, 16), F32)
    odd = pltpu.bitcast(lax.bitwise_and(words, jnp.int32(-65536)), F32)
    return even, odd


def _conv_silu(p, halo, w, bias):
    rows = p.shape[0]
    sub0 = lax.broadcasted_iota(jnp.int32, (SUBLANES, p.shape[1]), 0) == 0
    wrapped = []
    for j in range(CONV_K - 1):
        cur = p[rows - HALO + j * SUBLANES:rows - HALO + (j + 1) * SUBLANES]
        prev = halo[j * SUBLANES:(j + 1) * SUBLANES]
        wrapped.append(jnp.where(sub0, pltpu.roll(prev, 1, 0), pltpu.roll(cur, 1, 0)))
    acc = p * w[CONV_K - 1:CONV_K, :]
    if bias is not None:
        acc = acc + bias
    for k in range(1, CONV_K):
        shifted = jnp.concatenate(wrapped[CONV_K - 1 - k:] + [p[:rows - k * SUBLANES]], axis=0)
        acc = acc + shifted * w[CONV_K - 1 - k:CONV_K - k, :]
    return _silu(acc)


def _in_proj_kernel(x_ref, n1_ref, perm_ref, w_ref, wgc_ref, wgr_ref,
                    wdc_ref, wdr_ref, gconv_ref, sconv_ref, sconvb_ref,
                    galog_c_ref, gbias_c_ref, galog_r_ref, gbias_r_ref, dbias_c_ref, dbias_r_ref,
                    ssm_a_c_ref, ssm_a_r_ref,
                    qkv_out, gz_out, gcol_out, grow_out, sz_out, xbc_out, dcol_out, drow_out,
                    acol_out, arow_out, ghalo_ref, shalo_ref):
    rows = x_ref.shape[1]

    @pl.when(pl.program_id(1) == 0)
    def _():
        ghalo_ref[...] = jnp.zeros_like(ghalo_ref)
        shalo_ref[...] = jnp.zeros_like(shalo_ref)

    h = _rmsnorm(x_ref[0], n1_ref[...]).astype(BF16)
    hp = jnp.dot(perm_ref[...], h, preferred_element_type=F32).astype(BF16)

    off_gz, off_sz, off_xbc = 3 * GDN_DIM, 4 * GDN_DIM, 4 * GDN_DIM + SSM_DIM
    q_scale = GDN_DK ** -0.5
    for c0 in range(0, 3 * GDN_DIM, COL_GROUP):
        cs = slice(c0, c0 + COL_GROUP)
        p = jnp.dot(hp, w_ref[:, cs], preferred_element_type=F32)
        y = _conv_silu(p, ghalo_ref[:, cs], gconv_ref[:, cs], None)
        ghalo_ref[:, cs] = p[rows - HALO:]
        for j in range(COL_GROUP // LANES):
            yh = y[:, j * LANES:(j + 1) * LANES]
            if c0 < 2 * GDN_DIM:
                yh = yh * lax.rsqrt(jnp.sum(yh * yh, axis=-1, keepdims=True) + EPS)
                if c0 < GDN_DIM:
                    yh = yh * q_scale
            qkv_out[0, c0 // LANES + j] = _row_pair_words(yh)

    for c0 in range(0, GDN_DIM, COL_GROUP):
        gz = _silu(jnp.dot(hp, w_ref[:, off_gz + c0:off_gz + c0 + COL_GROUP],
                           preferred_element_type=F32))
        sz = _silu(jnp.dot(hp, w_ref[:, off_sz + c0:off_sz + c0 + COL_GROUP],
                           preferred_element_type=F32))
        for j in range(COL_GROUP // LANES):
            gz_out[0, c0 // LANES + j] = _row_pair_words(gz[:, j * LANES:(j + 1) * LANES])
            sz_out[0, c0 // LANES + j] = _row_pair_words(sz[:, j * LANES:(j + 1) * LANES])

    for c0 in range(0, SSM_XBC, COL_GROUP):
        cs = slice(c0, c0 + COL_GROUP)
        p = jnp.dot(hp, w_ref[:, off_xbc + c0:off_xbc + c0 + COL_GROUP],
                    preferred_element_type=F32)
        y = _conv_silu(p, shalo_ref[:, cs], sconv_ref[:, cs], sconvb_ref[:, cs])
        shalo_ref[:, cs] = p[rows - HALO:]
        for j in range(COL_GROUP // LANES):
            xbc_out[0, c0 // LANES + j] = _row_pair_words(y[:, j * LANES:(j + 1) * LANES])

    blk = SSM_CHUNK
    ri = lax.broadcasted_iota(jnp.int32, (blk, blk), 0)
    ci = lax.broadcasted_iota(jnp.int32, (blk, blk), 1)
    same_gdn_chunk = (ri // GDN_CHUNK) == (ci // GDN_CHUNK)
    wide = lambda m: jnp.concatenate([m.astype(BF16)] * SPLIT, axis=1)
    tall = lambda m: jnp.concatenate([m.astype(BF16)] * SPLIT, axis=0)
    tril_ssd = wide((ri >= ci).astype(F32))
    triu_ssd = tall((ri <= ci).astype(F32))
    tril_gdn = wide(jnp.logical_and(ri >= ci, same_gdn_chunk).astype(F32))
    triu_gdn = tall(jnp.logical_and(ri <= ci, same_gdn_chunk).astype(F32))

    pg = jnp.dot(h, wgc_ref[...], preferred_element_type=F32)
    gates = jnp.where(lax.broadcasted_iota(jnp.int32, pg.shape, 1) < GDN_HEADS, _sigmoid(pg),
                      -jnp.exp(galog_c_ref[...]) * _softplus(pg + gbias_c_ref[...]))
    pgt = lax.dot_general(wgr_ref[...], h, (((1,), (1,)), ((), ())), preferred_element_type=F32)
    gates_t = jnp.where(lax.broadcasted_iota(jnp.int32, pgt.shape, 0) < GDN_HEADS, _sigmoid(pgt),
                        -jnp.exp(galog_r_ref[...]) * _softplus(pgt + gbias_r_ref[...]))
    dt = _softplus(jnp.dot(h, wdc_ref[...], preferred_element_type=F32) + dbias_c_ref[...])
    pdt = lax.dot_general(wdr_ref[...], h, (((1,), (1,)), ((), ())), preferred_element_type=F32)
    dt_t = _softplus(pdt + dbias_r_ref[...])
    dcol_out[0] = dt
    drow_out[0] = dt_t
    a_col = dt * ssm_a_c_ref[...]
    a_row = dt_t * ssm_a_r_ref[...]
    lane = lax.broadcasted_iota(jnp.int32, (blk, 2 * GDN_HEADS), 1)
    sub = lax.broadcasted_iota(jnp.int32, (2 * GDN_HEADS, blk), 0)
    for r0 in range(0, rows, blk):
        rs = slice(r0, r0 + blk)
        g_blk = gates[rs]
        gcol_out[0, rs, :] = jnp.where(lane < GDN_HEADS, g_blk, _tri_dot_left(tril_gdn, g_blk))
        gt_blk = gates_t[:, rs]
        grow_out[0, :, rs] = jnp.where(sub < GDN_HEADS, gt_blk, _tri_dot_right(gt_blk, triu_gdn))
        acol_out[0, rs, :] = _tri_dot_left(tril_ssd, a_col[rs])
        arow_out[0, :, rs] = _tri_dot_right(a_row[:, rs], triu_ssd)


def _in_proj(x, norm1_w, w_in, gdn_conv_w, gdn_a_log, gdn_dt_bias, ssm_conv_w, ssm_conv_b,
             ssm_dt_bias, ssm_a_log):
    bsz, seq, _ = x.shape
    rows = ROW_TILE
    assert seq % rows == 0
    nh = GDN_HEADS
    off_gates = 4 * GDN_DIM
    off_sz = off_gates + 2 * nh
    off_dt = off_sz + SSM_DIM + SSM_XBC
    assert off_dt + SSM_HEADS == w_in.shape[1]
    w_wide = jnp.concatenate([w_in[:, :off_gates], w_in[:, off_sz:off_dt]], axis=1).astype(BF16)
    w_gates = w_in[:, off_gates:off_sz].astype(BF16)
    w_dt = w_in[:, off_dt:].astype(BF16)
    zeros8 = jnp.zeros((nh,), F32)
    galog16 = jnp.concatenate([zeros8, gdn_a_log.astype(F32)])
    gbias16 = jnp.concatenate([zeros8, gdn_dt_bias.astype(F32)])
    dbias = ssm_dt_bias.astype(F32)
    ssm_a = -jnp.exp(ssm_a_log.astype(F32))
    tile_row = jnp.arange(rows)
    src_time = ROW_GROUPS * (tile_row % SUBLANES) + tile_row // SUBLANES
    perm = (src_time[:, None] == jnp.arange(rows)[None, :]).astype(BF16)

    args = [
        x, norm1_w.reshape(1, D_MODEL), perm, w_wide, w_gates, w_gates.T,
        jnp.tile(w_dt, (1, SPLIT)), w_dt.T, gdn_conv_w, ssm_conv_w, ssm_conv_b.reshape(1, SSM_XBC),
        galog16.reshape(1, 2 * nh), gbias16.reshape(1, 2 * nh),
        galog16.reshape(2 * nh, 1), gbias16.reshape(2 * nh, 1),
        jnp.tile(dbias, SPLIT).reshape(1, SPLIT * SSM_HEADS), dbias.reshape(SSM_HEADS, 1),
        jnp.tile(ssm_a, SPLIT).reshape(1, SPLIT * SSM_HEADS), ssm_a.reshape(SSM_HEADS, 1),
    ]
    in_specs = [pl.BlockSpec((1, rows, D_MODEL), lambda b, t: (b, t, 0))]
    in_specs += [_const_spec(a.shape) for a in args[1:]]

    def slab_spec(cols):
        return pl.BlockSpec((1, cols // LANES, rows // 2, LANES), lambda b, t: (b, 0, t, 0))

    def slab_shape(cols):
        return jax.ShapeDtypeStruct((bsz, cols // LANES, seq // 2, LANES), jnp.int32)

    def row_spec(cols):
        return pl.BlockSpec((1, rows, cols), lambda b, t: (b, t, 0))

    def col_major_spec(nrows):
        return pl.BlockSpec((1, nrows, rows), lambda b, t: (b, 0, t))

    out_shape = [
        slab_shape(3 * GDN_DIM),
        slab_shape(GDN_DIM),
        jax.ShapeDtypeStruct((bsz, seq, 2 * nh), F32),
        jax.ShapeDtypeStruct((bsz, 2 * nh, seq), F32),
        slab_shape(SSM_DIM),
        slab_shape(SSM_XBC),
        jax.ShapeDtypeStruct((bsz, seq, SPLIT * SSM_HEADS), F32),
        jax.ShapeDtypeStruct((bsz, SSM_HEADS, seq), F32),
        jax.ShapeDtypeStruct((bsz, seq, SPLIT * SSM_HEADS), F32),
        jax.ShapeDtypeStruct((bsz, SSM_HEADS, seq), F32),
    ]
    out_specs = [slab_spec(3 * GDN_DIM), slab_spec(GDN_DIM), row_spec(2 * nh),
                 col_major_spec(2 * nh), slab_spec(SSM_DIM), slab_spec(SSM_XBC),
                 row_spec(SPLIT * SSM_HEADS), col_major_spec(SSM_HEADS),
                 row_spec(SPLIT * SSM_HEADS), col_major_spec(SSM_HEADS)]
    return pl.pallas_call(
        _in_proj_kernel,
        grid=(bsz, seq // rows),
        in_specs=in_specs,
        out_specs=out_specs,
        out_shape=out_shape,
        scratch_shapes=[pltpu.VMEM((HALO, 3 * GDN_DIM), F32),
                        pltpu.VMEM((HALO, SSM_XBC), F32)],
        compiler_params=pltpu.CompilerParams(
            dimension_semantics=("arbitrary", "arbitrary"), vmem_limit_bytes=VMEM_LIMIT_BYTES),
        name="in_proj",
    )(*args)


GDN_PAIR = 2
GDN_CHAINS = GDN_PAIR * GDN_HEADS


def _gdn_prepare(qkv_ref, gcol_ref, grow_ref, pair):
    c = GDN_CHUNK
    ri = lax.broadcasted_iota(jnp.int32, (c, 2 * c), 0)
    lane = lax.broadcasted_iota(jnp.int32, (c, 2 * c), 1)
    ci = lane % c
    low = lane < c
    causal = ri >= ci
    strict = ri > ci
    same16 = (ri // 16) == (ci // 16)
    same32 = (ri // 32) == (ci // 32)
    only32 = jnp.logical_and(same32, jnp.logical_not(same16))
    eye = (ri == ci).astype(F32)
    zeros_k = jnp.zeros((c, GDN_DK), BF16)
    zeros_uw = jnp.zeros((c, GDN_DV + GDN_DK), BF16)

    def blockdiag(x_p):
        return jnp.concatenate([jnp.where(low, x_p, 0.0), jnp.where(low, 0.0, x_p)],
                               axis=0).astype(BF16)

    def mm_packed(l_p, x_p):
        return jnp.dot(l_p.astype(BF16), blockdiag(x_p), preferred_element_type=F32)

    q, k, kbs, rhs, exp_g, kdt, cdec = [], [], [], [], [], [], []
    q_pair = [_chunk_pair(qkv_ref, (0, h), pair) for h in range(GDN_HEADS)]
    k_pair = [_chunk_pair(qkv_ref, (0, GDN_HEADS + h), pair) for h in range(GDN_HEADS)]
    v_pair = [_chunk_pair(qkv_ref, (0, 2 * GDN_HEADS + h), pair) for h in range(GDN_HEADS)]
    gram_lhs, gram_rhs, decay = [], [], []
    for j in range(GDN_PAIR):
        s = GDN_PAIR * pair + j
        gcol = gcol_ref[0, _time_rows(s, c), :]
        grow = grow_ref[0, s]
        gc_wide = []
        for h in range(GDN_HEADS):
            q.append(q_pair[h][j])
            k.append(k_pair[h][j])
            vv = v_pair[h][j]
            beta = jnp.broadcast_to(gcol[:, h:h + 1], (c, GDN_DK))
            gcw = jnp.broadcast_to(gcol[:, GDN_HEADS + h:GDN_HEADS + h + 1], (c, GDN_DK))
            g_last = gcw[c - 1:c, :]
            gc_wide.append(gcw)
            exp_g.append(jnp.exp(gcw))
            kbs.append(k[-1] * beta)
            rhs.append(jnp.concatenate([vv * beta, kbs[-1] * exp_g[-1]], axis=1).astype(BF16))
            kdt.append((k[-1] * jnp.exp(g_last - gcw)).T.astype(BF16))
            cdec.append(jnp.broadcast_to(jnp.exp(g_last), (SUBLANES, GDN_DV)))
            if h % 2 == 1:
                a, b = len(q) - 2, len(q) - 1
                gram_lhs.append(jnp.concatenate(
                    [jnp.concatenate([kbs[a], kbs[b]], axis=1),
                     jnp.concatenate([q[a], q[b]], axis=1)], axis=0).astype(BF16))
                gram_rhs.append(jnp.concatenate(
                    [jnp.concatenate([k[a].astype(BF16), zeros_k], axis=1),
                     jnp.concatenate([zeros_k, k[b].astype(BF16)], axis=1)], axis=0))
                gc_p = jnp.where(low, gc_wide[h - 1], gc_wide[h])
                gr_p = grow[h // 2:h // 2 + 1, :]
                decay.append(jnp.exp(jnp.where(causal, gc_p - gr_p, -jnp.inf)))
    gram = [lax.dot_general(l, r, (((1,), (1,)), ((), ())), preferred_element_type=F32)
            for l, r in zip(gram_lhs, gram_rhs)]
    yield None
    a_strict = [jnp.where(strict, g[:c] * d, 0.0) for g, d in zip(gram, decay)]
    attn = [(g[c:] * d).astype(BF16) for g, d in zip(gram, decay)]
    d16 = [jnp.where(same16, a, 0.0) for a in a_strict]
    inv = [eye - x for x in d16]
    xp = [mm_packed(x, x) for x in d16]
    yield None
    for step in range(3):
        inv = [i + mm_packed(i, x) for i, x in zip(inv, xp)]
        if step < 2:
            xp = [mm_packed(x, x) for x in xp]
        yield None
    for mask in (only32, jnp.logical_not(same32)):
        tmp = [mm_packed(i, jnp.where(mask, a, 0.0)) for i, a in zip(inv, a_strict)]
        yield None
        inv = [i - mm_packed(t, i) for i, t in zip(inv, tmp)]
        yield None
    sol = [jnp.dot(t.astype(BF16),
                   jnp.concatenate([jnp.concatenate([rhs[2 * i], zeros_uw], axis=1),
                                    jnp.concatenate([zeros_uw, rhs[2 * i + 1]], axis=1)], axis=0),
                   preferred_element_type=F32) for i, t in enumerate(inv)]
    width = GDN_DV + GDN_DK
    sol = [x[:, half * width:(half + 1) * width] for x in sol for half in range(2)]
    u = [x[:, :GDN_DV] for x in sol]
    wq = [jnp.concatenate([x[:, GDN_DV:], qq * e], axis=0).astype(BF16)
          for x, qq, e in zip(sol, q, exp_g)]
    yield u, wq, attn, kdt, cdec


def _gdn_recur(u_ref, wq_ref, at_ref, kdt_ref, cd_ref, s_ref, gz_ref, nw_ref, o_ref, pair, j):
    c = GDN_CHUNK
    s = GDN_PAIR * pair + j
    chains = [j * GDN_HEADS + h for h in range(GDN_HEADS)]
    state = [s_ref[h] for h in range(GDN_HEADS)]
    ws_qs = [jnp.dot(wq_ref[ch], st.astype(BF16), preferred_element_type=F32)
             for ch, st in zip(chains, state)]
    yield None
    v_new = [(u_ref[ch] - x[:c]).astype(BF16) for ch, x in zip(chains, ws_qs)]
    zeros_v = jnp.zeros((c, GDN_DV), BF16)
    o = []
    for p in range(GDN_HEADS // 2):
        vn = jnp.concatenate([jnp.concatenate([v_new[2 * p], zeros_v], axis=1),
                              jnp.concatenate([zeros_v, v_new[2 * p + 1]], axis=1)], axis=0)
        o_p = jnp.dot(at_ref[j * (GDN_HEADS // 2) + p], vn, preferred_element_type=F32)
        o.append(ws_qs[2 * p][c:] + o_p[:, :GDN_DV])
        o.append(ws_qs[2 * p + 1][c:] + o_p[:, GDN_DV:])
    for h, ch in enumerate(chains):
        s_ref[h] = (state[h] * cd_ref[ch][:1]
                    + jnp.dot(kdt_ref[ch], v_new[h], preferred_element_type=F32))
    rows = _time_rows(s, c)
    for h in range(GDN_HEADS):
        gate = _chunk_pair(gz_ref, (0, h), pair)[j]
        o_ref[0, rows, h * GDN_DV:(h + 1) * GDN_DV] = (
            _rmsnorm(o[h], nw_ref[...]) * gate).astype(o_ref.dtype)
    yield None


def _spread_lanes(cols, onehot):
    heads = cols.shape[1] // SPLIT
    replica = lax.broadcasted_iota(jnp.int32, cols.shape, 1) // heads
    hi, mid, lo = _split(cols)
    terms = jnp.where(replica == 0, hi, jnp.where(replica == 1, mid, lo))
    return jnp.dot(terms, onehot, preferred_element_type=F32)


def _ssd_chunk(xbc_ref, sz_ref, dcol_ref, acol_ref, drow_ref, arow_ref, dskip_ref, nw_ref,
               e64_ref, e128_ref, o_ref, h_ref, n):
    c = SSM_CHUNK
    sub_per_chunk = c // GDN_CHUNK
    x_slabs = SSM_DIM // LANES
    ri = lax.broadcasted_iota(jnp.int32, (c, c), 0)
    ci = lax.broadcasted_iota(jnp.int32, (c, c), 1)
    causal = ri >= ci
    pair_w = 2 * SSM_HEADDIM
    low_half = lax.broadcasted_iota(jnp.int32, (c, pair_w), 1) < SSM_HEADDIM
    high_half = jnp.logical_not(low_half)
    gd = SSM_GROUP_DIM
    pairs_per_group = SSM_HPG // 2
    groups = range(SSM_GROUPS)
    rs = _time_rows(n, c)
    s0 = n * sub_per_chunk

    def slab(ref, idx):
        return jnp.concatenate(_chunk_pair(ref, (0, idx), n), axis=0)

    dt_col = dcol_ref[0, rs, :]
    acs_col = acol_ref[0, rs, :]
    dt_row = drow_ref[0, n]
    acs_row = arow_ref[0, n]
    a_last = acs_col[c - 1:c, :]
    coef = jnp.exp(a_last - acs_col) * dt_col
    chunk_decay = jnp.broadcast_to(jnp.exp(a_last), (SUBLANES, a_last.shape[1]))
    acs_wide = _spread_lanes(acs_col, e128_ref[...])
    wide = _spread_lanes(jnp.concatenate([coef, jnp.exp(acs_col), chunk_decay], axis=0),
                         e64_ref[...])
    coef_wide, exp_a_wide, cd_wide = wide[:c], wide[c:2 * c], wide[2 * c:2 * c + 1]
    yield None
    bg = [slab(xbc_ref, x_slabs + g) for g in groups]
    cg = [slab(xbc_ref, x_slabs + SSM_GROUPS + g) for g in groups]
    cb = [_mm_nt(cg[g], bg[g]) for g in groups]
    h_prev = [h_ref[g] for g in groups]
    y_off = [_mm(cg[g], h_prev[g]) * exp_a_wide[:, g * gd:(g + 1) * gd] for g in groups]
    yield None
    x_pairs = [[slab(xbc_ref, g * pairs_per_group + jp) for jp in range(pairs_per_group)]
               for g in groups]
    x_g = [jnp.concatenate(x_pairs[g], axis=1) for g in groups]
    for g in groups:
        gs = slice(g * gd, (g + 1) * gd)
        h_ref[g] = h_prev[g] * cd_wide[:, gs] + _mm_tn(bg[g], x_g[g] * coef_wide[:, gs])
    yield None
    y_pairs = [[], []]
    for g in groups:
        for jp in range(pairs_per_group):
            hd = g * SSM_HPG + 2 * jp
            acc = y_off[g][:, jp * pair_w:(jp + 1) * pair_w]
            for half, mask in enumerate((low_half, high_half)):
                ac = acs_wide[:, (hd + half) * LANES:(hd + half + 1) * LANES]
                ar = acs_row[hd + half:hd + half + 1, :]
                seg = jnp.exp(jnp.where(causal, ac - ar, -jnp.inf))
                scores = seg * (cb[g] * dt_row[hd + half:hd + half + 1, :])
                acc = acc + _mm(scores, jnp.where(mask, x_pairs[g][jp], 0.0))
            y_pairs[g].append(acc)
            if jp % 2 == 1:
                yield None
    for g in groups:
        gs = slice(g * gd, (g + 1) * gd)
        y = jnp.concatenate(y_pairs[g], axis=1) + x_g[g] * dskip_ref[:, gs]
        gate = jnp.concatenate(
            [slab(sz_ref, g * pairs_per_group + jp) for jp in range(pairs_per_group)], axis=1)
        yg = y * gate
        yg = yg * lax.rsqrt(jnp.mean(yg * yg, axis=-1, keepdims=True) + EPS)
        o_ref[0, rs, gs] = (yg * nw_ref[:, gs]).astype(o_ref.dtype)
    yield None


def _interleave(*gens):
    last = [None] * len(gens)
    live = list(range(len(gens)))
    while live:
        for i in list(live):
            try:
                last[i] = next(gens[i])
            except StopIteration:
                live.remove(i)
    return last


def _spaced(gen, gap):
    for value in gen:
        yield value
        for _ in range(gap):
            yield value


def _mixers_kernel(qkv_ref, gz_ref, gcol_ref, grow_ref, gnw_ref,
                   xbc_ref, sz_ref, dcol_ref, acol_ref, drow_ref, arow_ref, dskip_ref, snw_ref,
                   e64_ref, e128_ref, oa_ref, ob_ref,
                   s_ref, u_ref, wq_ref, at_ref, kdt_ref, cd_ref, h_ref):
    npairs = ROW_TILE // (GDN_PAIR * GDN_CHUNK)
    assert npairs == ROW_TILE // SSM_CHUNK

    @pl.when(pl.program_id(1) == 0)
    def _():
        s_ref[...] = jnp.zeros_like(s_ref)
        h_ref[...] = jnp.zeros_like(h_ref)

    def store_prepared(prepared):
        u, wq, attn, kdt, cdec = prepared
        for ch in range(GDN_CHAINS):
            u_ref[ch] = u[ch]
            wq_ref[ch] = wq[ch]
            kdt_ref[ch] = kdt[ch]
            cd_ref[ch] = cdec[ch]
        for pc in range(GDN_CHAINS // 2):
            at_ref[pc] = attn[pc]

    def recur(pair):
        for j in range(GDN_PAIR):
            yield from _gdn_recur(u_ref, wq_ref, at_ref, kdt_ref, cd_ref, s_ref, gz_ref, gnw_ref,
                                  oa_ref, pair, j)

    def ssd(n):
        return _ssd_chunk(xbc_ref, sz_ref, dcol_ref, acol_ref, drow_ref, arow_ref, dskip_ref,
                          snw_ref, e64_ref, e128_ref, ob_ref, h_ref, n)

    prepared, _ = _interleave(_gdn_prepare(qkv_ref, gcol_ref, grow_ref, 0), ssd(0))
    store_prepared(prepared)

    def body(pair, carry):
        prepared, _, _ = _interleave(_gdn_prepare(qkv_ref, gcol_ref, grow_ref, pair + 1),
                                     _spaced(recur(pair), 1), ssd(pair + 1))
        store_prepared(prepared)
        return carry

    lax.fori_loop(0, npairs - 1, body, 0)

    _interleave(recur(npairs - 1))


def _mixers(qkv, gz, gcol, grow, gdn_norm_w, xbc, sz, dcol, acol, drow, arow, d_skip, ssm_norm_w):
    bsz, seq, _ = gcol.shape
    rows = ROW_TILE
    nh2 = 2 * GDN_HEADS
    nh = SSM_HEADS
    c = GDN_CHUNK
    grow_c = grow[:, GDN_HEADS:].reshape(bsz, GDN_HEADS, seq // c, c).transpose(0, 2, 1, 3)
    grow_c = grow_c.reshape(bsz, seq // c, GDN_HEADS // 2, 2 * c)
    per_chunk = lambda a: a.reshape(bsz, nh, seq // SSM_CHUNK, SSM_CHUNK).transpose(0, 2, 1, 3)
    dskip = jnp.repeat(d_skip.astype(F32), SSM_HEADDIM).reshape(1, SSM_DIM)
    eye = jnp.tile(jnp.eye(nh, dtype=BF16), (SPLIT, 1))
    e64 = jnp.repeat(eye, SSM_HEADDIM, axis=1)
    e128 = jnp.repeat(eye, LANES, axis=1)

    def slab_spec(cols):
        return pl.BlockSpec((1, cols // LANES, rows // 2, LANES), lambda b, t: (b, 0, t, 0))

    def row_spec(cols):
        return pl.BlockSpec((1, rows, cols), lambda b, t: (b, t, 0))

    def per_chunk_spec(nrows, chunk):
        return pl.BlockSpec((1, rows // chunk, nrows, chunk), lambda b, t: (b, t, 0, 0))

    out_shape = jax.ShapeDtypeStruct((bsz, seq, D_MODEL), BF16)
    return pl.pallas_call(
        _mixers_kernel,
        grid=(bsz, seq // rows),
        in_specs=[
            slab_spec(3 * GDN_DIM), slab_spec(GDN_DIM), row_spec(nh2),
            pl.BlockSpec((1, rows // c, GDN_HEADS // 2, 2 * c), lambda b, t: (b, t, 0, 0)),
            _const_spec((1, GDN_DV)),
            slab_spec(SSM_XBC), slab_spec(SSM_DIM), row_spec(SPLIT * nh), row_spec(SPLIT * nh),
            per_chunk_spec(nh, SSM_CHUNK), per_chunk_spec(nh, SSM_CHUNK),
            _const_spec((1, SSM_DIM)), _const_spec((1, SSM_DIM)),
            _const_spec((SPLIT * nh, SSM_DIM)), _const_spec((SPLIT * nh, nh * LANES)),
        ],
        out_specs=[row_spec(GDN_DIM), row_spec(SSM_DIM)],
        out_shape=[out_shape, out_shape],
        scratch_shapes=[
            pltpu.VMEM((GDN_HEADS, GDN_DK, GDN_DV), F32),
            pltpu.VMEM((GDN_CHAINS, c, GDN_DV), F32),
            pltpu.VMEM((GDN_CHAINS, 2 * c, GDN_DK), BF16),
            pltpu.VMEM((GDN_CHAINS // 2, c, 2 * c), BF16),
            pltpu.VMEM((GDN_CHAINS, GDN_DK, c), BF16),
            pltpu.VMEM((GDN_CHAINS, SUBLANES, GDN_DV), F32),
            pltpu.VMEM((SSM_GROUPS, SSM_STATE, SSM_GROUP_DIM), F32),
        ],
        compiler_params=pltpu.CompilerParams(
            dimension_semantics=("arbitrary", "arbitrary"), vmem_limit_bytes=VMEM_LIMIT_BYTES),
        name="mixers",
    )(qkv, gz, gcol, grow_c, gdn_norm_w.reshape(1, GDN_DV).astype(F32),
      xbc, sz, dcol, acol, per_chunk(drow), per_chunk(arow), dskip,
      ssm_norm_w.reshape(1, SSM_DIM).astype(F32), e64, e128)


def _mem_kv_kernel(mem_ref, nw_ref, wk_ref, wv_ref, k_out, v_out):
    m = _rmsnorm(mem_ref[0], nw_ref[...]).astype(BF16)
    k_out[0] = jnp.dot(m, wk_ref[...], preferred_element_type=F32).astype(k_out.dtype)
    v_out[0] = jnp.dot(m, wv_ref[...], preferred_element_type=F32).astype(v_out.dtype)


def _mem_kv(mem, mem_norm_w, wk, wv):
    bsz, mlen, _ = mem.shape
    kv_shape = jax.ShapeDtypeStruct((bsz, mlen, D_MODEL), BF16)
    return pl.pallas_call(
        _mem_kv_kernel,
        grid=(bsz,),
        in_specs=[pl.BlockSpec((1, mlen, D_MODEL), lambda b: (b, 0, 0)),
                  _const_spec((1, D_MODEL)),
                  _const_spec((D_MODEL, D_MODEL)),
                  _const_spec((D_MODEL, D_MODEL))],
        out_specs=[pl.BlockSpec((1, mlen, D_MODEL), lambda b: (b, 0, 0))] * 2,
        out_shape=[kv_shape, kv_shape],
        compiler_params=pltpu.CompilerParams(
            dimension_semantics=("arbitrary",), vmem_limit_bytes=VMEM_LIMIT_BYTES),
        name="mem_kv",
    )(mem, mem_norm_w.reshape(1, D_MODEL), wk.astype(BF16), wv.astype(BF16))


def _tail_kernel(x_ref, oa_ref, ob_ref, k_ref, v_ref, wout_ref, n2_ref, wq_ref, wo_ref, n3_ref,
                 wup_ref, wdown_ref, nf_ref, y_ref):
    x = x_ref[0]
    x = x + jnp.dot(oa_ref[0], wout_ref[:GDN_DIM, :], preferred_element_type=F32)
    x = x + jnp.dot(ob_ref[0], wout_ref[GDN_DIM:, :], preferred_element_type=F32)

    h = _rmsnorm(x, n2_ref[...]).astype(BF16)
    q = jnp.dot(h, wq_ref[...], preferred_element_type=F32)
    scale = MEM_HEADDIM ** -0.5
    heads = []
    for hd in range(MEM_HEADS):
        hs = slice(hd * MEM_HEADDIM, (hd + 1) * MEM_HEADDIM)
        s = _mm_nt(q[:, hs], k_ref[0, :, hs]) * scale
        s = s - jnp.max(s, axis=-1, keepdims=True)
        p = jnp.exp(s)
        p = p / jnp.sum(p, axis=-1, keepdims=True)
        heads.append(_mm(p, v_ref[0, :, hs]))
    attn = jnp.concatenate(heads, axis=1)
    x = x + _mm(attn, wo_ref[...])

    h = _rmsnorm(x, n3_ref[...]).astype(BF16)
    acc = x
    for c0 in range(0, D_FF, FF_GROUP):
        u = jnp.maximum(jnp.dot(h, wup_ref[:, c0:c0 + FF_GROUP], preferred_element_type=F32), 0.0)
        acc = acc + _mm(u * u, wdown_ref[c0:c0 + FF_GROUP, :])
    y_ref[0] = _rmsnorm(acc, nf_ref[...])


def _tail(x, o_a, o_b, k_mem, v_mem, w_out, norm2_w, wq, wo, norm3_w, w_up, w_down, final_w):
    bsz, seq, _ = x.shape
    rows = min(TAIL_ROWS, seq)
    mlen = k_mem.shape[1]
    row_spec = pl.BlockSpec((1, rows, D_MODEL), lambda b, t: (b, t, 0))
    mem_spec = pl.BlockSpec((1, mlen, D_MODEL), lambda b, t: (b, 0, 0))
    vec = lambda w: w.reshape(1, D_MODEL).astype(F32)
    return pl.pallas_call(
        _tail_kernel,
        grid=(bsz, seq // rows),
        in_specs=[row_spec, row_spec, row_spec, mem_spec, mem_spec,
                  _const_spec((GDN_DIM + SSM_DIM, D_MODEL)), _const_spec((1, D_MODEL)),
                  _const_spec((D_MODEL, D_MODEL)), _const_spec((D_MODEL, D_MODEL)),
                  _const_spec((1, D_MODEL)), _const_spec((D_MODEL, D_FF)),
                  _const_spec((D_FF, D_MODEL)), _const_spec((1, D_MODEL))],
        out_specs=row_spec,
        out_shape=jax.ShapeDtypeStruct((bsz, seq, D_MODEL), x.dtype),
        compiler_params=pltpu.CompilerParams(
            dimension_semantics=("arbitrary", "arbitrary"), vmem_limit_bytes=VMEM_LIMIT_BYTES),
        name="tail",
    )(x, o_a, o_b, k_mem, v_mem, w_out.astype(BF16), vec(norm2_w), wq.astype(BF16),
      wo.astype(BF16), vec(norm3_w), w_up.astype(BF16), w_down.astype(BF16), vec(final_w))


def kernel(x, mem, norm1_w, w_in, gdn_conv_w, gdn_a_log, gdn_dt_bias, gdn_norm_w, ssm_conv_w,
           ssm_conv_b, ssm_a_log, ssm_dt_bias, ssm_d, ssm_norm_w, w_out, norm2_w, mem_norm_w,
           wq_mem, wk_mem, wv_mem, wo_mem, norm3_w, w_up, w_down, final_norm_w):
    qkv, gz, gcol, grow, sz, xbc, dcol, drow, acol, arow = _in_proj(
        x, norm1_w, w_in, gdn_conv_w, gdn_a_log, gdn_dt_bias, ssm_conv_w, ssm_conv_b, ssm_dt_bias,
        ssm_a_log)
    o_a, o_b = _mixers(qkv, gz, gcol, grow, gdn_norm_w, xbc, sz, dcol, acol, drow, arow, ssm_d,
                       ssm_norm_w)
    k_mem, v_mem = _mem_kv(mem, mem_norm_w, wk_mem, wv_mem)
    return _tail(x, o_a, o_b, k_mem, v_mem, w_out, norm2_w, wq_mem, wo_mem, norm3_w, w_up, w_down,
                 final_norm_w)
```

```python
import jax
import jax.numpy as jnp
from jax import lax
from jax.experimental import pallas as pl
from jax.experimental.pallas import tpu as pltpu

D_MODEL = 1024
EPS = 1e-6
CONV_K = 4
GDN_HEADS = 8
GDN_DK = 128
GDN_DV = 128
GDN_DIM = GDN_HEADS * GDN_DV
GDN_CHUNK = 64
SSM_DIM = D_MODEL
SSM_HEADDIM = 64
SSM_HEADS = SSM_DIM // SSM_HEADDIM
SSM_GROUPS = 2
SSM_HPG = SSM_HEADS // SSM_GROUPS
SSM_STATE = 128
SSM_CHUNK = 128
SSM_BC = SSM_GROUPS * SSM_STATE
SSM_XBC = SSM_DIM + 2 * SSM_BC
SSM_GROUP_DIM = SSM_DIM // SSM_GROUPS
MEM_HEADS = 4
MEM_HEADDIM = D_MODEL // MEM_HEADS
D_FF = 4 * D_MODEL

F32 = jnp.float32
BF16 = jnp.bfloat16

V7X_VMEM_BYTES = 64 * 1024 * 1024
VMEM_LIMIT_BYTES = V7X_VMEM_BYTES - 8 * 1024 * 1024
SUBLANES = 8
LANES = 128

ROW_TILE = 512
ROW_GROUPS = ROW_TILE // SUBLANES
assert ROW_GROUPS == GDN_CHUNK and SSM_CHUNK == 2 * GDN_CHUNK
TAIL_ROWS = 512
COL_GROUP = 512
FF_GROUP = 1024
HALO = (CONV_K - 1) * SUBLANES


def _mm(a, b):
    return jnp.dot(a.astype(BF16), b.astype(BF16), preferred_element_type=F32)


def _mm_nt(a, b):
    return lax.dot_general(a.astype(BF16), b.astype(BF16), (((1,), (1,)), ((), ())),
                           preferred_element_type=F32)


def _mm_tn(a, b):
    return lax.dot_general(a.astype(BF16), b.astype(BF16), (((0,), (0,)), ((), ())),
                           preferred_element_type=F32)


SPLIT = 3


def _split(x):
    hi = x.astype(BF16)
    rest = x - hi.astype(F32)
    mid = rest.astype(BF16)
    lo = (rest - mid.astype(F32)).astype(BF16)
    return hi, mid, lo


def _tri_dot_left(tri_wide, x):
    return jnp.dot(tri_wide, jnp.concatenate(_split(x), axis=0), preferred_element_type=F32)


def _tri_dot_right(x, tri_tall):
    return jnp.dot(jnp.concatenate(_split(x), axis=1), tri_tall, preferred_element_type=F32)


def _rmsnorm(x, w):
    return x * lax.rsqrt(jnp.mean(x * x, axis=-1, keepdims=True) + EPS) * w


def _sigmoid(x):
    return 1.0 / (1.0 + jnp.exp(-x))


def _silu(x):
    half = 0.5 * x
    return half * (1.0 + jnp.tanh(half))


def _softplus(x):
    return jnp.maximum(x, 0.0) + jnp.log(1.0 + jnp.exp(-jnp.abs(x)))


def _const_spec(shape):
    zeros = (0,) * len(shape)
    return pl.BlockSpec(shape, lambda *_: zeros, pipeline_mode=pl.Buffered(1))


def _time_rows(chunk, size):
    start = chunk * size
    return pl.ds(start if isinstance(start, int) else pl.multiple_of(start, size), size)


def _row_pair_words(y):
    return pltpu.bitcast(y.astype(BF16), jnp.int32)


def _chunk_pair(ref, lead, pair):
    words = ref[lead + (pl.ds(pair, ROW_GROUPS, stride=SUBLANES // 2), slice(None))]
    even = pltpu.bitcast(lax.shift_left(words, 16), F32)
    odd = pltpu.bitcast(lax.bitwise_and(words, jnp.int32(-65536)), F32)
    return even, odd


def _conv_silu(p, halo, w, bias):
    rows = p.shape[0]
    sub0 = lax.broadcasted_iota(jnp.int32, (SUBLANES, p.shape[1]), 0) == 0
    wrapped = []
    for j in range(CONV_K - 1):
        cur = p[rows - HALO + j * SUBLANES:rows - HALO + (j + 1) * SUBLANES]
        prev = halo[j * SUBLANES:(j + 1) * SUBLANES]
        wrapped.append(jnp.where(sub0, pltpu.roll(prev, 1, 0), pltpu.roll(cur, 1, 0)))
    acc = p * w[CONV_K - 1:CONV_K, :]
    if bias is not None:
        acc = acc + bias
    for k in range(1, CONV_K):
        shifted = jnp.concatenate(wrapped[CONV_K - 1 - k:] + [p[:rows - k * SUBLANES]], axis=0)
        acc = acc + shifted * w[CONV_K - 1 - k:CONV_K - k, :]
    return _silu(acc)


def _in_proj_kernel(x_ref, n1_ref, perm_ref, w_ref, ws_ref, wc_ref, wr_ref,
                    gconv_ref, sconv_ref, sconvb_ref,
                    galog_c_ref, gbias_c_ref, galog_r_ref, gbias_r_ref, dbias_c_ref, dbias_r_ref,
                    ssm_a_c_ref, ssm_a_r_ref,
                    qkv_out, gz_out, gcol_out, grow_out, sz_out, xbc_out, dcol_out, drow_out,
                    acol_out, arow_out, ghalo_ref, shalo_ref):
    rows = x_ref.shape[1]

    @pl.when(pl.program_id(1) == 0)
    def _():
        ghalo_ref[...] = jnp.zeros_like(ghalo_ref)
        shalo_ref[...] = jnp.zeros_like(shalo_ref)

    h = _rmsnorm(x_ref[0], n1_ref[...]).astype(BF16)
    hp = jnp.dot(perm_ref[...], h, preferred_element_type=F32).astype(BF16)

    off_gz, off_sz, off_xbc = 3 * GDN_DIM, 0, SSM_DIM
    q_scale = GDN_DK ** -0.5
    for c0 in range(0, 3 * GDN_DIM, COL_GROUP):
        cs = slice(c0, c0 + COL_GROUP)
        p = jnp.dot(hp, w_ref[:, cs], preferred_element_type=F32)
        y = _conv_silu(p, ghalo_ref[:, cs], gconv_ref[:, cs], None)
        ghalo_ref[:, cs] = p[rows - HALO:]
        for j in range(COL_GROUP // LANES):
            yh = y[:, j * LANES:(j + 1) * LANES]
            if c0 < 2 * GDN_DIM:
                yh = yh * lax.rsqrt(jnp.sum(yh * yh, axis=-1, keepdims=True) + EPS)
                if c0 < GDN_DIM:
                    yh = yh * q_scale
            qkv_out[0, c0 // LANES + j] = _row_pair_words(yh)

    for c0 in range(0, GDN_DIM, COL_GROUP):
        gz = _silu(jnp.dot(hp, w_ref[:, off_gz + c0:off_gz + c0 + COL_GROUP],
                           preferred_element_type=F32))
        sz = _silu(jnp.dot(hp, ws_ref[:, off_sz + c0:off_sz + c0 + COL_GROUP],
                           preferred_element_type=F32))
        for j in range(COL_GROUP // LANES):
            gz_out[0, c0 // LANES + j] = _row_pair_words(gz[:, j * LANES:(j + 1) * LANES])
            sz_out[0, c0 // LANES + j] = _row_pair_words(sz[:, j * LANES:(j + 1) * LANES])

    for c0 in range(0, SSM_XBC, COL_GROUP):
        cs = slice(c0, c0 + COL_GROUP)
        p = jnp.dot(hp, ws_ref[:, off_xbc + c0:off_xbc + c0 + COL_GROUP],
                    preferred_element_type=F32)
        y = _conv_silu(p, shalo_ref[:, cs], sconv_ref[:, cs], sconvb_ref[:, cs])
        shalo_ref[:, cs] = p[rows - HALO:]
        for j in range(COL_GROUP // LANES):
            xbc_out[0, c0 // LANES + j] = _row_pair_words(y[:, j * LANES:(j + 1) * LANES])

    blk = SSM_CHUNK
    ri = lax.broadcasted_iota(jnp.int32, (blk, blk), 0)
    ci = lax.broadcasted_iota(jnp.int32, (blk, blk), 1)
    same_gdn_chunk = (ri // GDN_CHUNK) == (ci // GDN_CHUNK)
    wide = lambda m: jnp.concatenate([m.astype(BF16)] * SPLIT, axis=1)
    tall = lambda m: jnp.concatenate([m.astype(BF16)] * SPLIT, axis=0)
    tril_ssd = wide((ri >= ci).astype(F32))
    triu_ssd = tall((ri <= ci).astype(F32))
    tril_gdn = wide(jnp.logical_and(ri >= ci, same_gdn_chunk).astype(F32))
    triu_gdn = tall(jnp.logical_and(ri <= ci, same_gdn_chunk).astype(F32))

    n_dt = SPLIT * SSM_HEADS
    p_cols = jnp.dot(h, wc_ref[...], preferred_element_type=F32)
    p_rows = lax.dot_general(wr_ref[...], h, (((1,), (1,)), ((), ())),
                             preferred_element_type=F32)
    pg = p_cols[:, n_dt:]
    gates = jnp.where(lax.broadcasted_iota(jnp.int32, pg.shape, 1) < GDN_HEADS, _sigmoid(pg),
                      -jnp.exp(galog_c_ref[...]) * _softplus(pg + gbias_c_ref[...]))
    pgt = p_rows[:2 * GDN_HEADS]
    gates_t = jnp.where(lax.broadcasted_iota(jnp.int32, pgt.shape, 0) < GDN_HEADS, _sigmoid(pgt),
                        -jnp.exp(galog_r_ref[...]) * _softplus(pgt + gbias_r_ref[...]))
    dt = _softplus(p_cols[:, :n_dt] + dbias_c_ref[...])
    dt_t = _softplus(p_rows[2 * GDN_HEADS:] + dbias_r_ref[...])
    dcol_out[0] = dt
    drow_out[0] = dt_t
    a_col = dt * ssm_a_c_ref[...]
    a_row = dt_t * ssm_a_r_ref[...]
    lane = lax.broadcasted_iota(jnp.int32, (blk, 2 * GDN_HEADS), 1)
    sub = lax.broadcasted_iota(jnp.int32, (2 * GDN_HEADS, blk), 0)
    for r0 in range(0, rows, blk):
        rs = slice(r0, r0 + blk)
        g_blk = gates[rs]
        gcol_out[0, rs, :] = jnp.where(lane < GDN_HEADS, g_blk, _tri_dot_left(tril_gdn, g_blk))
        gt_blk = gates_t[:, rs]
        grow_out[0, :, rs] = jnp.where(sub < GDN_HEADS, gt_blk, _tri_dot_right(gt_blk, triu_gdn))
        acol_out[0, rs, :] = _tri_dot_left(tril_ssd, a_col[rs])
        arow_out[0, :, rs] = _tri_dot_right(a_row[:, rs], triu_ssd)


def _in_proj(x, norm1_w, w_in, gdn_conv_w, gdn_a_log, gdn_dt_bias, ssm_conv_w, ssm_conv_b,
             ssm_dt_bias, ssm_a_log):
    bsz, seq, _ = x.shape
    rows = ROW_TILE
    assert seq % rows == 0
    nh = GDN_HEADS
    off_gates = 4 * GDN_DIM
    off_sz = off_gates + 2 * nh
    off_dt = off_sz + SSM_DIM + SSM_XBC
    assert off_dt + SSM_HEADS == w_in.shape[1]
    w_gdn = w_in[:, :off_gates].astype(BF16)
    w_ssd = w_in[:, off_sz:off_dt].astype(BF16)
    w_gates = w_in[:, off_gates:off_sz].astype(BF16)
    w_dt = w_in[:, off_dt:].astype(BF16)
    zeros8 = jnp.zeros((nh,), F32)
    galog16 = jnp.concatenate([zeros8, gdn_a_log.astype(F32)])
    gbias16 = jnp.concatenate([zeros8, gdn_dt_bias.astype(F32)])
    dbias = ssm_dt_bias.astype(F32)
    ssm_a = -jnp.exp(ssm_a_log.astype(F32))
    tile_row = jnp.arange(rows)
    src_time = ROW_GROUPS * (tile_row % SUBLANES) + tile_row // SUBLANES
    perm = (src_time[:, None] == jnp.arange(rows)[None, :]).astype(BF16)

    args = [
        x, norm1_w.reshape(1, D_MODEL), perm, w_gdn, w_ssd,
        jnp.concatenate([jnp.tile(w_dt, (1, SPLIT)), w_gates], axis=1),
        jnp.concatenate([w_gates.T, w_dt.T], axis=0),
        gdn_conv_w, ssm_conv_w, ssm_conv_b.reshape(1, SSM_XBC),
        galog16.reshape(1, 2 * nh), gbias16.reshape(1, 2 * nh),
        galog16.reshape(2 * nh, 1), gbias16.reshape(2 * nh, 1),
        jnp.tile(dbias, SPLIT).reshape(1, SPLIT * SSM_HEADS), dbias.reshape(SSM_HEADS, 1),
        jnp.tile(ssm_a, SPLIT).reshape(1, SPLIT * SSM_HEADS), ssm_a.reshape(SSM_HEADS, 1),
    ]
    in_specs = [pl.BlockSpec((1, rows, D_MODEL), lambda b, t: (b, t, 0))]
    in_specs += [_const_spec(a.shape) for a in args[1:]]

    def slab_spec(cols):
        return pl.BlockSpec((1, cols // LANES, rows // 2, LANES), lambda b, t: (b, 0, t, 0))

    def slab_shape(cols):
        return jax.ShapeDtypeStruct((bsz, cols // LANES, seq // 2, LANES), jnp.int32)

    def row_spec(cols):
        return pl.BlockSpec((1, rows, cols), lambda b, t: (b, t, 0))

    def col_major_spec(nrows):
        return pl.BlockSpec((1, nrows, rows), lambda b, t: (b, 0, t))

    out_shape = [
        slab_shape(3 * GDN_DIM),
        slab_shape(GDN_DIM),
        jax.ShapeDtypeStruct((bsz, seq, 2 * nh), F32),
        jax.ShapeDtypeStruct((bsz, 2 * nh, seq), F32),
        slab_shape(SSM_DIM),
        slab_shape(SSM_XBC),
        jax.ShapeDtypeStruct((bsz, seq, SPLIT * SSM_HEADS), F32),
        jax.ShapeDtypeStruct((bsz, SSM_HEADS, seq), F32),
        jax.ShapeDtypeStruct((bsz, seq, SPLIT * SSM_HEADS), F32),
        jax.ShapeDtypeStruct((bsz, SSM_HEADS, seq), F32),
    ]
    out_specs = [slab_spec(3 * GDN_DIM), slab_spec(GDN_DIM), row_spec(2 * nh),
                 col_major_spec(2 * nh), slab_spec(SSM_DIM), slab_spec(SSM_XBC),
                 row_spec(SPLIT * SSM_HEADS), col_major_spec(SSM_HEADS),
                 row_spec(SPLIT * SSM_HEADS), col_major_spec(SSM_HEADS)]
    return pl.pallas_call(
        _in_proj_kernel,
        grid=(bsz, seq // rows),
        in_specs=in_specs,
        out_specs=out_specs,
        out_shape=out_shape,
        scratch_shapes=[pltpu.VMEM((HALO, 3 * GDN_DIM), F32),
                        pltpu.VMEM((HALO, SSM_XBC), F32)],
        compiler_params=pltpu.CompilerParams(
            dimension_semantics=("arbitrary", "arbitrary"), vmem_limit_bytes=VMEM_LIMIT_BYTES),
        name="in_proj",
    )(*args)


GDN_PAIR = 2
GDN_CHAINS = GDN_PAIR * GDN_HEADS


def _gdn_prepare(qkv_ref, gcol_ref, grow_ref, pair):
    c = GDN_CHUNK
    ri = lax.broadcasted_iota(jnp.int32, (c, 2 * c), 0)
    lane = lax.broadcasted_iota(jnp.int32, (c, 2 * c), 1)
    ci = lane % c
    low = lane < c
    causal = ri >= ci
    strict = ri > ci
    same16 = (ri // 16) == (ci // 16)
    same32 = (ri // 32) == (ci // 32)
    only32 = jnp.logical_and(same32, jnp.logical_not(same16))
    eye = (ri == ci).astype(F32)
    zeros_k = jnp.zeros((c, GDN_DK), BF16)
    zeros_uw = jnp.zeros((c, GDN_DV + GDN_DK), BF16)

    def blockdiag(x_p):
        return jnp.concatenate([jnp.where(low, x_p, 0.0), jnp.where(low, 0.0, x_p)],
                               axis=0).astype(BF16)

    def mm_packed(l_p, x_p):
        return jnp.dot(l_p.astype(BF16), blockdiag(x_p), preferred_element_type=F32)

    q, k, kbs, rhs, exp_g, kdt, cdec = [], [], [], [], [], [], []
    q_pair = [_chunk_pair(qkv_ref, (0, h), pair) for h in range(GDN_HEADS)]
    k_pair = [_chunk_pair(qkv_ref, (0, GDN_HEADS + h), pair) for h in range(GDN_HEADS)]
    v_pair = [_chunk_pair(qkv_ref, (0, 2 * GDN_HEADS + h), pair) for h in range(GDN_HEADS)]
    gram_lhs, gram_rhs, decay = [], [], []
    for j in range(GDN_PAIR):
        s = GDN_PAIR * pair + j
        gcol = gcol_ref[0, _time_rows(s, c), :]
        grow = grow_ref[0, s]
        gc_wide = []
        for h in range(GDN_HEADS):
            q.append(q_pair[h][j])
            k.append(k_pair[h][j])
            vv = v_pair[h][j]
            beta = jnp.broadcast_to(gcol[:, h:h + 1], (c, GDN_DK))
            gcw = jnp.broadcast_to(gcol[:, GDN_HEADS + h:GDN_HEADS + h + 1], (c, GDN_DK))
            g_last = gcw[c - 1:c, :]
            gc_wide.append(gcw)
            exp_g.append(jnp.exp(gcw))
            kbs.append(k[-1] * beta)
            rhs.append(jnp.concatenate([vv * beta, kbs[-1] * exp_g[-1]], axis=1).astype(BF16))
            kdt.append((k[-1] * jnp.exp(g_last - gcw)).T.astype(BF16))
            cdec.append(jnp.broadcast_to(jnp.exp(g_last), (SUBLANES, GDN_DV)))
            if h % 2 == 1:
                a, b = len(q) - 2, len(q) - 1
                gram_lhs.append(jnp.concatenate(
                    [jnp.concatenate([kbs[a], kbs[b]], axis=1),
                     jnp.concatenate([q[a], q[b]], axis=1)], axis=0).astype(BF16))
                gram_rhs.append(jnp.concatenate(
                    [jnp.concatenate([k[a].astype(BF16), zeros_k], axis=1),
                     jnp.concatenate([zeros_k, k[b].astype(BF16)], axis=1)], axis=0))
                gc_p = jnp.where(low, gc_wide[h - 1], gc_wide[h])
                gr_p = grow[h // 2:h // 2 + 1, :]
                decay.append(jnp.exp(jnp.where(causal, gc_p - gr_p, -jnp.inf)))
    gram = [lax.dot_general(l, r, (((1,), (1,)), ((), ())), preferred_element_type=F32)
            for l, r in zip(gram_lhs, gram_rhs)]
    yield None
    a_strict = [jnp.where(strict, g[:c] * d, 0.0) for g, d in zip(gram, decay)]
    attn = [(g[c:] * d).astype(BF16) for g, d in zip(gram, decay)]
    d16 = [jnp.where(same16, a, 0.0) for a in a_strict]
    inv = [eye - x for x in d16]
    xp = [mm_packed(x, x) for x in d16]
    yield None
    for step in range(3):
        inv = [i + mm_packed(i, x) for i, x in zip(inv, xp)]
        if step < 2:
            xp = [mm_packed(x, x) for x in xp]
        yield None
    for mask in (only32, jnp.logical_not(same32)):
        tmp = [mm_packed(i, jnp.where(mask, a, 0.0)) for i, a in zip(inv, a_strict)]
        yield None
        inv = [i - mm_packed(t, i) for i, t in zip(inv, tmp)]
        yield None
    sol = [jnp.dot(t.astype(BF16),
                   jnp.concatenate([jnp.concatenate([rhs[2 * i], zeros_uw], axis=1),
                                    jnp.concatenate([zeros_uw, rhs[2 * i + 1]], axis=1)], axis=0),
                   preferred_element_type=F32) for i, t in enumerate(inv)]
    width = GDN_DV + GDN_DK
    sol = [x[:, half * width:(half + 1) * width] for x in sol for half in range(2)]
    u = [x[:, :GDN_DV] for x in sol]
    wq = [jnp.concatenate([x[:, GDN_DV:], qq * e], axis=0).astype(BF16)
          for x, qq, e in zip(sol, q, exp_g)]
    yield u, wq, attn, kdt, cdec


def _gdn_recur(u_ref, wq_ref, at_ref, kdt_ref, cd_ref, s_ref, gz_ref, nw_ref, o_ref, pair, j):
    c = GDN_CHUNK
    s = GDN_PAIR * pair + j
    chains = [j * GDN_HEADS + h for h in range(GDN_HEADS)]
    state = [s_ref[h] for h in range(GDN_HEADS)]
    ws_qs = [jnp.dot(wq_ref[ch], st.astype(BF16), preferred_element_type=F32)
             for ch, st in zip(chains, state)]
    yield None
    v_new = [(u_ref[ch] - x[:c]).astype(BF16) for ch, x in zip(chains, ws_qs)]
    zeros_v = jnp.zeros((c, GDN_DV), BF16)
    o = []
    for p in range(GDN_HEADS // 2):
        vn = jnp.concatenate([jnp.concatenate([v_new[2 * p], zeros_v], axis=1),
                              jnp.concatenate([zeros_v, v_new[2 * p + 1]], axis=1)], axis=0)
        o_p = jnp.dot(at_ref[j * (GDN_HEADS // 2) + p], vn, preferred_element_type=F32)
        o.append(ws_qs[2 * p][c:] + o_p[:, :GDN_DV])
        o.append(ws_qs[2 * p + 1][c:] + o_p[:, GDN_DV:])
    for h, ch in enumerate(chains):
        s_ref[h] = (state[h] * cd_ref[ch][:1]
                    + jnp.dot(kdt_ref[ch], v_new[h], preferred_element_type=F32))
    rows = _time_rows(s, c)
    for h in range(GDN_HEADS):
        gate = _chunk_pair(gz_ref, (0, h), pair)[j]
        o_ref[0, rows, h * GDN_DV:(h + 1) * GDN_DV] = (
            _rmsnorm(o[h], nw_ref[...]) * gate).astype(o_ref.dtype)
    yield None


def _spread_lanes(cols, onehot):
    heads = cols.shape[1] // SPLIT
    replica = lax.broadcasted_iota(jnp.int32, cols.shape, 1) // heads
    hi, mid, lo = _split(cols)
    terms = jnp.where(replica == 0, hi, jnp.where(replica == 1, mid, lo))
    return jnp.dot(terms, onehot, preferred_element_type=F32)


def _ssd_chunk(xbc_ref, sz_ref, dcol_ref, acol_ref, drow_ref, arow_ref, dskip_ref, nw_ref,
               e64_ref, e128_ref, o_ref, h_ref, n):
    c = SSM_CHUNK
    sub_per_chunk = c // GDN_CHUNK
    x_slabs = SSM_DIM // LANES
    ri = lax.broadcasted_iota(jnp.int32, (c, c), 0)
    ci = lax.broadcasted_iota(jnp.int32, (c, c), 1)
    causal = ri >= ci
    pair_w = 2 * SSM_HEADDIM
    low_half = lax.broadcasted_iota(jnp.int32, (c, pair_w), 1) < SSM_HEADDIM
    high_half = jnp.logical_not(low_half)
    gd = SSM_GROUP_DIM
    pairs_per_group = SSM_HPG // 2
    groups = range(SSM_GROUPS)
    rs = _time_rows(n, c)
    s0 = n * sub_per_chunk

    def slab(ref, idx):
        return jnp.concatenate(_chunk_pair(ref, (0, idx), n), axis=0)

    dt_col = dcol_ref[0, rs, :]
    acs_col = acol_ref[0, rs, :]
    dt_row = drow_ref[0, n]
    acs_row = arow_ref[0, n]
    a_last = acs_col[c - 1:c, :]
    coef = jnp.exp(a_last - acs_col) * dt_col
    chunk_decay = jnp.broadcast_to(jnp.exp(a_last), (SUBLANES, a_last.shape[1]))
    acs_wide = _spread_lanes(acs_col, e128_ref[...])
    wide = _spread_lanes(jnp.concatenate([coef, jnp.exp(acs_col), chunk_decay], axis=0),
                         e64_ref[...])
    coef_wide, exp_a_wide, cd_wide = wide[:c], wide[c:2 * c], wide[2 * c:2 * c + 1]
    yield None
    bg = [slab(xbc_ref, x_slabs + g) for g in groups]
    cg = [slab(xbc_ref, x_slabs + SSM_GROUPS + g) for g in groups]
    cb = [_mm_nt(cg[g], bg[g]) for g in groups]
    h_prev = [h_ref[g] for g in groups]
    y_off = [_mm(cg[g], h_prev[g]) * exp_a_wide[:, g * gd:(g + 1) * gd] for g in groups]
    yield None
    x_pairs = [[slab(xbc_ref, g * pairs_per_group + jp) for jp in range(pairs_per_group)]
               for g in groups]
    x_g = [jnp.concatenate(x_pairs[g], axis=1) for g in groups]
    for g in groups:
        gs = slice(g * gd, (g + 1) * gd)
        h_ref[g] = h_prev[g] * cd_wide[:, gs] + _mm_tn(bg[g], x_g[g] * coef_wide[:, gs])
    yield None
    y_pairs = [[], []]
    for g in groups:
        for jp in range(pairs_per_group):
            hd = g * SSM_HPG + 2 * jp
            acc = y_off[g][:, jp * pair_w:(jp + 1) * pair_w]
            for half, mask in enumerate((low_half, high_half)):
                ac = acs_wide[:, (hd + half) * LANES:(hd + half + 1) * LANES]
                ar = acs_row[hd + half:hd + half + 1, :]
                seg = jnp.exp(jnp.where(causal, ac - ar, -jnp.inf))
                scores = seg * (cb[g] * dt_row[hd + half:hd + half + 1, :])
                acc = acc + _mm(scores, jnp.where(mask, x_pairs[g][jp], 0.0))
            y_pairs[g].append(acc)
            if jp % 2 == 1:
                yield None
    for g in groups:
        gs = slice(g * gd, (g + 1) * gd)
        y = jnp.concatenate(y_pairs[g], axis=1) + x_g[g] * dskip_ref[:, gs]
        gate = jnp.concatenate(
            [slab(sz_ref, g * pairs_per_group + jp) for jp in range(pairs_per_group)], axis=1)
        yg = y * gate
        yg = yg * lax.rsqrt(jnp.mean(yg * yg, axis=-1, keepdims=True) + EPS)
        o_ref[0, rs, gs] = (yg * nw_ref[:, gs]).astype(o_ref.dtype)
    yield None


def _interleave(*gens):
    last = [None] * len(gens)
    live = list(range(len(gens)))
    while live:
        for i in list(live):
            try:
                last[i] = next(gens[i])
            except StopIteration:
                live.remove(i)
    return last


def _spaced(gen, gap):
    for value in gen:
        yield value
        for _ in range(gap):
            yield value


def _mixers_kernel(qkv_ref, gz_ref, gcol_ref, grow_ref, gnw_ref,
                   xbc_ref, sz_ref, dcol_ref, acol_ref, drow_ref, arow_ref, dskip_ref, snw_ref,
                   e64_ref, e128_ref, oa_ref, ob_ref,
                   s_ref, u_ref, wq_ref, at_ref, kdt_ref, cd_ref, h_ref):
    npairs = ROW_TILE // (GDN_PAIR * GDN_CHUNK)
    assert npairs == ROW_TILE // SSM_CHUNK

    @pl.when(pl.program_id(1) == 0)
    def _():
        s_ref[...] = jnp.zeros_like(s_ref)
        h_ref[...] = jnp.zeros_like(h_ref)

    def store_prepared(prepared):
        u, wq, attn, kdt, cdec = prepared
        for ch in range(GDN_CHAINS):
            u_ref[ch] = u[ch]
            wq_ref[ch] = wq[ch]
            kdt_ref[ch] = kdt[ch]
            cd_ref[ch] = cdec[ch]
        for pc in range(GDN_CHAINS // 2):
            at_ref[pc] = attn[pc]

    def recur(pair):
        for j in range(GDN_PAIR):
            yield from _gdn_recur(u_ref, wq_ref, at_ref, kdt_ref, cd_ref, s_ref, gz_ref, gnw_ref,
                                  oa_ref, pair, j)

    def ssd(n):
        return _ssd_chunk(xbc_ref, sz_ref, dcol_ref, acol_ref, drow_ref, arow_ref, dskip_ref,
                          snw_ref, e64_ref, e128_ref, ob_ref, h_ref, n)

    prepared, _ = _interleave(_gdn_prepare(qkv_ref, gcol_ref, grow_ref, 0), ssd(0))
    store_prepared(prepared)

    def body(pair, carry):
        prepared, _, _ = _interleave(_gdn_prepare(qkv_ref, gcol_ref, grow_ref, pair + 1),
                                     _spaced(recur(pair), 1), ssd(pair + 1))
        store_prepared(prepared)
        return carry

    lax.fori_loop(0, npairs - 1, body, 0)

    _interleave(recur(npairs - 1))


def _mixers(qkv, gz, gcol, grow, gdn_norm_w, xbc, sz, dcol, acol, drow, arow, d_skip, ssm_norm_w):
    bsz, seq, _ = gcol.shape
    rows = ROW_TILE
    nh2 = 2 * GDN_HEADS
    nh = SSM_HEADS
    c = GDN_CHUNK
    grow_c = grow[:, GDN_HEADS:].reshape(bsz, GDN_HEADS, seq // c, c).transpose(0, 2, 1, 3)
    grow_c = grow_c.reshape(bsz, seq // c, GDN_HEADS // 2, 2 * c)
    per_chunk = lambda a: a.reshape(bsz, nh, seq // SSM_CHUNK, SSM_CHUNK).transpose(0, 2, 1, 3)
    dskip = jnp.repeat(d_skip.astype(F32), SSM_HEADDIM).reshape(1, SSM_DIM)
    eye = jnp.tile(jnp.eye(nh, dtype=BF16), (SPLIT, 1))
    e64 = jnp.repeat(eye, SSM_HEADDIM, axis=1)
    e128 = jnp.repeat(eye, LANES, axis=1)

    def slab_spec(cols):
        return pl.BlockSpec((1, cols // LANES, rows // 2, LANES), lambda b, t: (b, 0, t, 0))

    def row_spec(cols):
        return pl.BlockSpec((1, rows, cols), lambda b, t: (b, t, 0))

    def per_chunk_spec(nrows, chunk):
        return pl.BlockSpec((1, rows // chunk, nrows, chunk), lambda b, t: (b, t, 0, 0))

    out_shape = jax.ShapeDtypeStruct((bsz, seq, D_MODEL), BF16)
    return pl.pallas_call(
        _mixers_kernel,
        grid=(bsz, seq // rows),
        in_specs=[
            slab_spec(3 * GDN_DIM), slab_spec(GDN_DIM), row_spec(nh2),
            pl.BlockSpec((1, rows // c, GDN_HEADS // 2, 2 * c), lambda b, t: (b, t, 0, 0)),
            _const_spec((1, GDN_DV)),
            slab_spec(SSM_XBC), slab_spec(SSM_DIM), row_spec(SPLIT * nh), row_spec(SPLIT * nh),
            per_chunk_spec(nh, SSM_CHUNK), per_chunk_spec(nh, SSM_CHUNK),
            _const_spec((1, SSM_DIM)), _const_spec((1, SSM_DIM)),
            _const_spec((SPLIT * nh, SSM_DIM)), _const_spec((SPLIT * nh, nh * LANES)),
        ],
        out_specs=[row_spec(GDN_DIM), row_spec(SSM_DIM)],
        out_shape=[out_shape, out_shape],
        scratch_shapes=[
            pltpu.VMEM((GDN_HEADS, GDN_DK, GDN_DV), F32),
            pltpu.VMEM((GDN_CHAINS, c, GDN_DV), F32),
            pltpu.VMEM((GDN_CHAINS, 2 * c, GDN_DK), BF16),
            pltpu.VMEM((GDN_CHAINS // 2, c, 2 * c), BF16),
            pltpu.VMEM((GDN_CHAINS, GDN_DK, c), BF16),
            pltpu.VMEM((GDN_CHAINS, SUBLANES, GDN_DV), F32),
            pltpu.VMEM((SSM_GROUPS, SSM_STATE, SSM_GROUP_DIM), F32),
        ],
        compiler_params=pltpu.CompilerParams(
            dimension_semantics=("arbitrary", "arbitrary"), vmem_limit_bytes=VMEM_LIMIT_BYTES),
        name="mixers",
    )(qkv, gz, gcol, grow_c, gdn_norm_w.reshape(1, GDN_DV).astype(F32),
      xbc, sz, dcol, acol, per_chunk(drow), per_chunk(arow), dskip,
      ssm_norm_w.reshape(1, SSM_DIM).astype(F32), e64, e128)


def _mem_kv_kernel(mem_ref, nw_ref, wk_ref, wv_ref, k_out, v_out):
    m = _rmsnorm(mem_ref[0], nw_ref[...]).astype(BF16)
    k_out[0] = jnp.dot(m, wk_ref[...], preferred_element_type=F32).astype(k_out.dtype)
    v_out[0] = jnp.dot(m, wv_ref[...], preferred_element_type=F32).astype(v_out.dtype)


def _mem_kv(mem, mem_norm_w, wk, wv):
    bsz, mlen, _ = mem.shape
    kv_shape = jax.ShapeDtypeStruct((bsz, mlen, D_MODEL), BF16)
    return pl.pallas_call(
        _mem_kv_kernel,
        grid=(bsz,),
        in_specs=[pl.BlockSpec((1, mlen, D_MODEL), lambda b: (b, 0, 0)),
                  _const_spec((1, D_MODEL)),
                  _const_spec((D_MODEL, D_MODEL)),
                  _const_spec((D_MODEL, D_MODEL))],
        out_specs=[pl.BlockSpec((1, mlen, D_MODEL), lambda b: (b, 0, 0))] * 2,
        out_shape=[kv_shape, kv_shape],
        compiler_params=pltpu.CompilerParams(
            dimension_semantics=("arbitrary",), vmem_limit_bytes=VMEM_LIMIT_BYTES),
        name="mem_kv",
    )(mem, mem_norm_w.reshape(1, D_MODEL), wk.astype(BF16), wv.astype(BF16))


def _tail_kernel(x_ref, oa_ref, ob_ref, k_ref, v_ref, wout_ref, n2_ref, wq_ref, wo_ref, n3_ref,
                 wup_ref, wdown_ref, nf_ref, y_ref):
    x = x_ref[0]
    x = x + jnp.dot(oa_ref[0], wout_ref[:GDN_DIM, :], preferred_element_type=F32)
    x = x + jnp.dot(ob_ref[0], wout_ref[GDN_DIM:, :], preferred_element_type=F32)

    h = _rmsnorm(x, n2_ref[...]).astype(BF16)
    q = jnp.dot(h, wq_ref[...], preferred_element_type=F32)
    scale = MEM_HEADDIM ** -0.5
    heads = []
    for hd in range(MEM_HEADS):
        hs = slice(hd * MEM_HEADDIM, (hd + 1) * MEM_HEADDIM)
        s = _mm_nt(q[:, hs], k_ref[0, :, hs]) * scale
        s = s - jnp.max(s, axis=-1, keepdims=True)
        p = jnp.exp(s)
        p = p / jnp.sum(p, axis=-1, keepdims=True)
        heads.append(_mm(p, v_ref[0, :, hs]))
    attn = jnp.concatenate(heads, axis=1)
    x = x + _mm(attn, wo_ref[...])

    h = _rmsnorm(x, n3_ref[...]).astype(BF16)
    acc = x
    for c0 in range(0, D_FF, FF_GROUP):
        u = jnp.maximum(jnp.dot(h, wup_ref[:, c0:c0 + FF_GROUP], preferred_element_type=F32), 0.0)
        acc = acc + _mm(u * u, wdown_ref[c0:c0 + FF_GROUP, :])
    y_ref[0] = _rmsnorm(acc, nf_ref[...])


def _tail(x, o_a, o_b, k_mem, v_mem, w_out, norm2_w, wq, wo, norm3_w, w_up, w_down, final_w):
    bsz, seq, _ = x.shape
    rows = min(TAIL_ROWS, seq)
    mlen = k_mem.shape[1]
    row_spec = pl.BlockSpec((1, rows, D_MODEL), lambda b, t: (b, t, 0))
    mem_spec = pl.BlockSpec((1, mlen, D_MODEL), lambda b, t: (b, 0, 0))
    vec = lambda w: w.reshape(1, D_MODEL).astype(F32)
    return pl.pallas_call(
        _tail_kernel,
        grid=(bsz, seq // rows),
        in_specs=[row_spec, row_spec, row_spec, mem_spec, mem_spec,
                  _const_spec((GDN_DIM + SSM_DIM, D_MODEL)), _const_spec((1, D_MODEL)),
                  _const_spec((D_MODEL, D_MODEL)), _const_spec((D_MODEL, D_MODEL)),
                  _const_spec((1, D_MODEL)), _const_spec((D_MODEL, D_FF)),
                  _const_spec((D_FF, D_MODEL)), _const_spec((1, D_MODEL))],
        out_specs=row_spec,
        out_shape=jax.ShapeDtypeStruct((bsz, seq, D_MODEL), x.dtype),
        compiler_params=pltpu.CompilerParams(
            dimension_semantics=("arbitrary", "arbitrary"), vmem_limit_bytes=VMEM_LIMIT_BYTES),
        name="tail",
    )(x, o_a, o_b, k_mem, v_mem, w_out.astype(BF16), vec(norm2_w), wq.astype(BF16),
      wo.astype(BF16), vec(norm3_w), w_up.astype(BF16), w_down.astype(BF16), vec(final_w))


def kernel(x, mem, norm1_w, w_in, gdn_conv_w, gdn_a_log, gdn_dt_bias, gdn_norm_w, ssm_conv_w,
           ssm_conv_b, ssm_a_log, ssm_dt_bias, ssm_d, ssm_norm_w, w_out, norm2_w, mem_norm_w,
           wq_mem, wk_mem, wv_mem, wo_mem, norm3_w, w_up, w_down, final_norm_w):
    qkv, gz, gcol, grow, sz, xbc, dcol, drow, acol, arow = _in_proj(
        x, norm1_w, w_in, gdn_conv_w, gdn_a_log, gdn_dt_bias, ssm_conv_w, ssm_conv_b, ssm_dt_bias,
        ssm_a_log)
    o_a, o_b = _mixers(qkv, gz, gcol, grow, gdn_norm_w, xbc, sz, dcol, acol, drow, arow, ssm_d,
                       ssm_norm_w)
    k_mem, v_mem = _mem_kv(mem, mem_norm_w, wk_mem, wv_mem)
    return _tail(x, o_a, o_b, k_mem, v_mem, w_out, norm2_w, wq_mem, wo_mem, norm3_w, w_up, w_down,
                 final_norm_w)
```

```python
import jax
import jax.numpy as jnp
from jax import lax
from jax.experimental import pallas as pl
from jax.experimental.pallas import tpu as pltpu

D_MODEL = 1024
EPS = 1e-6
CONV_K = 4
GDN_HEADS = 8
GDN_DK = 128
GDN_DV = 128
GDN_DIM = GDN_HEADS * GDN_DV
GDN_CHUNK = 64
SSM_DIM = D_MODEL
SSM_HEADDIM = 64
SSM_HEADS = SSM_DIM // SSM_HEADDIM
SSM_GROUPS = 2
SSM_HPG = SSM_HEADS // SSM_GROUPS
SSM_STATE = 128
SSM_CHUNK = 128
SSM_BC = SSM_GROUPS * SSM_STATE
SSM_XBC = SSM_DIM + 2 * SSM_BC
SSM_GROUP_DIM = SSM_DIM // SSM_GROUPS
MEM_HEADS = 4
MEM_HEADDIM = D_MODEL // MEM_HEADS
D_FF = 4 * D_MODEL

F32 = jnp.float32
BF16 = jnp.bfloat16

V7X_VMEM_BYTES = 64 * 1024 * 1024
VMEM_LIMIT_BYTES = V7X_VMEM_BYTES - 8 * 1024 * 1024
SUBLANES = 8
LANES = 128

ROW_TILE = 512
ROW_GROUPS = ROW_TILE // SUBLANES
assert ROW_GROUPS == GDN_CHUNK and SSM_CHUNK == 2 * GDN_CHUNK
TAIL_ROWS = 512
COL_GROUP = 512
FF_GROUP = 1024
HALO = (CONV_K - 1) * SUBLANES


def _mm(a, b):
    return jnp.dot(a.astype(BF16), b.astype(BF16), preferred_element_type=F32)


def _mm_nt(a, b):
    return lax.dot_general(a.astype(BF16), b.astype(BF16), (((1,), (1,)), ((), ())),
                           preferred_element_type=F32)


def _mm_tn(a, b):
    return lax.dot_general(a.astype(BF16), b.astype(BF16), (((0,), (0,)), ((), ())),
                           preferred_element_type=F32)


SPLIT = 3


def _split(x):
    hi = x.astype(BF16)
    rest = x - hi.astype(F32)
    mid = rest.astype(BF16)
    lo = (rest - mid.astype(F32)).astype(BF16)
    return hi, mid, lo


def _tri_dot_left(tri_wide, x):
    return jnp.dot(tri_wide, jnp.concatenate(_split(x), axis=0), preferred_element_type=F32)


def _tri_dot_right(x, tri_tall):
    return jnp.dot(jnp.concatenate(_split(x), axis=1), tri_tall, preferred_element_type=F32)


def _rmsnorm(x, w):
    return x * lax.rsqrt(jnp.mean(x * x, axis=-1, keepdims=True) + EPS) * w


def _sigmoid(x):
    return 1.0 / (1.0 + jnp.exp(-x))


def _silu(x):
    half = 0.5 * x
    return half * (1.0 + jnp.tanh(half))


def _softplus(x):
    return jnp.maximum(x, 0.0) + jnp.log(1.0 + jnp.exp(-jnp.abs(x)))


def _const_spec(shape):
    zeros = (0,) * len(shape)
    return pl.BlockSpec(shape, lambda *_: zeros, pipeline_mode=pl.Buffered(1))


def _time_rows(chunk, size):
    start = chunk * size
    return pl.ds(start if isinstance(start, int) else pl.multiple_of(start, size), size)


def _row_pair_words(y):
    return pltpu.bitcast(y.astype(BF16), jnp.int32)


def _chunk_pair(ref, lead, pair):
    words = ref[lead + (pl.ds(pair, ROW_GROUPS, stride=SUBLANES // 2), slice(None))]
    even = pltpu.bitcast(lax.shift_left(words, 16), F32)
    odd = pltpu.bitcast(lax.bitwise_and(words, jnp.int32(-65536)), F32)
    return even, odd


def _conv_silu(p, halo, w, bias):
    rows = p.shape[0]
    sub0 = lax.broadcasted_iota(jnp.int32, (SUBLANES, p.shape[1]), 0) == 0
    wrapped = []
    for j in range(CONV_K - 1):
        cur = p[rows - HALO + j * SUBLANES:rows - HALO + (j + 1) * SUBLANES]
        prev = halo[j * SUBLANES:(j + 1) * SUBLANES]
        wrapped.append(jnp.where(sub0, pltpu.roll(prev, 1, 0), pltpu.roll(cur, 1, 0)))
    acc = p * w[CONV_K - 1:CONV_K, :]
    if bias is not None:
        acc = acc + bias
    for k in range(1, CONV_K):
        shifted = jnp.concatenate(wrapped[CONV_K - 1 - k:] + [p[:rows - k * SUBLANES]], axis=0)
        acc = acc + shifted * w[CONV_K - 1 - k:CONV_K - k, :]
    return _silu(acc)


def _in_proj_kernel(x_ref, n1_ref, perm_ref, w_ref, ws_raw_ref, wc_ref, wr_ref,
                    gconv_ref, sconv_ref, sconvb_ref,
                    galog_c_ref, gbias_c_ref, galog_r_ref, gbias_r_ref, dbias_c_ref, dbias_r_ref,
                    ssm_a_c_ref, ssm_a_r_ref,
                    qkv_out, gz_out, gcol_out, grow_out, sz_out, xbc_out, dcol_out, drow_out,
                    acol_out, arow_out, ghalo_ref, shalo_ref, ws_ref):
    rows = x_ref.shape[1]

    @pl.when(pl.program_id(1) == 0)
    def _():
        ghalo_ref[...] = jnp.zeros_like(ghalo_ref)
        shalo_ref[...] = jnp.zeros_like(shalo_ref)

    @pl.when(jnp.logical_and(pl.program_id(0) == 0, pl.program_id(1) == 0))
    def _():
        words = pltpu.bitcast(ws_raw_ref[...], jnp.int32)
        lo = 2 * GDN_HEADS
        ws_ref[...] = pltpu.bitcast(words[:, lo:lo + SSM_DIM + SSM_XBC], BF16)

    h = _rmsnorm(x_ref[0], n1_ref[...]).astype(BF16)
    hp = jnp.dot(perm_ref[...], h, preferred_element_type=F32).astype(BF16)

    off_gz, off_sz, off_xbc = 3 * GDN_DIM, 0, SSM_DIM
    q_scale = GDN_DK ** -0.5
    for c0 in range(0, 3 * GDN_DIM, COL_GROUP):
        cs = slice(c0, c0 + COL_GROUP)
        p = jnp.dot(hp, w_ref[:, cs], preferred_element_type=F32)
        y = _conv_silu(p, ghalo_ref[:, cs], gconv_ref[:, cs], None)
        ghalo_ref[:, cs] = p[rows - HALO:]
        for j in range(COL_GROUP // LANES):
            yh = y[:, j * LANES:(j + 1) * LANES]
            if c0 < 2 * GDN_DIM:
                yh = yh * lax.rsqrt(jnp.sum(yh * yh, axis=-1, keepdims=True) + EPS)
                if c0 < GDN_DIM:
                    yh = yh * q_scale
            qkv_out[0, c0 // LANES + j] = _row_pair_words(yh)

    for c0 in range(0, GDN_DIM, COL_GROUP):
        gz = _silu(jnp.dot(hp, w_ref[:, off_gz + c0:off_gz + c0 + COL_GROUP],
                           preferred_element_type=F32))
        sz = _silu(jnp.dot(hp, ws_ref[:, off_sz + c0:off_sz + c0 + COL_GROUP],
                           preferred_element_type=F32))
        for j in range(COL_GROUP // LANES):
            gz_out[0, c0 // LANES + j] = _row_pair_words(gz[:, j * LANES:(j + 1) * LANES])
            sz_out[0, c0 // LANES + j] = _row_pair_words(sz[:, j * LANES:(j + 1) * LANES])

    for c0 in range(0, SSM_XBC, COL_GROUP):
        cs = slice(c0, c0 + COL_GROUP)
        p = jnp.dot(hp, ws_ref[:, off_xbc + c0:off_xbc + c0 + COL_GROUP],
                    preferred_element_type=F32)
        y = _conv_silu(p, shalo_ref[:, cs], sconv_ref[:, cs], sconvb_ref[:, cs])
        shalo_ref[:, cs] = p[rows - HALO:]
        for j in range(COL_GROUP // LANES):
            xbc_out[0, c0 // LANES + j] = _row_pair_words(y[:, j * LANES:(j + 1) * LANES])

    blk = SSM_CHUNK
    ri = lax.broadcasted_iota(jnp.int32, (blk, blk), 0)
    ci = lax.broadcasted_iota(jnp.int32, (blk, blk), 1)
    same_gdn_chunk = (ri // GDN_CHUNK) == (ci // GDN_CHUNK)
    wide = lambda m: jnp.concatenate([m.astype(BF16)] * SPLIT, axis=1)
    tall = lambda m: jnp.concatenate([m.astype(BF16)] * SPLIT, axis=0)
    tril_ssd = wide((ri >= ci).astype(F32))
    triu_ssd = tall((ri <= ci).astype(F32))
    tril_gdn = wide(jnp.logical_and(ri >= ci, same_gdn_chunk).astype(F32))
    triu_gdn = tall(jnp.logical_and(ri <= ci, same_gdn_chunk).astype(F32))

    n_dt = SPLIT * SSM_HEADS
    p_cols = jnp.dot(h, wc_ref[...], preferred_element_type=F32)
    p_rows = lax.dot_general(wr_ref[...], h, (((1,), (1,)), ((), ())),
                             preferred_element_type=F32)
    pg = p_cols[:, n_dt:]
    gates = jnp.where(lax.broadcasted_iota(jnp.int32, pg.shape, 1) < GDN_HEADS, _sigmoid(pg),
                      -jnp.exp(galog_c_ref[...]) * _softplus(pg + gbias_c_ref[...]))
    pgt = p_rows[:2 * GDN_HEADS]
    gates_t = jnp.where(lax.broadcasted_iota(jnp.int32, pgt.shape, 0) < GDN_HEADS, _sigmoid(pgt),
                        -jnp.exp(galog_r_ref[...]) * _softplus(pgt + gbias_r_ref[...]))
    dt = _softplus(p_cols[:, :n_dt] + dbias_c_ref[...])
    dt_t = _softplus(p_rows[2 * GDN_HEADS:] + dbias_r_ref[...])
    dcol_out[0] = dt
    drow_out[0] = dt_t
    a_col = dt * ssm_a_c_ref[...]
    a_row = dt_t * ssm_a_r_ref[...]
    lane = lax.broadcasted_iota(jnp.int32, (blk, 2 * GDN_HEADS), 1)
    sub = lax.broadcasted_iota(jnp.int32, (2 * GDN_HEADS, blk), 0)
    for r0 in range(0, rows, blk):
        rs = slice(r0, r0 + blk)
        g_blk = gates[rs]
        gcol_out[0, rs, :] = jnp.where(lane < GDN_HEADS, g_blk, _tri_dot_left(tril_gdn, g_blk))
        gt_blk = gates_t[:, rs]
        grow_out[0, :, rs] = jnp.where(sub < GDN_HEADS, gt_blk, _tri_dot_right(gt_blk, triu_gdn))
        acol_out[0, rs, :] = _tri_dot_left(tril_ssd, a_col[rs])
        arow_out[0, :, rs] = _tri_dot_right(a_row[:, rs], triu_ssd)


def _in_proj(x, norm1_w, w_in, gdn_conv_w, gdn_a_log, gdn_dt_bias, ssm_conv_w, ssm_conv_b,
             ssm_dt_bias, ssm_a_log):
    bsz, seq, _ = x.shape
    rows = ROW_TILE
    assert seq % rows == 0
    nh = GDN_HEADS
    off_gates = 4 * GDN_DIM
    off_sz = off_gates + 2 * nh
    off_dt = off_sz + SSM_DIM + SSM_XBC
    assert off_dt + SSM_HEADS == w_in.shape[1]
    w_gdn = w_in[:, :off_gates].astype(BF16)
    w_ssd = w_in[:, off_gates:].astype(BF16)
    w_gates = w_in[:, off_gates:off_sz].astype(BF16)
    w_dt = w_in[:, off_dt:].astype(BF16)
    zeros8 = jnp.zeros((nh,), F32)
    galog16 = jnp.concatenate([zeros8, gdn_a_log.astype(F32)])
    gbias16 = jnp.concatenate([zeros8, gdn_dt_bias.astype(F32)])
    dbias = ssm_dt_bias.astype(F32)
    ssm_a = -jnp.exp(ssm_a_log.astype(F32))
    tile_row = jnp.arange(rows)
    src_time = ROW_GROUPS * (tile_row % SUBLANES) + tile_row // SUBLANES
    perm = (src_time[:, None] == jnp.arange(rows)[None, :]).astype(BF16)

    args = [
        x, norm1_w.reshape(1, D_MODEL), perm, w_gdn, w_ssd,
        jnp.concatenate([jnp.tile(w_dt, (1, SPLIT)), w_gates], axis=1),
        jnp.concatenate([w_gates.T, w_dt.T], axis=0),
        gdn_conv_w, ssm_conv_w, ssm_conv_b.reshape(1, SSM_XBC),
        galog16.reshape(1, 2 * nh), gbias16.reshape(1, 2 * nh),
        galog16.reshape(2 * nh, 1), gbias16.reshape(2 * nh, 1),
        jnp.tile(dbias, SPLIT).reshape(1, SPLIT * SSM_HEADS), dbias.reshape(SSM_HEADS, 1),
        jnp.tile(ssm_a, SPLIT).reshape(1, SPLIT * SSM_HEADS), ssm_a.reshape(SSM_HEADS, 1),
    ]
    in_specs = [pl.BlockSpec((1, rows, D_MODEL), lambda b, t: (b, t, 0))]
    in_specs += [_const_spec(a.shape) for a in args[1:]]

    def slab_spec(cols):
        return pl.BlockSpec((1, cols // LANES, rows // 2, LANES), lambda b, t: (b, 0, t, 0))

    def slab_shape(cols):
        return jax.ShapeDtypeStruct((bsz, cols // LANES, seq // 2, LANES), jnp.int32)

    def row_spec(cols):
        return pl.BlockSpec((1, rows, cols), lambda b, t: (b, t, 0))

    def col_major_spec(nrows):
        return pl.BlockSpec((1, nrows, rows), lambda b, t: (b, 0, t))

    out_shape = [
        slab_shape(3 * GDN_DIM),
        slab_shape(GDN_DIM),
        jax.ShapeDtypeStruct((bsz, seq, 2 * nh), F32),
        jax.ShapeDtypeStruct((bsz, 2 * nh, seq), F32),
        slab_shape(SSM_DIM),
        slab_shape(SSM_XBC),
        jax.ShapeDtypeStruct((bsz, seq, SPLIT * SSM_HEADS), F32),
        jax.ShapeDtypeStruct((bsz, SSM_HEADS, seq), F32),
        jax.ShapeDtypeStruct((bsz, seq, SPLIT * SSM_HEADS), F32),
        jax.ShapeDtypeStruct((bsz, SSM_HEADS, seq), F32),
    ]
    out_specs = [slab_spec(3 * GDN_DIM), slab_spec(GDN_DIM), row_spec(2 * nh),
                 col_major_spec(2 * nh), slab_spec(SSM_DIM), slab_spec(SSM_XBC),
                 row_spec(SPLIT * SSM_HEADS), col_major_spec(SSM_HEADS),
                 row_spec(SPLIT * SSM_HEADS), col_major_spec(SSM_HEADS)]
    return pl.pallas_call(
        _in_proj_kernel,
        grid=(bsz, seq // rows),
        in_specs=in_specs,
        out_specs=out_specs,
        out_shape=out_shape,
        scratch_shapes=[pltpu.VMEM((HALO, 3 * GDN_DIM), F32),
                        pltpu.VMEM((HALO, SSM_XBC), F32),
                        pltpu.VMEM((D_MODEL, SSM_DIM + SSM_XBC), BF16)],
        compiler_params=pltpu.CompilerParams(
            dimension_semantics=("arbitrary", "arbitrary"), vmem_limit_bytes=VMEM_LIMIT_BYTES),
        name="in_proj",
    )(*args)


GDN_PAIR = 2
GDN_CHAINS = GDN_PAIR * GDN_HEADS


def _gdn_prepare(qkv_ref, gcol_ref, grow_ref, pair):
    c = GDN_CHUNK
    ri = lax.broadcasted_iota(jnp.int32, (c, 2 * c), 0)
    lane = lax.broadcasted_iota(jnp.int32, (c, 2 * c), 1)
    ci = lane % c
    low = lane < c
    causal = ri >= ci
    strict = ri > ci
    same16 = (ri // 16) == (ci // 16)
    same32 = (ri // 32) == (ci // 32)
    only32 = jnp.logical_and(same32, jnp.logical_not(same16))
    eye = (ri == ci).astype(F32)
    zeros_k = jnp.zeros((c, GDN_DK), BF16)
    zeros_uw = jnp.zeros((c, GDN_DV + GDN_DK), BF16)

    def blockdiag(x_p):
        return jnp.concatenate([jnp.where(low, x_p, 0.0), jnp.where(low, 0.0, x_p)],
                               axis=0).astype(BF16)

    def mm_packed(l_p, x_p):
        return jnp.dot(l_p.astype(BF16), blockdiag(x_p), preferred_element_type=F32)

    q, k, kbs, rhs, exp_g, kdt, cdec = [], [], [], [], [], [], []
    q_pair = [_chunk_pair(qkv_ref, (0, h), pair) for h in range(GDN_HEADS)]
    k_pair = [_chunk_pair(qkv_ref, (0, GDN_HEADS + h), pair) for h in range(GDN_HEADS)]
    v_pair = [_chunk_pair(qkv_ref, (0, 2 * GDN_HEADS + h), pair) for h in range(GDN_HEADS)]
    gram_lhs, gram_rhs, decay = [], [], []
    for j in range(GDN_PAIR):
        s = GDN_PAIR * pair + j
        gcol = gcol_ref[0, _time_rows(s, c), :]
        grow = grow_ref[0, s]
        gc_wide = []
        for h in range(GDN_HEADS):
            q.append(q_pair[h][j])
            k.append(k_pair[h][j])
            vv = v_pair[h][j]
            beta = jnp.broadcast_to(gcol[:, h:h + 1], (c, GDN_DK))
            gcw = jnp.broadcast_to(gcol[:, GDN_HEADS + h:GDN_HEADS + h + 1], (c, GDN_DK))
            g_last = gcw[c - 1:c, :]
            gc_wide.append(gcw)
            exp_g.append(jnp.exp(gcw))
            kbs.append(k[-1] * beta)
            rhs.append(jnp.concatenate([vv * beta, kbs[-1] * exp_g[-1]], axis=1).astype(BF16))
            kdt.append((k[-1] * jnp.exp(g_last - gcw)).T.astype(BF16))
            cdec.append(jnp.broadcast_to(jnp.exp(g_last), (SUBLANES, GDN_DV)))
            if h % 2 == 1:
                a, b = len(q) - 2, len(q) - 1
                gram_lhs.append(jnp.concatenate(
                    [jnp.concatenate([kbs[a], kbs[b]], axis=1),
                     jnp.concatenate([q[a], q[b]], axis=1)], axis=0).astype(BF16))
                gram_rhs.append(jnp.concatenate(
                    [jnp.concatenate([k[a].astype(BF16), zeros_k], axis=1),
                     jnp.concatenate([zeros_k, k[b].astype(BF16)], axis=1)], axis=0))
                gc_p = jnp.where(low, gc_wide[h - 1], gc_wide[h])
                gr_p = grow[h // 2:h // 2 + 1, :]
                decay.append(jnp.exp(jnp.where(causal, gc_p - gr_p, -jnp.inf)))
    gram = [lax.dot_general(l, r, (((1,), (1,)), ((), ())), preferred_element_type=F32)
            for l, r in zip(gram_lhs, gram_rhs)]
    yield None
    a_strict = [jnp.where(strict, g[:c] * d, 0.0) for g, d in zip(gram, decay)]
    attn = [(g[c:] * d).astype(BF16) for g, d in zip(gram, decay)]
    d16 = [jnp.where(same16, a, 0.0) for a in a_strict]
    inv = [eye - x for x in d16]
    xp = [mm_packed(x, x) for x in d16]
    yield None
    for step in range(3):
        inv = [i + mm_packed(i, x) for i, x in zip(inv, xp)]
        if step < 2:
            xp = [mm_packed(x, x) for x in xp]
        yield None
    for mask in (only32, jnp.logical_not(same32)):
        tmp = [mm_packed(i, jnp.where(mask, a, 0.0)) for i, a in zip(inv, a_strict)]
        yield None
        inv = [i - mm_packed(t, i) for i, t in zip(inv, tmp)]
        yield None
    sol = [jnp.dot(t.astype(BF16),
                   jnp.concatenate([jnp.concatenate([rhs[2 * i], zeros_uw], axis=1),
                                    jnp.concatenate([zeros_uw, rhs[2 * i + 1]], axis=1)], axis=0),
                   preferred_element_type=F32) for i, t in enumerate(inv)]
    width = GDN_DV + GDN_DK
    sol = [x[:, half * width:(half + 1) * width] for x in sol for half in range(2)]
    u = [x[:, :GDN_DV] for x in sol]
    wq = [jnp.concatenate([x[:, GDN_DV:], qq * e], axis=0).astype(BF16)
          for x, qq, e in zip(sol, q, exp_g)]
    yield u, wq, attn, kdt, cdec


def _gdn_recur(u_ref, wq_ref, at_ref, kdt_ref, cd_ref, s_ref, gz_ref, nw_ref, o_ref, pair, j):
    c = GDN_CHUNK
    s = GDN_PAIR * pair + j
    chains = [j * GDN_HEADS + h for h in range(GDN_HEADS)]
    state = [s_ref[h] for h in range(GDN_HEADS)]
    ws_qs = [jnp.dot(wq_ref[ch], st.astype(BF16), preferred_element_type=F32)
             for ch, st in zip(chains, state)]
    yield None
    v_new = [(u_ref[ch] - x[:c]).astype(BF16) for ch, x in zip(chains, ws_qs)]
    zeros_v = jnp.zeros((c, GDN_DV), BF16)
    o = []
    for p in range(GDN_HEADS // 2):
        vn = jnp.concatenate([jnp.concatenate([v_new[2 * p], zeros_v], axis=1),
                              jnp.concatenate([zeros_v, v_new[2 * p + 1]], axis=1)], axis=0)
        o_p = jnp.dot(at_ref[j * (GDN_HEADS // 2) + p], vn, preferred_element_type=F32)
        o.append(ws_qs[2 * p][c:] + o_p[:, :GDN_DV])
        o.append(ws_qs[2 * p + 1][c:] + o_p[:, GDN_DV:])
    for h, ch in enumerate(chains):
        s_ref[h] = (state[h] * cd_ref[ch][:1]
                    + jnp.dot(kdt_ref[ch], v_new[h], preferred_element_type=F32))
    rows = _time_rows(s, c)
    for h in range(GDN_HEADS):
        gate = _chunk_pair(gz_ref, (0, h), pair)[j]
        o_ref[0, rows, h * GDN_DV:(h + 1) * GDN_DV] = (
            _rmsnorm(o[h], nw_ref[...]) * gate).astype(o_ref.dtype)
    yield None


def _spread_lanes(cols, onehot):
    heads = cols.shape[1] // SPLIT
    replica = lax.broadcasted_iota(jnp.int32, cols.shape, 1) // heads
    hi, mid, lo = _split(cols)
    terms = jnp.where(replica == 0, hi, jnp.where(replica == 1, mid, lo))
    return jnp.dot(terms, onehot, preferred_element_type=F32)


def _ssd_chunk(xbc_ref, sz_ref, dcol_ref, acol_ref, drow_ref, arow_ref, dskip_ref, nw_ref,
               e64_ref, e128_ref, o_ref, h_ref, n):
    c = SSM_CHUNK
    sub_per_chunk = c // GDN_CHUNK
    x_slabs = SSM_DIM // LANES
    ri = lax.broadcasted_iota(jnp.int32, (c, c), 0)
    ci = lax.broadcasted_iota(jnp.int32, (c, c), 1)
    causal = ri >= ci
    pair_w = 2 * SSM_HEADDIM
    low_half = lax.broadcasted_iota(jnp.int32, (c, pair_w), 1) < SSM_HEADDIM
    high_half = jnp.logical_not(low_half)
    gd = SSM_GROUP_DIM
    pairs_per_group = SSM_HPG // 2
    groups = range(SSM_GROUPS)
    rs = _time_rows(n, c)
    s0 = n * sub_per_chunk

    def slab(ref, idx):
        return jnp.concatenate(_chunk_pair(ref, (0, idx), n), axis=0)

    dt_col = dcol_ref[0, rs, :]
    acs_col = acol_ref[0, rs, :]
    dt_row = drow_ref[0, n]
    acs_row = arow_ref[0, n]
    a_last = acs_col[c - 1:c, :]
    coef = jnp.exp(a_last - acs_col) * dt_col
    chunk_decay = jnp.broadcast_to(jnp.exp(a_last), (SUBLANES, a_last.shape[1]))
    acs_wide = _spread_lanes(acs_col, e128_ref[...])
    wide = _spread_lanes(jnp.concatenate([coef, jnp.exp(acs_col), chunk_decay], axis=0),
                         e64_ref[...])
    coef_wide, exp_a_wide, cd_wide = wide[:c], wide[c:2 * c], wide[2 * c:2 * c + 1]
    yield None
    bg = [slab(xbc_ref, x_slabs + g) for g in groups]
    cg = [slab(xbc_ref, x_slabs + SSM_GROUPS + g) for g in groups]
    cb = [_mm_nt(cg[g], bg[g]) for g in groups]
    h_prev = [h_ref[g] for g in groups]
    y_off = [_mm(cg[g], h_prev[g]) * exp_a_wide[:, g * gd:(g + 1) * gd] for g in groups]
    yield None
    x_pairs = [[slab(xbc_ref, g * pairs_per_group + jp) for jp in range(pairs_per_group)]
               for g in groups]
    x_g = [jnp.concatenate(x_pairs[g], axis=1) for g in groups]
    for g in groups:
        gs = slice(g * gd, (g + 1) * gd)
        h_ref[g] = h_prev[g] * cd_wide[:, gs] + _mm_tn(bg[g], x_g[g] * coef_wide[:, gs])
    yield None
    y_pairs = [[], []]
    for g in groups:
        for jp in range(pairs_per_group):
            hd = g * SSM_HPG + 2 * jp
            acc = y_off[g][:, jp * pair_w:(jp + 1) * pair_w]
            for half, mask in enumerate((low_half, high_half)):
                ac = acs_wide[:, (hd + half) * LANES:(hd + half + 1) * LANES]
                ar = acs_row[hd + half:hd + half + 1, :]
                seg = jnp.exp(jnp.where(causal, ac - ar, -jnp.inf))
                scores = seg * (cb[g] * dt_row[hd + half:hd + half + 1, :])
                acc = acc + _mm(scores, jnp.where(mask, x_pairs[g][jp], 0.0))
            y_pairs[g].append(acc)
            if jp % 2 == 1:
                yield None
    for g in groups:
        gs = slice(g * gd, (g + 1) * gd)
        y = jnp.concatenate(y_pairs[g], axis=1) + x_g[g] * dskip_ref[:, gs]
        gate = jnp.concatenate(
            [slab(sz_ref, g * pairs_per_group + jp) for jp in range(pairs_per_group)], axis=1)
        yg = y * gate
        yg = yg * lax.rsqrt(jnp.mean(yg * yg, axis=-1, keepdims=True) + EPS)
        o_ref[0, rs, gs] = (yg * nw_ref[:, gs]).astype(o_ref.dtype)
    yield None


def _interleave(*gens):
    last = [None] * len(gens)
    live = list(range(len(gens)))
    while live:
        for i in list(live):
            try:
                last[i] = next(gens[i])
            except StopIteration:
                live.remove(i)
    return last


def _spaced(gen, gap):
    for value in gen:
        yield value
        for _ in range(gap):
            yield value


def _mixers_kernel(qkv_ref, gz_ref, gcol_ref, grow_ref, gnw_ref,
                   xbc_ref, sz_ref, dcol_ref, acol_ref, drow_ref, arow_ref, dskip_ref, snw_ref,
                   e64_ref, e128_ref, oa_ref, ob_ref,
                   s_ref, u_ref, wq_ref, at_ref, kdt_ref, cd_ref, h_ref):
    npairs = ROW_TILE // (GDN_PAIR * GDN_CHUNK)
    assert npairs == ROW_TILE // SSM_CHUNK

    @pl.when(pl.program_id(1) == 0)
    def _():
        s_ref[...] = jnp.zeros_like(s_ref)
        h_ref[...] = jnp.zeros_like(h_ref)

    def store_prepared(prepared):
        u, wq, attn, kdt, cdec = prepared
        for ch in range(GDN_CHAINS):
            u_ref[ch] = u[ch]
            wq_ref[ch] = wq[ch]
            kdt_ref[ch] = kdt[ch]
            cd_ref[ch] = cdec[ch]
        for pc in range(GDN_CHAINS // 2):
            at_ref[pc] = attn[pc]

    def recur(pair):
        for j in range(GDN_PAIR):
            yield from _gdn_recur(u_ref, wq_ref, at_ref, kdt_ref, cd_ref, s_ref, gz_ref, gnw_ref,
                                  oa_ref, pair, j)

    def ssd(n):
        return _ssd_chunk(xbc_ref, sz_ref, dcol_ref, acol_ref, drow_ref, arow_ref, dskip_ref,
                          snw_ref, e64_ref, e128_ref, ob_ref, h_ref, n)

    prepared, _ = _interleave(_gdn_prepare(qkv_ref, gcol_ref, grow_ref, 0), ssd(0))
    store_prepared(prepared)

    def body(pair, carry):
        prepared, _, _ = _interleave(_gdn_prepare(qkv_ref, gcol_ref, grow_ref, pair + 1),
                                     _spaced(recur(pair), 1), ssd(pair + 1))
        store_prepared(prepared)
        return carry

    lax.fori_loop(0, npairs - 1, body, 0)

    _interleave(recur(npairs - 1))


def _mixers(qkv, gz, gcol, grow, gdn_norm_w, xbc, sz, dcol, acol, drow, arow, d_skip, ssm_norm_w):
    bsz, seq, _ = gcol.shape
    rows = ROW_TILE
    nh2 = 2 * GDN_HEADS
    nh = SSM_HEADS
    c = GDN_CHUNK
    grow_c = grow[:, GDN_HEADS:].reshape(bsz, GDN_HEADS, seq // c, c).transpose(0, 2, 1, 3)
    grow_c = grow_c.reshape(bsz, seq // c, GDN_HEADS // 2, 2 * c)
    per_chunk = lambda a: a.reshape(bsz, nh, seq // SSM_CHUNK, SSM_CHUNK).transpose(0, 2, 1, 3)
    dskip = jnp.repeat(d_skip.astype(F32), SSM_HEADDIM).reshape(1, SSM_DIM)
    eye = jnp.tile(jnp.eye(nh, dtype=BF16), (SPLIT, 1))
    e64 = jnp.repeat(eye, SSM_HEADDIM, axis=1)
    e128 = jnp.repeat(eye, LANES, axis=1)

    def slab_spec(cols):
        return pl.BlockSpec((1, cols // LANES, rows // 2, LANES), lambda b, t: (b, 0, t, 0))

    def row_spec(cols):
        return pl.BlockSpec((1, rows, cols), lambda b, t: (b, t, 0))

    def per_chunk_spec(nrows, chunk):
        return pl.BlockSpec((1, rows // chunk, nrows, chunk), lambda b, t: (b, t, 0, 0))

    out_shape = jax.ShapeDtypeStruct((bsz, seq, D_MODEL), BF16)
    return pl.pallas_call(
        _mixers_kernel,
        grid=(bsz, seq // rows),
        in_specs=[
            slab_spec(3 * GDN_DIM), slab_spec(GDN_DIM), row_spec(nh2),
            pl.BlockSpec((1, rows // c, GDN_HEADS // 2, 2 * c), lambda b, t: (b, t, 0, 0)),
            _const_spec((1, GDN_DV)),
            slab_spec(SSM_XBC), slab_spec(SSM_DIM), row_spec(SPLIT * nh), row_spec(SPLIT * nh),
            per_chunk_spec(nh, SSM_CHUNK), per_chunk_spec(nh, SSM_CHUNK),
            _const_spec((1, SSM_DIM)), _const_spec((1, SSM_DIM)),
            _const_spec((SPLIT * nh, SSM_DIM)), _const_spec((SPLIT * nh, nh * LANES)),
        ],
        out_specs=[row_spec(GDN_DIM), row_spec(SSM_DIM)],
        out_shape=[out_shape, out_shape],
        scratch_shapes=[
            pltpu.VMEM((GDN_HEADS, GDN_DK, GDN_DV), F32),
            pltpu.VMEM((GDN_CHAINS, c, GDN_DV), F32),
            pltpu.VMEM((GDN_CHAINS, 2 * c, GDN_DK), BF16),
            pltpu.VMEM((GDN_CHAINS // 2, c, 2 * c), BF16),
            pltpu.VMEM((GDN_CHAINS, GDN_DK, c), BF16),
            pltpu.VMEM((GDN_CHAINS, SUBLANES, GDN_DV), F32),
            pltpu.VMEM((SSM_GROUPS, SSM_STATE, SSM_GROUP_DIM), F32),
        ],
        compiler_params=pltpu.CompilerParams(
            dimension_semantics=("arbitrary", "arbitrary"), vmem_limit_bytes=VMEM_LIMIT_BYTES),
        name="mixers",
    )(qkv, gz, gcol, grow_c, gdn_norm_w.reshape(1, GDN_DV).astype(F32),
      xbc, sz, dcol, acol, per_chunk(drow), per_chunk(arow), dskip,
      ssm_norm_w.reshape(1, SSM_DIM).astype(F32), e64, e128)


def _mem_kv_kernel(mem_ref, nw_ref, wk_ref, wv_ref, k_out, v_out):
    m = _rmsnorm(mem_ref[0], nw_ref[...]).astype(BF16)
    k_out[0] = jnp.dot(m, wk_ref[...], preferred_element_type=F32).astype(k_out.dtype)
    v_out[0] = jnp.dot(m, wv_ref[...], preferred_element_type=F32).astype(v_out.dtype)


def _mem_kv(mem, mem_norm_w, wk, wv):
    bsz, mlen, _ = mem.shape
    kv_shape = jax.ShapeDtypeStruct((bsz, mlen, D_MODEL), BF16)
    return pl.pallas_call(
        _mem_kv_kernel,
        grid=(bsz,),
        in_specs=[pl.BlockSpec((1, mlen, D_MODEL), lambda b: (b, 0, 0)),
                  _const_spec((1, D_MODEL)),
                  _const_spec((D_MODEL, D_MODEL)),
                  _const_spec((D_MODEL, D_MODEL))],
        out_specs=[pl.BlockSpec((1, mlen, D_MODEL), lambda b: (b, 0, 0))] * 2,
        out_shape=[kv_shape, kv_shape],
        compiler_params=pltpu.CompilerParams(
            dimension_semantics=("arbitrary",), vmem_limit_bytes=VMEM_LIMIT_BYTES),
        name="mem_kv",
    )(mem, mem_norm_w.reshape(1, D_MODEL), wk.astype(BF16), wv.astype(BF16))


def _tail_kernel(x_ref, oa_ref, ob_ref, k_ref, v_ref, wout_ref, n2_ref, wq_ref, wo_ref, n3_ref,
                 wup_ref, wdown_ref, nf_ref, y_ref):
    x = x_ref[0]
    x = x + jnp.dot(oa_ref[0], wout_ref[:GDN_DIM, :], preferred_element_type=F32)
    x = x + jnp.dot(ob_ref[0], wout_ref[GDN_DIM:, :], preferred_element_type=F32)

    h = _rmsnorm(x, n2_ref[...]).astype(BF16)
    q = jnp.dot(h, wq_ref[...], preferred_element_type=F32)
    scale = MEM_HEADDIM ** -0.5
    heads = []
    for hd in range(MEM_HEADS):
        hs = slice(hd * MEM_HEADDIM, (hd + 1) * MEM_HEADDIM)
        s = _mm_nt(q[:, hs], k_ref[0, :, hs]) * scale
        s = s - jnp.max(s, axis=-1, keepdims=True)
        p = jnp.exp(s)
        p = p / jnp.sum(p, axis=-1, keepdims=True)
        heads.append(_mm(p, v_ref[0, :, hs]))
    attn = jnp.concatenate(heads, axis=1)
    x = x + _mm(attn, wo_ref[...])

    h = _rmsnorm(x, n3_ref[...]).astype(BF16)
    acc = x
    for c0 in range(0, D_FF, FF_GROUP):
        u = jnp.maximum(jnp.dot(h, wup_ref[:, c0:c0 + FF_GROUP], preferred_element_type=F32), 0.0)
        acc = acc + _mm(u * u, wdown_ref[c0:c0 + FF_GROUP, :])
    y_ref[0] = _rmsnorm(acc, nf_ref[...])


def _tail(x, o_a, o_b, k_mem, v_mem, w_out, norm2_w, wq, wo, norm3_w, w_up, w_down, final_w):
    bsz, seq, _ = x.shape
    rows = min(TAIL_ROWS, seq)
    mlen = k_mem.shape[1]
    row_spec = pl.BlockSpec((1, rows, D_MODEL), lambda b, t: (b, t, 0))
    mem_spec = pl.BlockSpec((1, mlen, D_MODEL), lambda b, t: (b, 0, 0))
    vec = lambda w: w.reshape(1, D_MODEL).astype(F32)
    return pl.pallas_call(
        _tail_kernel,
        grid=(bsz, seq // rows),
        in_specs=[row_spec, row_spec, row_spec, mem_spec, mem_spec,
                  _const_spec((GDN_DIM + SSM_DIM, D_MODEL)), _const_spec((1, D_MODEL)),
                  _const_spec((D_MODEL, D_MODEL)), _const_spec((D_MODEL, D_MODEL)),
                  _const_spec((1, D_MODEL)), _const_spec((D_MODEL, D_FF)),
                  _const_spec((D_FF, D_MODEL)), _const_spec((1, D_MODEL))],
        out_specs=row_spec,
        out_shape=jax.ShapeDtypeStruct((bsz, seq, D_MODEL), x.dtype),
        compiler_params=pltpu.CompilerParams(
            dimension_semantics=("arbitrary", "arbitrary"), vmem_limit_bytes=VMEM_LIMIT_BYTES),
        name="tail",
    )(x, o_a, o_b, k_mem, v_mem, w_out.astype(BF16), vec(norm2_w), wq.astype(BF16),
      wo.astype(BF16), vec(norm3_w), w_up.astype(BF16), w_down.astype(BF16), vec(final_w))


def kernel(x, mem, norm1_w, w_in, gdn_conv_w, gdn_a_log, gdn_dt_bias, gdn_norm_w, ssm_conv_w,
           ssm_conv_b, ssm_a_log, ssm_dt_bias, ssm_d, ssm_norm_w, w_out, norm2_w, mem_norm_w,
           wq_mem, wk_mem, wv_mem, wo_mem, norm3_w, w_up, w_down, final_norm_w):
    qkv, gz, gcol, grow, sz, xbc, dcol, drow, acol, arow = _in_proj(
        x, norm1_w, w_in, gdn_conv_w, gdn_a_log, gdn_dt_bias, ssm_conv_w, ssm_conv_b, ssm_dt_bias,
        ssm_a_log)
    o_a, o_b = _mixers(qkv, gz, gcol, grow, gdn_norm_w, xbc, sz, dcol, acol, drow, arow, ssm_d,
                       ssm_norm_w)
    k_mem, v_mem = _mem_kv(mem, mem_norm_w, wk_mem, wv_mem)
    return _tail(x, o_a, o_b, k_mem, v_mem, w_out, norm2_w, wq_mem, wo_mem, norm3_w, w_up, w_down,
                 final_norm_w)
```

```python
import jax
import jax.numpy as jnp
from jax import lax
from jax.experimental import pallas as pl
from jax.experimental.pallas import tpu as pltpu

D_MODEL = 1024
EPS = 1e-6
CONV_K = 4
GDN_HEADS = 8
GDN_DK = 128
GDN_DV = 128
GDN_DIM = GDN_HEADS * GDN_DV
GDN_CHUNK = 64
SSM_DIM = D_MODEL
SSM_HEADDIM = 64
SSM_HEADS = SSM_DIM // SSM_HEADDIM
SSM_GROUPS = 2
SSM_HPG = SSM_HEADS // SSM_GROUPS
SSM_STATE = 128
SSM_CHUNK = 128
SSM_BC = SSM_GROUPS * SSM_STATE
SSM_XBC = SSM_DIM + 2 * SSM_BC
SSM_GROUP_DIM = SSM_DIM // SSM_GROUPS
MEM_HEADS = 4
MEM_HEADDIM = D_MODEL // MEM_HEADS
D_FF = 4 * D_MODEL

F32 = jnp.float32
BF16 = jnp.bfloat16

V7X_VMEM_BYTES = 64 * 1024 * 1024
VMEM_LIMIT_BYTES = V7X_VMEM_BYTES - 8 * 1024 * 1024
SUBLANES = 8
LANES = 128

ROW_TILE = 512
ROW_GROUPS = ROW_TILE // SUBLANES
assert ROW_GROUPS == GDN_CHUNK and SSM_CHUNK == 2 * GDN_CHUNK
TAIL_ROWS = 512
COL_GROUP = 512
FF_GROUP = 1024
HALO = (CONV_K - 1) * SUBLANES


def _mm(a, b):
    return jnp.dot(a.astype(BF16), b.astype(BF16), preferred_element_type=F32)


def _mm_nt(a, b):
    return lax.dot_general(a.astype(BF16), b.astype(BF16), (((1,), (1,)), ((), ())),
                           preferred_element_type=F32)


def _mm_tn(a, b):
    return lax.dot_general(a.astype(BF16), b.astype(BF16), (((0,), (0,)), ((), ())),
                           preferred_element_type=F32)


SPLIT = 3


def _split(x):
    hi = x.astype(BF16)
    rest = x - hi.astype(F32)
    mid = rest.astype(BF16)
    lo = (rest - mid.astype(F32)).astype(BF16)
    return hi, mid, lo


def _tri_dot_left(tri_wide, x):
    return jnp.dot(tri_wide, jnp.concatenate(_split(x), axis=0), preferred_element_type=F32)


def _tri_dot_right(x, tri_tall):
    return jnp.dot(jnp.concatenate(_split(x), axis=1), tri_tall, preferred_element_type=F32)


def _rmsnorm(x, w):
    return x * lax.rsqrt(jnp.mean(x * x, axis=-1, keepdims=True) + EPS) * w


def _sigmoid(x):
    return 1.0 / (1.0 + jnp.exp(-x))


def _silu(x):
    half = 0.5 * x
    return half * (1.0 + jnp.tanh(half))


def _softplus(x):
    return jnp.maximum(x, 0.0) + jnp.log(1.0 + jnp.exp(-jnp.abs(x)))


def _const_spec(shape):
    zeros = (0,) * len(shape)
    return pl.BlockSpec(shape, lambda *_: zeros, pipeline_mode=pl.Buffered(1))


def _time_rows(chunk, size):
    start = chunk * size
    return pl.ds(start if isinstance(start, int) else pl.multiple_of(start, size), size)


def _row_pair_words(y):
    return pltpu.bitcast(y.astype(BF16), jnp.int32)


def _chunk_pair(ref, lead, pair):
    words = ref[lead + (pl.ds(pair, ROW_GROUPS, stride=SUBLANES // 2), slice(None))]
    even = pltpu.bitcast(lax.shift_left(words, 16), F32)
    odd = pltpu.bitcast(lax.bitwise_and(words, jnp.int32(-65536)), F32)
    return even, odd


def _conv_silu(p, halo, w, bias):
    rows = p.shape[0]
    sub0 = lax.broadcasted_iota(jnp.int32, (SUBLANES, p.shape[1]), 0) == 0
    wrapped = []
    for j in range(CONV_K - 1):
        cur = p[rows - HALO + j * SUBLANES:rows - HALO + (j + 1) * SUBLANES]
        prev = halo[j * SUBLANES:(j + 1) * SUBLANES]
        wrapped.append(jnp.where(sub0, pltpu.roll(prev, 1, 0), pltpu.roll(cur, 1, 0)))
    acc = p * w[CONV_K - 1:CONV_K, :]
    if bias is not None:
        acc = acc + bias
    for k in range(1, CONV_K):
        shifted = jnp.concatenate(wrapped[CONV_K - 1 - k:] + [p[:rows - k * SUBLANES]], axis=0)
        acc = acc + shifted * w[CONV_K - 1 - k:CONV_K - k, :]
    return _silu(acc)


def _in_proj_kernel(x_ref, n1_ref, perm_ref, w_ref, wc_ref, wr_ref,
                    gconv_ref, sconv_ref, sconvb_ref,
                    galog_c_ref, gbias_c_ref, galog_r_ref, gbias_r_ref, dbias_c_ref, dbias_r_ref,
                    ssm_a_c_ref, ssm_a_r_ref,
                    qkv_out, gz_out, gcol_out, grow_out, sz_out, xbc_out, dcol_out, drow_out,
                    acol_out, arow_out, ghalo_ref, shalo_ref, ws_ref):
    rows = x_ref.shape[1]

    @pl.when(pl.program_id(1) == 0)
    def _():
        ghalo_ref[...] = jnp.zeros_like(ghalo_ref)
        shalo_ref[...] = jnp.zeros_like(shalo_ref)

    @pl.when(jnp.logical_and(pl.program_id(0) == 0, pl.program_id(1) == 0))
    def _():
        words = pltpu.bitcast(w_ref[:, 4 * GDN_DIM:], jnp.int32)
        lo = 2 * GDN_HEADS
        ws_ref[...] = pltpu.bitcast(words[:, lo:lo + SSM_DIM + SSM_XBC], BF16)

    h = _rmsnorm(x_ref[0], n1_ref[...]).astype(BF16)
    hp = jnp.dot(perm_ref[...], h, preferred_element_type=F32).astype(BF16)

    off_gz, off_sz, off_xbc = 3 * GDN_DIM, 0, SSM_DIM
    q_scale = GDN_DK ** -0.5
    for c0 in range(0, 3 * GDN_DIM, COL_GROUP):
        cs = slice(c0, c0 + COL_GROUP)
        p = jnp.dot(hp, w_ref[:, cs], preferred_element_type=F32)
        y = _conv_silu(p, ghalo_ref[:, cs], gconv_ref[:, cs], None)
        ghalo_ref[:, cs] = p[rows - HALO:]
        for j in range(COL_GROUP // LANES):
            yh = y[:, j * LANES:(j + 1) * LANES]
            if c0 < 2 * GDN_DIM:
                yh = yh * lax.rsqrt(jnp.sum(yh * yh, axis=-1, keepdims=True) + EPS)
                if c0 < GDN_DIM:
                    yh = yh * q_scale
            qkv_out[0, c0 // LANES + j] = _row_pair_words(yh)

    for c0 in range(0, GDN_DIM, COL_GROUP):
        gz = _silu(jnp.dot(hp, w_ref[:, off_gz + c0:off_gz + c0 + COL_GROUP],
                           preferred_element_type=F32))
        sz = _silu(jnp.dot(hp, ws_ref[:, off_sz + c0:off_sz + c0 + COL_GROUP],
                           preferred_element_type=F32))
        for j in range(COL_GROUP // LANES):
            gz_out[0, c0 // LANES + j] = _row_pair_words(gz[:, j * LANES:(j + 1) * LANES])
            sz_out[0, c0 // LANES + j] = _row_pair_words(sz[:, j * LANES:(j + 1) * LANES])

    for c0 in range(0, SSM_XBC, COL_GROUP):
        cs = slice(c0, c0 + COL_GROUP)
        p = jnp.dot(hp, ws_ref[:, off_xbc + c0:off_xbc + c0 + COL_GROUP],
                    preferred_element_type=F32)
        y = _conv_silu(p, shalo_ref[:, cs], sconv_ref[:, cs], sconvb_ref[:, cs])
        shalo_ref[:, cs] = p[rows - HALO:]
        for j in range(COL_GROUP // LANES):
            xbc_out[0, c0 // LANES + j] = _row_pair_words(y[:, j * LANES:(j + 1) * LANES])

    blk = SSM_CHUNK
    ri = lax.broadcasted_iota(jnp.int32, (blk, blk), 0)
    ci = lax.broadcasted_iota(jnp.int32, (blk, blk), 1)
    same_gdn_chunk = (ri // GDN_CHUNK) == (ci // GDN_CHUNK)
    wide = lambda m: jnp.concatenate([m.astype(BF16)] * SPLIT, axis=1)
    tall = lambda m: jnp.concatenate([m.astype(BF16)] * SPLIT, axis=0)
    tril_ssd = wide((ri >= ci).astype(F32))
    triu_ssd = tall((ri <= ci).astype(F32))
    tril_gdn = wide(jnp.logical_and(ri >= ci, same_gdn_chunk).astype(F32))
    triu_gdn = tall(jnp.logical_and(ri <= ci, same_gdn_chunk).astype(F32))

    n_dt = SPLIT * SSM_HEADS
    p_cols = jnp.dot(h, wc_ref[...], preferred_element_type=F32)
    p_rows = lax.dot_general(wr_ref[...], h, (((1,), (1,)), ((), ())),
                             preferred_element_type=F32)
    pg = p_cols[:, n_dt:]
    gates = jnp.where(lax.broadcasted_iota(jnp.int32, pg.shape, 1) < GDN_HEADS, _sigmoid(pg),
                      -jnp.exp(galog_c_ref[...]) * _softplus(pg + gbias_c_ref[...]))
    pgt = p_rows[:2 * GDN_HEADS]
    gates_t = jnp.where(lax.broadcasted_iota(jnp.int32, pgt.shape, 0) < GDN_HEADS, _sigmoid(pgt),
                        -jnp.exp(galog_r_ref[...]) * _softplus(pgt + gbias_r_ref[...]))
    dt = _softplus(p_cols[:, :n_dt] + dbias_c_ref[...])
    dt_t = _softplus(p_rows[2 * GDN_HEADS:] + dbias_r_ref[...])
    dcol_out[0] = dt
    drow_out[0] = dt_t
    a_col = dt * ssm_a_c_ref[...]
    a_row = dt_t * ssm_a_r_ref[...]
    lane = lax.broadcasted_iota(jnp.int32, (blk, 2 * GDN_HEADS), 1)
    sub = lax.broadcasted_iota(jnp.int32, (2 * GDN_HEADS, blk), 0)
    for r0 in range(0, rows, blk):
        rs = slice(r0, r0 + blk)
        g_blk = gates[rs]
        gcol_out[0, rs, :] = jnp.where(lane < GDN_HEADS, g_blk, _tri_dot_left(tril_gdn, g_blk))
        gt_blk = gates_t[:, rs]
        grow_out[0, :, rs] = jnp.where(sub < GDN_HEADS, gt_blk, _tri_dot_right(gt_blk, triu_gdn))
        acol_out[0, rs, :] = _tri_dot_left(tril_ssd, a_col[rs])
        arow_out[0, :, rs] = _tri_dot_right(a_row[:, rs], triu_ssd)


def _in_proj(x, norm1_w, w_in, gdn_conv_w, gdn_a_log, gdn_dt_bias, ssm_conv_w, ssm_conv_b,
             ssm_dt_bias, ssm_a_log):
    bsz, seq, _ = x.shape
    rows = ROW_TILE
    assert seq % rows == 0
    nh = GDN_HEADS
    off_gates = 4 * GDN_DIM
    off_sz = off_gates + 2 * nh
    off_dt = off_sz + SSM_DIM + SSM_XBC
    assert off_dt + SSM_HEADS == w_in.shape[1]
    w_all = w_in.astype(BF16)
    w_gates = w_all[:, off_gates:off_sz]
    w_dt = w_all[:, off_dt:]
    zeros8 = jnp.zeros((nh,), F32)
    galog16 = jnp.concatenate([zeros8, gdn_a_log.astype(F32)])
    gbias16 = jnp.concatenate([zeros8, gdn_dt_bias.astype(F32)])
    dbias = ssm_dt_bias.astype(F32)
    ssm_a = -jnp.exp(ssm_a_log.astype(F32))
    tile_row = jnp.arange(rows)
    src_time = ROW_GROUPS * (tile_row % SUBLANES) + tile_row // SUBLANES
    perm = (src_time[:, None] == jnp.arange(rows)[None, :]).astype(BF16)

    args = [
        x, norm1_w.reshape(1, D_MODEL), perm, w_all,
        jnp.concatenate([jnp.tile(w_dt, (1, SPLIT)), w_gates], axis=1),
        jnp.concatenate([w_gates.T, w_dt.T], axis=0),
        gdn_conv_w, ssm_conv_w, ssm_conv_b.reshape(1, SSM_XBC),
        galog16.reshape(1, 2 * nh), gbias16.reshape(1, 2 * nh),
        galog16.reshape(2 * nh, 1), gbias16.reshape(2 * nh, 1),
        jnp.tile(dbias, SPLIT).reshape(1, SPLIT * SSM_HEADS), dbias.reshape(SSM_HEADS, 1),
        jnp.tile(ssm_a, SPLIT).reshape(1, SPLIT * SSM_HEADS), ssm_a.reshape(SSM_HEADS, 1),
    ]
    in_specs = [pl.BlockSpec((1, rows, D_MODEL), lambda b, t: (b, t, 0))]
    in_specs += [_const_spec(a.shape) for a in args[1:]]

    def slab_spec(cols):
        return pl.BlockSpec((1, cols // LANES, rows // 2, LANES), lambda b, t: (b, 0, t, 0))

    def slab_shape(cols):
        return jax.ShapeDtypeStruct((bsz, cols // LANES, seq // 2, LANES), jnp.int32)

    def row_spec(cols):
        return pl.BlockSpec((1, rows, cols), lambda b, t: (b, t, 0))

    def col_major_spec(nrows):
        return pl.BlockSpec((1, nrows, rows), lambda b, t: (b, 0, t))

    out_shape = [
        slab_shape(3 * GDN_DIM),
        slab_shape(GDN_DIM),
        jax.ShapeDtypeStruct((bsz, seq, 2 * nh), F32),
        jax.ShapeDtypeStruct((bsz, 2 * nh, seq), F32),
        slab_shape(SSM_DIM),
        slab_shape(SSM_XBC),
        jax.ShapeDtypeStruct((bsz, seq, SPLIT * SSM_HEADS), F32),
        jax.ShapeDtypeStruct((bsz, SSM_HEADS, seq), F32),
        jax.ShapeDtypeStruct((bsz, seq, SPLIT * SSM_HEADS), F32),
        jax.ShapeDtypeStruct((bsz, SSM_HEADS, seq), F32),
    ]
    out_specs = [slab_spec(3 * GDN_DIM), slab_spec(GDN_DIM), row_spec(2 * nh),
                 col_major_spec(2 * nh), slab_spec(SSM_DIM), slab_spec(SSM_XBC),
                 row_spec(SPLIT * SSM_HEADS), col_major_spec(SSM_HEADS),
                 row_spec(SPLIT * SSM_HEADS), col_major_spec(SSM_HEADS)]
    return pl.pallas_call(
        _in_proj_kernel,
        grid=(bsz, seq // rows),
        in_specs=in_specs,
        out_specs=out_specs,
        out_shape=out_shape,
        scratch_shapes=[pltpu.VMEM((HALO, 3 * GDN_DIM), F32),
                        pltpu.VMEM((HALO, SSM_XBC), F32),
                        pltpu.VMEM((D_MODEL, SSM_DIM + SSM_XBC), BF16)],
        compiler_params=pltpu.CompilerParams(
            dimension_semantics=("arbitrary", "arbitrary"), vmem_limit_bytes=VMEM_LIMIT_BYTES),
        name="in_proj",
    )(*args)


GDN_PAIR = 2
GDN_CHAINS = GDN_PAIR * GDN_HEADS


def _gdn_prepare(qkv_ref, gcol_ref, grow_ref, pair):
    c = GDN_CHUNK
    ri = lax.broadcasted_iota(jnp.int32, (c, 2 * c), 0)
    lane = lax.broadcasted_iota(jnp.int32, (c, 2 * c), 1)
    ci = lane % c
    low = lane < c
    causal = ri >= ci
    strict = ri > ci
    same16 = (ri // 16) == (ci // 16)
    same32 = (ri // 32) == (ci // 32)
    only32 = jnp.logical_and(same32, jnp.logical_not(same16))
    eye = (ri == ci).astype(F32)
    zeros_k = jnp.zeros((c, GDN_DK), BF16)
    zeros_uw = jnp.zeros((c, GDN_DV + GDN_DK), BF16)

    def blockdiag(x_p):
        return jnp.concatenate([jnp.where(low, x_p, 0.0), jnp.where(low, 0.0, x_p)],
                               axis=0).astype(BF16)

    def mm_packed(l_p, x_p):
        return jnp.dot(l_p.astype(BF16), blockdiag(x_p), preferred_element_type=F32)

    q, k, kbs, rhs, exp_g, kdt, cdec = [], [], [], [], [], [], []
    q_pair = [_chunk_pair(qkv_ref, (0, h), pair) for h in range(GDN_HEADS)]
    k_pair = [_chunk_pair(qkv_ref, (0, GDN_HEADS + h), pair) for h in range(GDN_HEADS)]
    v_pair = [_chunk_pair(qkv_ref, (0, 2 * GDN_HEADS + h), pair) for h in range(GDN_HEADS)]
    gram_lhs, gram_rhs, decay = [], [], []
    for j in range(GDN_PAIR):
        s = GDN_PAIR * pair + j
        gcol = gcol_ref[0, _time_rows(s, c), :]
        grow = grow_ref[0, s]
        gc_wide = []
        for h in range(GDN_HEADS):
            q.append(q_pair[h][j])
            k.append(k_pair[h][j])
            vv = v_pair[h][j]
            beta = jnp.broadcast_to(gcol[:, h:h + 1], (c, GDN_DK))
            gcw = jnp.broadcast_to(gcol[:, GDN_HEADS + h:GDN_HEADS + h + 1], (c, GDN_DK))
            g_last = gcw[c - 1:c, :]
            gc_wide.append(gcw)
            exp_g.append(jnp.exp(gcw))
            kbs.append(k[-1] * beta)
            rhs.append(jnp.concatenate([vv * beta, kbs[-1] * exp_g[-1]], axis=1).astype(BF16))
            kdt.append((k[-1] * jnp.exp(g_last - gcw)).T.astype(BF16))
            cdec.append(jnp.broadcast_to(jnp.exp(g_last), (SUBLANES, GDN_DV)))
            if h % 2 == 1:
                a, b = len(q) - 2, len(q) - 1
                gram_lhs.append(jnp.concatenate(
                    [jnp.concatenate([kbs[a], kbs[b]], axis=1),
                     jnp.concatenate([q[a], q[b]], axis=1)], axis=0).astype(BF16))
                gram_rhs.append(jnp.concatenate(
                    [jnp.concatenate([k[a].astype(BF16), zeros_k], axis=1),
                     jnp.concatenate([zeros_k, k[b].astype(BF16)], axis=1)], axis=0))
                gc_p = jnp.where(low, gc_wide[h - 1], gc_wide[h])
                gr_p = grow[h // 2:h // 2 + 1, :]
                decay.append(jnp.exp(jnp.where(causal, gc_p - gr_p, -jnp.inf)))
    gram = [lax.dot_general(l, r, (((1,), (1,)), ((), ())), preferred_element_type=F32)
            for l, r in zip(gram_lhs, gram_rhs)]
    yield None
    a_strict = [jnp.where(strict, g[:c] * d, 0.0) for g, d in zip(gram, decay)]
    attn = [(g[c:] * d).astype(BF16) for g, d in zip(gram, decay)]
    d16 = [jnp.where(same16, a, 0.0) for a in a_strict]
    inv = [eye - x for x in d16]
    xp = [mm_packed(x, x) for x in d16]
    yield None
    for step in range(3):
        inv = [i + mm_packed(i, x) for i, x in zip(inv, xp)]
        if step < 2:
            xp = [mm_packed(x, x) for x in xp]
        yield None
    for mask in (only32, jnp.logical_not(same32)):
        tmp = [mm_packed(i, jnp.where(mask, a, 0.0)) for i, a in zip(inv, a_strict)]
        yield None
        inv = [i - mm_packed(t, i) for i, t in zip(inv, tmp)]
        yield None
    sol = [jnp.dot(t.astype(BF16),
                   jnp.concatenate([jnp.concatenate([rhs[2 * i], zeros_uw], axis=1),
                                    jnp.concatenate([zeros_uw, rhs[2 * i + 1]], axis=1)], axis=0),
                   preferred_element_type=F32) for i, t in enumerate(inv)]
    width = GDN_DV + GDN_DK
    sol = [x[:, half * width:(half + 1) * width] for x in sol for half in range(2)]
    u = [x[:, :GDN_DV] for x in sol]
    wq = [jnp.concatenate([x[:, GDN_DV:], qq * e], axis=0).astype(BF16)
          for x, qq, e in zip(sol, q, exp_g)]
    yield u, wq, attn, kdt, cdec


def _gdn_recur(u_ref, wq_ref, at_ref, kdt_ref, cd_ref, s_ref, gz_ref, nw_ref, o_ref, pair, j):
    c = GDN_CHUNK
    s = GDN_PAIR * pair + j
    chains = [j * GDN_HEADS + h for h in range(GDN_HEADS)]
    state = [s_ref[h] for h in range(GDN_HEADS)]
    ws_qs = [jnp.dot(wq_ref[ch], st.astype(BF16), preferred_element_type=F32)
             for ch, st in zip(chains, state)]
    yield None
    v_new = [(u_ref[ch] - x[:c]).astype(BF16) for ch, x in zip(chains, ws_qs)]
    zeros_v = jnp.zeros((c, GDN_DV), BF16)
    o = []
    for p in range(GDN_HEADS // 2):
        vn = jnp.concatenate([jnp.concatenate([v_new[2 * p], zeros_v], axis=1),
                              jnp.concatenate([zeros_v, v_new[2 * p + 1]], axis=1)], axis=0)
        o_p = jnp.dot(at_ref[j * (GDN_HEADS // 2) + p], vn, preferred_element_type=F32)
        o.append(ws_qs[2 * p][c:] + o_p[:, :GDN_DV])
        o.append(ws_qs[2 * p + 1][c:] + o_p[:, GDN_DV:])
    for h, ch in enumerate(chains):
        s_ref[h] = (state[h] * cd_ref[ch][:1]
                    + jnp.dot(kdt_ref[ch], v_new[h], preferred_element_type=F32))
    rows = _time_rows(s, c)
    for h in range(GDN_HEADS):
        gate = _chunk_pair(gz_ref, (0, h), pair)[j]
        o_ref[0, rows, h * GDN_DV:(h + 1) * GDN_DV] = (
            _rmsnorm(o[h], nw_ref[...]) * gate).astype(o_ref.dtype)
    yield None


def _spread_lanes(cols, onehot):
    heads = cols.shape[1] // SPLIT
    replica = lax.broadcasted_iota(jnp.int32, cols.shape, 1) // heads
    hi, mid, lo = _split(cols)
    terms = jnp.where(replica == 0, hi, jnp.where(replica == 1, mid, lo))
    return jnp.dot(terms, onehot, preferred_element_type=F32)


def _ssd_chunk(xbc_ref, sz_ref, dcol_ref, acol_ref, drow_ref, arow_ref, dskip_ref, nw_ref,
               e64_ref, e128_ref, o_ref, h_ref, n):
    c = SSM_CHUNK
    sub_per_chunk = c // GDN_CHUNK
    x_slabs = SSM_DIM // LANES
    ri = lax.broadcasted_iota(jnp.int32, (c, c), 0)
    ci = lax.broadcasted_iota(jnp.int32, (c, c), 1)
    causal = ri >= ci
    pair_w = 2 * SSM_HEADDIM
    low_half = lax.broadcasted_iota(jnp.int32, (c, pair_w), 1) < SSM_HEADDIM
    high_half = jnp.logical_not(low_half)
    gd = SSM_GROUP_DIM
    pairs_per_group = SSM_HPG // 2
    groups = range(SSM_GROUPS)
    rs = _time_rows(n, c)
    s0 = n * sub_per_chunk

    def slab(ref, idx):
        return jnp.concatenate(_chunk_pair(ref, (0, idx), n), axis=0)

    dt_col = dcol_ref[0, rs, :]
    acs_col = acol_ref[0, rs, :]
    dt_row = drow_ref[0, n]
    acs_row = arow_ref[0, n]
    a_last = acs_col[c - 1:c, :]
    coef = jnp.exp(a_last - acs_col) * dt_col
    chunk_decay = jnp.broadcast_to(jnp.exp(a_last), (SUBLANES, a_last.shape[1]))
    acs_wide = _spread_lanes(acs_col, e128_ref[...])
    wide = _spread_lanes(jnp.concatenate([coef, jnp.exp(acs_col), chunk_decay], axis=0),
                         e64_ref[...])
    coef_wide, exp_a_wide, cd_wide = wide[:c], wide[c:2 * c], wide[2 * c:2 * c + 1]
    yield None
    bg = [slab(xbc_ref, x_slabs + g) for g in groups]
    cg = [slab(xbc_ref, x_slabs + SSM_GROUPS + g) for g in groups]
    cb = [_mm_nt(cg[g], bg[g]) for g in groups]
    h_prev = [h_ref[g] for g in groups]
    y_off = [_mm(cg[g], h_prev[g]) * exp_a_wide[:, g * gd:(g + 1) * gd] for g in groups]
    yield None
    x_pairs = [[slab(xbc_ref, g * pairs_per_group + jp) for jp in range(pairs_per_group)]
               for g in groups]
    x_g = [jnp.concatenate(x_pairs[g], axis=1) for g in groups]
    for g in groups:
        gs = slice(g * gd, (g + 1) * gd)
        h_ref[g] = h_prev[g] * cd_wide[:, gs] + _mm_tn(bg[g], x_g[g] * coef_wide[:, gs])
    yield None
    y_pairs = [[], []]
    for g in groups:
        for jp in range(pairs_per_group):
            hd = g * SSM_HPG + 2 * jp
            acc = y_off[g][:, jp * pair_w:(jp + 1) * pair_w]
            for half, mask in enumerate((low_half, high_half)):
                ac = acs_wide[:, (hd + half) * LANES:(hd + half + 1) * LANES]
                ar = acs_row[hd + half:hd + half + 1, :]
                seg = jnp.exp(jnp.where(causal, ac - ar, -jnp.inf))
                scores = seg * (cb[g] * dt_row[hd + half:hd + half + 1, :])
                acc = acc + _mm(scores, jnp.where(mask, x_pairs[g][jp], 0.0))
            y_pairs[g].append(acc)
            if jp % 2 == 1:
                yield None
    for g in groups:
        gs = slice(g * gd, (g + 1) * gd)
        y = jnp.concatenate(y_pairs[g], axis=1) + x_g[g] * dskip_ref[:, gs]
        gate = jnp.concatenate(
            [slab(sz_ref, g * pairs_per_group + jp) for jp in range(pairs_per_group)], axis=1)
        yg = y * gate
        yg = yg * lax.rsqrt(jnp.mean(yg * yg, axis=-1, keepdims=True) + EPS)
        o_ref[0, rs, gs] = (yg * nw_ref[:, gs]).astype(o_ref.dtype)
    yield None


def _interleave(*gens):
    last = [None] * len(gens)
    live = list(range(len(gens)))
    while live:
        for i in list(live):
            try:
                last[i] = next(gens[i])
            except StopIteration:
                live.remove(i)
    return last


def _spaced(gen, gap):
    for value in gen:
        yield value
        for _ in range(gap):
            yield value


def _mixers_kernel(qkv_ref, gz_ref, gcol_ref, grow_ref, gnw_ref,
                   xbc_ref, sz_ref, dcol_ref, acol_ref, drow_ref, arow_ref, dskip_ref, snw_ref,
                   e64_ref, e128_ref, oa_ref, ob_ref,
                   s_ref, u_ref, wq_ref, at_ref, kdt_ref, cd_ref, h_ref):
    npairs = ROW_TILE // (GDN_PAIR * GDN_CHUNK)
    assert npairs == ROW_TILE // SSM_CHUNK

    @pl.when(pl.program_id(1) == 0)
    def _():
        s_ref[...] = jnp.zeros_like(s_ref)
        h_ref[...] = jnp.zeros_like(h_ref)

    def store_prepared(prepared):
        u, wq, attn, kdt, cdec = prepared
        for ch in range(GDN_CHAINS):
            u_ref[ch] = u[ch]
            wq_ref[ch] = wq[ch]
            kdt_ref[ch] = kdt[ch]
            cd_ref[ch] = cdec[ch]
        for pc in range(GDN_CHAINS // 2):
            at_ref[pc] = attn[pc]

    def recur(pair):
        for j in range(GDN_PAIR):
            yield from _gdn_recur(u_ref, wq_ref, at_ref, kdt_ref, cd_ref, s_ref, gz_ref, gnw_ref,
                                  oa_ref, pair, j)

    def ssd(n):
        return _ssd_chunk(xbc_ref, sz_ref, dcol_ref, acol_ref, drow_ref, arow_ref, dskip_ref,
                          snw_ref, e64_ref, e128_ref, ob_ref, h_ref, n)

    prepared, _ = _interleave(_gdn_prepare(qkv_ref, gcol_ref, grow_ref, 0), ssd(0))
    store_prepared(prepared)

    def body(pair, carry):
        prepared, _, _ = _interleave(_gdn_prepare(qkv_ref, gcol_ref, grow_ref, pair + 1),
                                     _spaced(recur(pair), 1), ssd(pair + 1))
        store_prepared(prepared)
        return carry

    lax.fori_loop(0, npairs - 1, body, 0)

    _interleave(recur(npairs - 1))


def _mixers(qkv, gz, gcol, grow, gdn_norm_w, xbc, sz, dcol, acol, drow, arow, d_skip, ssm_norm_w):
    bsz, seq, _ = gcol.shape
    rows = ROW_TILE
    nh2 = 2 * GDN_HEADS
    nh = SSM_HEADS
    c = GDN_CHUNK
    grow_c = grow[:, GDN_HEADS:].reshape(bsz, GDN_HEADS, seq // c, c).transpose(0, 2, 1, 3)
    grow_c = grow_c.reshape(bsz, seq // c, GDN_HEADS // 2, 2 * c)
    per_chunk = lambda a: a.reshape(bsz, nh, seq // SSM_CHUNK, SSM_CHUNK).transpose(0, 2, 1, 3)
    dskip = jnp.repeat(d_skip.astype(F32), SSM_HEADDIM).reshape(1, SSM_DIM)
    eye = jnp.tile(jnp.eye(nh, dtype=BF16), (SPLIT, 1))
    e64 = jnp.repeat(eye, SSM_HEADDIM, axis=1)
    e128 = jnp.repeat(eye, LANES, axis=1)

    def slab_spec(cols):
        return pl.BlockSpec((1, cols // LANES, rows // 2, LANES), lambda b, t: (b, 0, t, 0))

    def row_spec(cols):
        return pl.BlockSpec((1, rows, cols), lambda b, t: (b, t, 0))

    def per_chunk_spec(nrows, chunk):
        return pl.BlockSpec((1, rows // chunk, nrows, chunk), lambda b, t: (b, t, 0, 0))

    out_shape = jax.ShapeDtypeStruct((bsz, seq, D_MODEL), BF16)
    return pl.pallas_call(
        _mixers_kernel,
        grid=(bsz, seq // rows),
        in_specs=[
            slab_spec(3 * GDN_DIM), slab_spec(GDN_DIM), row_spec(nh2),
            pl.BlockSpec((1, rows // c, GDN_HEADS // 2, 2 * c), lambda b, t: (b, t, 0, 0)),
            _const_spec((1, GDN_DV)),
            slab_spec(SSM_XBC), slab_spec(SSM_DIM), row_spec(SPLIT * nh), row_spec(SPLIT * nh),
            per_chunk_spec(nh, SSM_CHUNK), per_chunk_spec(nh, SSM_CHUNK),
            _const_spec((1, SSM_DIM)), _const_spec((1, SSM_DIM)),
            _const_spec((SPLIT * nh, SSM_DIM)), _const_spec((SPLIT * nh, nh * LANES)),
        ],
        out_specs=[row_spec(GDN_DIM), row_spec(SSM_DIM)],
        out_shape=[out_shape, out_shape],
        scratch_shapes=[
            pltpu.VMEM((GDN_HEADS, GDN_DK, GDN_DV), F32),
            pltpu.VMEM((GDN_CHAINS, c, GDN_DV), F32),
            pltpu.VMEM((GDN_CHAINS, 2 * c, GDN_DK), BF16),
            pltpu.VMEM((GDN_CHAINS // 2, c, 2 * c), BF16),
            pltpu.VMEM((GDN_CHAINS, GDN_DK, c), BF16),
            pltpu.VMEM((GDN_CHAINS, SUBLANES, GDN_DV), F32),
            pltpu.VMEM((SSM_GROUPS, SSM_STATE, SSM_GROUP_DIM), F32),
        ],
        compiler_params=pltpu.CompilerParams(
            dimension_semantics=("arbitrary", "arbitrary"), vmem_limit_bytes=VMEM_LIMIT_BYTES),
        name="mixers",
    )(qkv, gz, gcol, grow_c, gdn_norm_w.reshape(1, GDN_DV).astype(F32),
      xbc, sz, dcol, acol, per_chunk(drow), per_chunk(arow), dskip,
      ssm_norm_w.reshape(1, SSM_DIM).astype(F32), e64, e128)


def _mem_kv_kernel(mem_ref, nw_ref, wk_ref, wv_ref, k_out, v_out):
    m = _rmsnorm(mem_ref[0], nw_ref[...]).astype(BF16)
    k_out[0] = jnp.dot(m, wk_ref[...], preferred_element_type=F32).astype(k_out.dtype)
    v_out[0] = jnp.dot(m, wv_ref[...], preferred_element_type=F32).astype(v_out.dtype)


def _mem_kv(mem, mem_norm_w, wk, wv):
    bsz, mlen, _ = mem.shape
    kv_shape = jax.ShapeDtypeStruct((bsz, mlen, D_MODEL), BF16)
    return pl.pallas_call(
        _mem_kv_kernel,
        grid=(bsz,),
        in_specs=[pl.BlockSpec((1, mlen, D_MODEL), lambda b: (b, 0, 0)),
                  _const_spec((1, D_MODEL)),
                  _const_spec((D_MODEL, D_MODEL)),
                  _const_spec((D_MODEL, D_MODEL))],
        out_specs=[pl.BlockSpec((1, mlen, D_MODEL), lambda b: (b, 0, 0))] * 2,
        out_shape=[kv_shape, kv_shape],
        compiler_params=pltpu.CompilerParams(
            dimension_semantics=("arbitrary",), vmem_limit_bytes=VMEM_LIMIT_BYTES),
        name="mem_kv",
    )(mem, mem_norm_w.reshape(1, D_MODEL), wk.astype(BF16), wv.astype(BF16))


def _tail_kernel(x_ref, oa_ref, ob_ref, k_ref, v_ref, wout_ref, n2_ref, wq_ref, wo_ref, n3_ref,
                 wup_ref, wdown_ref, nf_ref, y_ref):
    x = x_ref[0]
    x = x + jnp.dot(oa_ref[0], wout_ref[:GDN_DIM, :], preferred_element_type=F32)
    x = x + jnp.dot(ob_ref[0], wout_ref[GDN_DIM:, :], preferred_element_type=F32)

    h = _rmsnorm(x, n2_ref[...]).astype(BF16)
    q = jnp.dot(h, wq_ref[...], preferred_element_type=F32)
    scale = MEM_HEADDIM ** -0.5
    heads = []
    for hd in range(MEM_HEADS):
        hs = slice(hd * MEM_HEADDIM, (hd + 1) * MEM_HEADDIM)
        s = _mm_nt(q[:, hs], k_ref[0, :, hs]) * scale
        s = s - jnp.max(s, axis=-1, keepdims=True)
        p = jnp.exp(s)
        p = p / jnp.sum(p, axis=-1, keepdims=True)
        heads.append(_mm(p, v_ref[0, :, hs]))
    attn = jnp.concatenate(heads, axis=1)
    x = x + _mm(attn, wo_ref[...])

    h = _rmsnorm(x, n3_ref[...]).astype(BF16)
    acc = x
    for c0 in range(0, D_FF, FF_GROUP):
        u = jnp.maximum(jnp.dot(h, wup_ref[:, c0:c0 + FF_GROUP], preferred_element_type=F32), 0.0)
        acc = acc + _mm(u * u, wdown_ref[c0:c0 + FF_GROUP, :])
    y_ref[0] = _rmsnorm(acc, nf_ref[...])


def _tail(x, o_a, o_b, k_mem, v_mem, w_out, norm2_w, wq, wo, norm3_w, w_up, w_down, final_w):
    bsz, seq, _ = x.shape
    rows = min(TAIL_ROWS, seq)
    mlen = k_mem.shape[1]
    row_spec = pl.BlockSpec((1, rows, D_MODEL), lambda b, t: (b, t, 0))
    mem_spec = pl.BlockSpec((1, mlen, D_MODEL), lambda b, t: (b, 0, 0))
    vec = lambda w: w.reshape(1, D_MODEL).astype(F32)
    return pl.pallas_call(
        _tail_kernel,
        grid=(bsz, seq // rows),
        in_specs=[row_spec, row_spec, row_spec, mem_spec, mem_spec,
                  _const_spec((GDN_DIM + SSM_DIM, D_MODEL)), _const_spec((1, D_MODEL)),
                  _const_spec((D_MODEL, D_MODEL)), _const_spec((D_MODEL, D_MODEL)),
                  _const_spec((1, D_MODEL)), _const_spec((D_MODEL, D_FF)),
                  _const_spec((D_FF, D_MODEL)), _const_spec((1, D_MODEL))],
        out_specs=row_spec,
        out_shape=jax.ShapeDtypeStruct((bsz, seq, D_MODEL), x.dtype),
        compiler_params=pltpu.CompilerParams(
            dimension_semantics=("arbitrary", "arbitrary"), vmem_limit_bytes=VMEM_LIMIT_BYTES),
        name="tail",
    )(x, o_a, o_b, k_mem, v_mem, w_out.astype(BF16), vec(norm2_w), wq.astype(BF16),
      wo.astype(BF16), vec(norm3_w), w_up.astype(BF16), w_down.astype(BF16), vec(final_w))


def kernel(x, mem, norm1_w, w_in, gdn_conv_w, gdn_a_log, gdn_dt_bias, gdn_norm_w, ssm_conv_w,
           ssm_conv_b, ssm_a_log, ssm_dt_bias, ssm_d, ssm_norm_w, w_out, norm2_w, mem_norm_w,
           wq_mem, wk_mem, wv_mem, wo_mem, norm3_w, w_up, w_down, final_norm_w):
    qkv, gz, gcol, grow, sz, xbc, dcol, drow, acol, arow = _in_proj(
        x, norm1_w, w_in, gdn_conv_w, gdn_a_log, gdn_dt_bias, ssm_conv_w, ssm_conv_b, ssm_dt_bias,
        ssm_a_log)
    o_a, o_b = _mixers(qkv, gz, gcol, grow, gdn_norm_w, xbc, sz, dcol, acol, drow, arow, ssm_d,
                       ssm_norm_w)
    k_mem, v_mem = _mem_kv(mem, mem_norm_w, wk_mem, wv_mem)
    return _tail(x, o_a, o_b, k_mem, v_mem, w_out, norm2_w, wq_mem, wo_mem, norm3_w, w_up, w_down,
                 final_norm_w)
```

```python
import jax
import jax.numpy as jnp
from jax import lax
from jax.experimental import pallas as pl
from jax.experimental.pallas import tpu as pltpu

D_MODEL = 1024
EPS = 1e-6
CONV_K = 4
GDN_HEADS = 8
GDN_DK = 128
GDN_DV = 128
GDN_DIM = GDN_HEADS * GDN_DV
GDN_CHUNK = 64
SSM_DIM = D_MODEL
SSM_HEADDIM = 64
SSM_HEADS = SSM_DIM // SSM_HEADDIM
SSM_GROUPS = 2
SSM_HPG = SSM_HEADS // SSM_GROUPS
SSM_STATE = 128
SSM_CHUNK = 128
SSM_BC = SSM_GROUPS * SSM_STATE
SSM_XBC = SSM_DIM + 2 * SSM_BC
SSM_GROUP_DIM = SSM_DIM // SSM_GROUPS
MEM_HEADS = 4
MEM_HEADDIM = D_MODEL // MEM_HEADS
D_FF = 4 * D_MODEL

F32 = jnp.float32
BF16 = jnp.bfloat16

V7X_VMEM_BYTES = 64 * 1024 * 1024
VMEM_LIMIT_BYTES = V7X_VMEM_BYTES - 8 * 1024 * 1024
SUBLANES = 8
LANES = 128

ROW_TILE = 512
ROW_GROUPS = ROW_TILE // SUBLANES
assert ROW_GROUPS == GDN_CHUNK and SSM_CHUNK == 2 * GDN_CHUNK
TAIL_ROWS = 512
COL_GROUP = 512
FF_GROUP = 1024
HALO = (CONV_K - 1) * SUBLANES


def _mm(a, b):
    return jnp.dot(a.astype(BF16), b.astype(BF16), preferred_element_type=F32)


def _mm_nt(a, b):
    return lax.dot_general(a.astype(BF16), b.astype(BF16), (((1,), (1,)), ((), ())),
                           preferred_element_type=F32)


def _mm_tn(a, b):
    return lax.dot_general(a.astype(BF16), b.astype(BF16), (((0,), (0,)), ((), ())),
                           preferred_element_type=F32)


SPLIT = 3


def _split(x):
    hi = x.astype(BF16)
    rest = x - hi.astype(F32)
    mid = rest.astype(BF16)
    lo = (rest - mid.astype(F32)).astype(BF16)
    return hi, mid, lo


def _tri_dot_left(tri_wide, x):
    return jnp.dot(tri_wide, jnp.concatenate(_split(x), axis=0), preferred_element_type=F32)


def _tri_dot_right(x, tri_tall):
    return jnp.dot(jnp.concatenate(_split(x), axis=1), tri_tall, preferred_element_type=F32)


def _rmsnorm(x, w):
    return x * lax.rsqrt(jnp.mean(x * x, axis=-1, keepdims=True) + EPS) * w


def _sigmoid(x):
    return 1.0 / (1.0 + jnp.exp(-x))


def _silu(x):
    half = 0.5 * x
    return half * (1.0 + jnp.tanh(half))


def _softplus(x):
    return jnp.maximum(x, 0.0) + jnp.log(1.0 + jnp.exp(-jnp.abs(x)))


def _const_spec(shape):
    zeros = (0,) * len(shape)
    return pl.BlockSpec(shape, lambda *_: zeros, pipeline_mode=pl.Buffered(1))


def _time_rows(chunk, size):
    start = chunk * size
    return pl.ds(start if isinstance(start, int) else pl.multiple_of(start, size), size)


def _row_pair_words(y):
    return pltpu.bitcast(y.astype(BF16), jnp.int32)


def _chunk_pair(ref, lead, pair):
    words = ref[lead + (pl.ds(pair, ROW_GROUPS, stride=SUBLANES // 2), slice(None))]
    even = pltpu.bitcast(lax.shift_left(words, 16), F32)
    odd = pltpu.bitcast(lax.bitwise_and(words, jnp.int32(-65536)), F32)
    return even, odd


def _conv_silu(p, halo, w, bias):
    rows = p.shape[0]
    sub0 = lax.broadcasted_iota(jnp.int32, (SUBLANES, p.shape[1]), 0) == 0
    wrapped = []
    for j in range(CONV_K - 1):
        cur = p[rows - HALO + j * SUBLANES:rows - HALO + (j + 1) * SUBLANES]
        prev = halo[j * SUBLANES:(j + 1) * SUBLANES]
        wrapped.append(jnp.where(sub0, pltpu.roll(prev, 1, 0), pltpu.roll(cur, 1, 0)))
    acc = p * w[CONV_K - 1:CONV_K, :]
    if bias is not None:
        acc = acc + bias
    for k in range(1, CONV_K):
        shifted = jnp.concatenate(wrapped[CONV_K - 1 - k:] + [p[:rows - k * SUBLANES]], axis=0)
        acc = acc + shifted * w[CONV_K - 1 - k:CONV_K - k, :]
    return _silu(acc)


def _in_proj_kernel(x_ref, n1_ref, perm_ref, w_ref, wc_ref,
                    gconv_ref, sconv_ref, sconvb_ref,
                    galog_c_ref, gbias_c_ref, galog_r_ref, gbias_r_ref, dbias_c_ref, dbias_r_ref,
                    ssm_a_c_ref, ssm_a_r_ref,
                    qkv_out, gz_out, gcol_out, grow_out, sz_out, xbc_out, dcol_out, drow_out,
                    acol_out, arow_out, ghalo_ref, shalo_ref, ws_ref):
    rows = x_ref.shape[1]

    @pl.when(pl.program_id(1) == 0)
    def _():
        ghalo_ref[...] = jnp.zeros_like(ghalo_ref)
        shalo_ref[...] = jnp.zeros_like(shalo_ref)

    @pl.when(jnp.logical_and(pl.program_id(0) == 0, pl.program_id(1) == 0))
    def _():
        words = pltpu.bitcast(w_ref[:, 4 * GDN_DIM:], jnp.int32)
        lo = 2 * GDN_HEADS
        ws_ref[...] = pltpu.bitcast(words[:, lo:lo + SSM_DIM + SSM_XBC], BF16)

    h = _rmsnorm(x_ref[0], n1_ref[...]).astype(BF16)
    hp = jnp.dot(perm_ref[...], h, preferred_element_type=F32).astype(BF16)

    off_gz, off_sz, off_xbc = 3 * GDN_DIM, 0, SSM_DIM
    q_scale = GDN_DK ** -0.5
    for c0 in range(0, 3 * GDN_DIM, COL_GROUP):
        cs = slice(c0, c0 + COL_GROUP)
        p = jnp.dot(hp, w_ref[:, cs], preferred_element_type=F32)
        y = _conv_silu(p, ghalo_ref[:, cs], gconv_ref[:, cs], None)
        ghalo_ref[:, cs] = p[rows - HALO:]
        for j in range(COL_GROUP // LANES):
            yh = y[:, j * LANES:(j + 1) * LANES]
            if c0 < 2 * GDN_DIM:
                yh = yh * lax.rsqrt(jnp.sum(yh * yh, axis=-1, keepdims=True) + EPS)
                if c0 < GDN_DIM:
                    yh = yh * q_scale
            qkv_out[0, c0 // LANES + j] = _row_pair_words(yh)

    for c0 in range(0, GDN_DIM, COL_GROUP):
        gz = _silu(jnp.dot(hp, w_ref[:, off_gz + c0:off_gz + c0 + COL_GROUP],
                           preferred_element_type=F32))
        sz = _silu(jnp.dot(hp, ws_ref[:, off_sz + c0:off_sz + c0 + COL_GROUP],
                           preferred_element_type=F32))
        for j in range(COL_GROUP // LANES):
            gz_out[0, c0 // LANES + j] = _row_pair_words(gz[:, j * LANES:(j + 1) * LANES])
            sz_out[0, c0 // LANES + j] = _row_pair_words(sz[:, j * LANES:(j + 1) * LANES])

    for c0 in range(0, SSM_XBC, COL_GROUP):
        cs = slice(c0, c0 + COL_GROUP)
        p = jnp.dot(hp, ws_ref[:, off_xbc + c0:off_xbc + c0 + COL_GROUP],
                    preferred_element_type=F32)
        y = _conv_silu(p, shalo_ref[:, cs], sconv_ref[:, cs], sconvb_ref[:, cs])
        shalo_ref[:, cs] = p[rows - HALO:]
        for j in range(COL_GROUP // LANES):
            xbc_out[0, c0 // LANES + j] = _row_pair_words(y[:, j * LANES:(j + 1) * LANES])

    blk = SSM_CHUNK
    ri = lax.broadcasted_iota(jnp.int32, (blk, blk), 0)
    ci = lax.broadcasted_iota(jnp.int32, (blk, blk), 1)
    same_gdn_chunk = (ri // GDN_CHUNK) == (ci // GDN_CHUNK)
    wide = lambda m: jnp.concatenate([m.astype(BF16)] * SPLIT, axis=1)
    tall = lambda m: jnp.concatenate([m.astype(BF16)] * SPLIT, axis=0)
    tril_ssd = wide((ri >= ci).astype(F32))
    triu_ssd = tall((ri <= ci).astype(F32))
    tril_gdn = wide(jnp.logical_and(ri >= ci, same_gdn_chunk).astype(F32))
    triu_gdn = tall(jnp.logical_and(ri <= ci, same_gdn_chunk).astype(F32))

    n_dt = SPLIT * SSM_HEADS
    p_cols = jnp.dot(h, wc_ref[...], preferred_element_type=F32)
    p_rows = p_cols.T
    pg = p_cols[:, n_dt:n_dt + 2 * GDN_HEADS]
    gates = jnp.where(lax.broadcasted_iota(jnp.int32, pg.shape, 1) < GDN_HEADS, _sigmoid(pg),
                      -jnp.exp(galog_c_ref[...]) * _softplus(pg + gbias_c_ref[...]))
    pgt = p_rows[n_dt:n_dt + 2 * GDN_HEADS]
    gates_t = jnp.where(lax.broadcasted_iota(jnp.int32, pgt.shape, 0) < GDN_HEADS, _sigmoid(pgt),
                        -jnp.exp(galog_r_ref[...]) * _softplus(pgt + gbias_r_ref[...]))
    dt = _softplus(p_cols[:, :n_dt] + dbias_c_ref[...])
    dt_t = _softplus(p_rows[:SSM_HEADS] + dbias_r_ref[...])
    dcol_out[0] = dt
    drow_out[0] = dt_t
    a_col = dt * ssm_a_c_ref[...]
    a_row = dt_t * ssm_a_r_ref[...]
    lane = lax.broadcasted_iota(jnp.int32, (blk, 2 * GDN_HEADS), 1)
    sub = lax.broadcasted_iota(jnp.int32, (2 * GDN_HEADS, blk), 0)
    for r0 in range(0, rows, blk):
        rs = slice(r0, r0 + blk)
        g_blk = gates[rs]
        gcol_out[0, rs, :] = jnp.where(lane < GDN_HEADS, g_blk, _tri_dot_left(tril_gdn, g_blk))
        gt_blk = gates_t[:, rs]
        grow_out[0, :, rs] = jnp.where(sub < GDN_HEADS, gt_blk, _tri_dot_right(gt_blk, triu_gdn))
        acol_out[0, rs, :] = _tri_dot_left(tril_ssd, a_col[rs])
        arow_out[0, :, rs] = _tri_dot_right(a_row[:, rs], triu_ssd)


def _in_proj(x, norm1_w, w_in, gdn_conv_w, gdn_a_log, gdn_dt_bias, ssm_conv_w, ssm_conv_b,
             ssm_dt_bias, ssm_a_log):
    bsz, seq, _ = x.shape
    rows = ROW_TILE
    assert seq % rows == 0
    nh = GDN_HEADS
    off_gates = 4 * GDN_DIM
    off_sz = off_gates + 2 * nh
    off_dt = off_sz + SSM_DIM + SSM_XBC
    assert off_dt + SSM_HEADS == w_in.shape[1]
    w_all = w_in.astype(BF16)
    w_gates = w_all[:, off_gates:off_sz]
    w_dt = w_all[:, off_dt:]
    zeros8 = jnp.zeros((nh,), F32)
    galog16 = jnp.concatenate([zeros8, gdn_a_log.astype(F32)])
    gbias16 = jnp.concatenate([zeros8, gdn_dt_bias.astype(F32)])
    dbias = ssm_dt_bias.astype(F32)
    ssm_a = -jnp.exp(ssm_a_log.astype(F32))
    tile_row = jnp.arange(rows)
    src_time = ROW_GROUPS * (tile_row % SUBLANES) + tile_row // SUBLANES
    perm = (src_time[:, None] == jnp.arange(rows)[None, :]).astype(BF16)

    args = [
        x, norm1_w.reshape(1, D_MODEL), perm, w_all,
        jnp.concatenate([jnp.tile(w_dt, (1, SPLIT)), w_gates,
                         jnp.zeros((D_MODEL, LANES - (SPLIT + 1) * SSM_HEADS), BF16)], axis=1),
        gdn_conv_w, ssm_conv_w, ssm_conv_b.reshape(1, SSM_XBC),
        galog16.reshape(1, 2 * nh), gbias16.reshape(1, 2 * nh),
        galog16.reshape(2 * nh, 1), gbias16.reshape(2 * nh, 1),
        jnp.tile(dbias, SPLIT).reshape(1, SPLIT * SSM_HEADS), dbias.reshape(SSM_HEADS, 1),
        jnp.tile(ssm_a, SPLIT).reshape(1, SPLIT * SSM_HEADS), ssm_a.reshape(SSM_HEADS, 1),
    ]
    in_specs = [pl.BlockSpec((1, rows, D_MODEL), lambda b, t: (b, t, 0))]
    in_specs += [_const_spec(a.shape) for a in args[1:]]

    def slab_spec(cols):
        return pl.BlockSpec((1, cols // LANES, rows // 2, LANES), lambda b, t: (b, 0, t, 0))

    def slab_shape(cols):
        return jax.ShapeDtypeStruct((bsz, cols // LANES, seq // 2, LANES), jnp.int32)

    def row_spec(cols):
        return pl.BlockSpec((1, rows, cols), lambda b, t: (b, t, 0))

    def col_major_spec(nrows):
        return pl.BlockSpec((1, nrows, rows), lambda b, t: (b, 0, t))

    out_shape = [
        slab_shape(3 * GDN_DIM),
        slab_shape(GDN_DIM),
        jax.ShapeDtypeStruct((bsz, seq, 2 * nh), F32),
        jax.ShapeDtypeStruct((bsz, 2 * nh, seq), F32),
        slab_shape(SSM_DIM),
        slab_shape(SSM_XBC),
        jax.ShapeDtypeStruct((bsz, seq, SPLIT * SSM_HEADS), F32),
        jax.ShapeDtypeStruct((bsz, SSM_HEADS, seq), F32),
        jax.ShapeDtypeStruct((bsz, seq, SPLIT * SSM_HEADS), F32),
        jax.ShapeDtypeStruct((bsz, SSM_HEADS, seq), F32),
    ]
    out_specs = [slab_spec(3 * GDN_DIM), slab_spec(GDN_DIM), row_spec(2 * nh),
                 col_major_spec(2 * nh), slab_spec(SSM_DIM), slab_spec(SSM_XBC),
                 row_spec(SPLIT * SSM_HEADS), col_major_spec(SSM_HEADS),
                 row_spec(SPLIT * SSM_HEADS), col_major_spec(SSM_HEADS)]
    return pl.pallas_call(
        _in_proj_kernel,
        grid=(bsz, seq // rows),
        in_specs=in_specs,
        out_specs=out_specs,
        out_shape=out_shape,
        scratch_shapes=[pltpu.VMEM((HALO, 3 * GDN_DIM), F32),
                        pltpu.VMEM((HALO, SSM_XBC), F32),
                        pltpu.VMEM((D_MODEL, SSM_DIM + SSM_XBC), BF16)],
        compiler_params=pltpu.CompilerParams(
            dimension_semantics=("arbitrary", "arbitrary"), vmem_limit_bytes=VMEM_LIMIT_BYTES),
        name="in_proj",
    )(*args)


GDN_PAIR = 2
GDN_CHAINS = GDN_PAIR * GDN_HEADS


def _gdn_prepare(qkv_ref, gcol_ref, grow_ref, pair):
    c = GDN_CHUNK
    ri = lax.broadcasted_iota(jnp.int32, (c, 2 * c), 0)
    lane = lax.broadcasted_iota(jnp.int32, (c, 2 * c), 1)
    ci = lane % c
    low = lane < c
    causal = ri >= ci
    strict = ri > ci
    same16 = (ri // 16) == (ci // 16)
    same32 = (ri // 32) == (ci // 32)
    only32 = jnp.logical_and(same32, jnp.logical_not(same16))
    eye = (ri == ci).astype(F32)
    zeros_k = jnp.zeros((c, GDN_DK), BF16)
    zeros_uw = jnp.zeros((c, GDN_DV + GDN_DK), BF16)

    def blockdiag(x_p):
        return jnp.concatenate([jnp.where(low, x_p, 0.0), jnp.where(low, 0.0, x_p)],
                               axis=0).astype(BF16)

    def mm_packed(l_p, x_p):
        return jnp.dot(l_p.astype(BF16), blockdiag(x_p), preferred_element_type=F32)

    q, k, kbs, rhs, exp_g, kdt, cdec = [], [], [], [], [], [], []
    q_pair = [_chunk_pair(qkv_ref, (0, h), pair) for h in range(GDN_HEADS)]
    k_pair = [_chunk_pair(qkv_ref, (0, GDN_HEADS + h), pair) for h in range(GDN_HEADS)]
    v_pair = [_chunk_pair(qkv_ref, (0, 2 * GDN_HEADS + h), pair) for h in range(GDN_HEADS)]
    gram_lhs, gram_rhs, decay = [], [], []
    for j in range(GDN_PAIR):
        s = GDN_PAIR * pair + j
        gcol = gcol_ref[0, _time_rows(s, c), :]
        grow = grow_ref[0, s]
        gc_wide = []
        for h in range(GDN_HEADS):
            q.append(q_pair[h][j])
            k.append(k_pair[h][j])
            vv = v_pair[h][j]
            beta = jnp.broadcast_to(gcol[:, h:h + 1], (c, GDN_DK))
            gcw = jnp.broadcast_to(gcol[:, GDN_HEADS + h:GDN_HEADS + h + 1], (c, GDN_DK))
            g_last = gcw[c - 1:c, :]
            gc_wide.append(gcw)
            exp_g.append(jnp.exp(gcw))
            kbs.append(k[-1] * beta)
            rhs.append(jnp.concatenate([vv * beta, kbs[-1] * exp_g[-1]], axis=1).astype(BF16))
            kdt.append((k[-1] * jnp.exp(g_last - gcw)).T.astype(BF16))
            cdec.append(jnp.broadcast_to(jnp.exp(g_last), (SUBLANES, GDN_DV)))
            if h % 2 == 1:
                a, b = len(q) - 2, len(q) - 1
                gram_lhs.append(jnp.concatenate(
                    [jnp.concatenate([kbs[a], kbs[b]], axis=1),
                     jnp.concatenate([q[a], q[b]], axis=1)], axis=0).astype(BF16))
                gram_rhs.append(jnp.concatenate(
                    [jnp.concatenate([k[a].astype(BF16), zeros_k], axis=1),
                     jnp.concatenate([zeros_k, k[b].astype(BF16)], axis=1)], axis=0))
                gc_p = jnp.where(low, gc_wide[h - 1], gc_wide[h])
                gr_p = grow[h // 2:h // 2 + 1, :]
                decay.append(jnp.exp(jnp.where(causal, gc_p - gr_p, -jnp.inf)))
    gram = [lax.dot_general(l, r, (((1,), (1,)), ((), ())), preferred_element_type=F32)
            for l, r in zip(gram_lhs, gram_rhs)]
    yield None
    a_strict = [jnp.where(strict, g[:c] * d, 0.0) for g, d in zip(gram, decay)]
    attn = [(g[c:] * d).astype(BF16) for g, d in zip(gram, decay)]
    d16 = [jnp.where(same16, a, 0.0) for a in a_strict]
    inv = [eye - x for x in d16]
    xp = [mm_packed(x, x) for x in d16]
    yield None
    for step in range(3):
        inv = [i + mm_packed(i, x) for i, x in zip(inv, xp)]
        if step < 2:
            xp = [mm_packed(x, x) for x in xp]
        yield None
    for mask in (only32, jnp.logical_not(same32)):
        tmp = [mm_packed(i, jnp.where(mask, a, 0.0)) for i, a in zip(inv, a_strict)]
        yield None
        inv = [i - mm_packed(t, i) for i, t in zip(inv, tmp)]
        yield None
    sol = [jnp.dot(t.astype(BF16),
                   jnp.concatenate([jnp.concatenate([rhs[2 * i], zeros_uw], axis=1),
                                    jnp.concatenate([zeros_uw, rhs[2 * i + 1]], axis=1)], axis=0),
                   preferred_element_type=F32) for i, t in enumerate(inv)]
    width = GDN_DV + GDN_DK
    sol = [x[:, half * width:(half + 1) * width] for x in sol for half in range(2)]
    u = [x[:, :GDN_DV] for x in sol]
    wq = [jnp.concatenate([x[:, GDN_DV:], qq * e], axis=0).astype(BF16)
          for x, qq, e in zip(sol, q, exp_g)]
    yield u, wq, attn, kdt, cdec


def _gdn_recur(u_ref, wq_ref, at_ref, kdt_ref, cd_ref, s_ref, gz_ref, nw_ref, o_ref, pair, j):
    c = GDN_CHUNK
    s = GDN_PAIR * pair + j
    chains = [j * GDN_HEADS + h for h in range(GDN_HEADS)]
    state = [s_ref[h] for h in range(GDN_HEADS)]
    ws_qs = [jnp.dot(wq_ref[ch], st.astype(BF16), preferred_element_type=F32)
             for ch, st in zip(chains, state)]
    yield None
    v_new = [(u_ref[ch] - x[:c]).astype(BF16) for ch, x in zip(chains, ws_qs)]
    zeros_v = jnp.zeros((c, GDN_DV), BF16)
    o = []
    for p in range(GDN_HEADS // 2):
        vn = jnp.concatenate([jnp.concatenate([v_new[2 * p], zeros_v], axis=1),
                              jnp.concatenate([zeros_v, v_new[2 * p + 1]], axis=1)], axis=0)
        o_p = jnp.dot(at_ref[j * (GDN_HEADS // 2) + p], vn, preferred_element_type=F32)
        o.append(ws_qs[2 * p][c:] + o_p[:, :GDN_DV])
        o.append(ws_qs[2 * p + 1][c:] + o_p[:, GDN_DV:])
    for h, ch in enumerate(chains):
        s_ref[h] = (state[h] * cd_ref[ch][:1]
                    + jnp.dot(kdt_ref[ch], v_new[h], preferred_element_type=F32))
    rows = _time_rows(s, c)
    for h in range(GDN_HEADS):
        gate = _chunk_pair(gz_ref, (0, h), pair)[j]
        o_ref[0, rows, h * GDN_DV:(h + 1) * GDN_DV] = (
            _rmsnorm(o[h], nw_ref[...]) * gate).astype(o_ref.dtype)
    yield None


def _spread_lanes(cols, onehot):
    heads = cols.shape[1] // SPLIT
    replica = lax.broadcasted_iota(jnp.int32, cols.shape, 1) // heads
    hi, mid, lo = _split(cols)
    terms = jnp.where(replica == 0, hi, jnp.where(replica == 1, mid, lo))
    return jnp.dot(terms, onehot, preferred_element_type=F32)


def _ssd_chunk(xbc_ref, sz_ref, dcol_ref, acol_ref, drow_ref, arow_ref, dskip_ref, nw_ref,
               e64_ref, e128_ref, o_ref, h_ref, n):
    c = SSM_CHUNK
    sub_per_chunk = c // GDN_CHUNK
    x_slabs = SSM_DIM // LANES
    ri = lax.broadcasted_iota(jnp.int32, (c, c), 0)
    ci = lax.broadcasted_iota(jnp.int32, (c, c), 1)
    causal = ri >= ci
    pair_w = 2 * SSM_HEADDIM
    low_half = lax.broadcasted_iota(jnp.int32, (c, pair_w), 1) < SSM_HEADDIM
    high_half = jnp.logical_not(low_half)
    gd = SSM_GROUP_DIM
    pairs_per_group = SSM_HPG // 2
    groups = range(SSM_GROUPS)
    rs = _time_rows(n, c)
    s0 = n * sub_per_chunk

    def slab(ref, idx):
        return jnp.concatenate(_chunk_pair(ref, (0, idx), n), axis=0)

    dt_col = dcol_ref[0, rs, :]
    acs_col = acol_ref[0, rs, :]
    dt_row = drow_ref[0, n]
    acs_row = arow_ref[0, n]
    a_last = acs_col[c - 1:c, :]
    coef = jnp.exp(a_last - acs_col) * dt_col
    chunk_decay = jnp.broadcast_to(jnp.exp(a_last), (SUBLANES, a_last.shape[1]))
    acs_wide = _spread_lanes(acs_col, e128_ref[...])
    wide = _spread_lanes(jnp.concatenate([coef, jnp.exp(acs_col), chunk_decay], axis=0),
                         e64_ref[...])
    coef_wide, exp_a_wide, cd_wide = wide[:c], wide[c:2 * c], wide[2 * c:2 * c + 1]
    yield None
    bg = [slab(xbc_ref, x_slabs + g) for g in groups]
    cg = [slab(xbc_ref, x_slabs + SSM_GROUPS + g) for g in groups]
    cb = [_mm_nt(cg[g], bg[g]) for g in groups]
    h_prev = [h_ref[g] for g in groups]
    y_off = [_mm(cg[g], h_prev[g]) * exp_a_wide[:, g * gd:(g + 1) * gd] for g in groups]
    yield None
    x_pairs = [[slab(xbc_ref, g * pairs_per_group + jp) for jp in range(pairs_per_group)]
               for g in groups]
    x_g = [jnp.concatenate(x_pairs[g], axis=1) for g in groups]
    for g in groups:
        gs = slice(g * gd, (g + 1) * gd)
        h_ref[g] = h_prev[g] * cd_wide[:, gs] + _mm_tn(bg[g], x_g[g] * coef_wide[:, gs])
    yield None
    y_pairs = [[], []]
    for g in groups:
        for jp in range(pairs_per_group):
            hd = g * SSM_HPG + 2 * jp
            acc = y_off[g][:, jp * pair_w:(jp + 1) * pair_w]
            for half, mask in enumerate((low_half, high_half)):
                ac = acs_wide[:, (hd + half) * LANES:(hd + half + 1) * LANES]
                ar = acs_row[hd + half:hd + half + 1, :]
                seg = jnp.exp(jnp.where(causal, ac - ar, -jnp.inf))
                scores = seg * (cb[g] * dt_row[hd + half:hd + half + 1, :])
                acc = acc + _mm(scores, jnp.where(mask, x_pairs[g][jp], 0.0))
            y_pairs[g].append(acc)
            if jp % 2 == 1:
                yield None
    for g in groups:
        gs = slice(g * gd, (g + 1) * gd)
        y = jnp.concatenate(y_pairs[g], axis=1) + x_g[g] * dskip_ref[:, gs]
        gate = jnp.concatenate(
            [slab(sz_ref, g * pairs_per_group + jp) for jp in range(pairs_per_group)], axis=1)
        yg = y * gate
        yg = yg * lax.rsqrt(jnp.mean(yg * yg, axis=-1, keepdims=True) + EPS)
        o_ref[0, rs, gs] = (yg * nw_ref[:, gs]).astype(o_ref.dtype)
    yield None


def _interleave(*gens):
    last = [None] * len(gens)
    live = list(range(len(gens)))
    while live:
        for i in list(live):
            try:
                last[i] = next(gens[i])
            except StopIteration:
                live.remove(i)
    return last


def _spaced(gen, gap):
    for value in gen:
        yield value
        for _ in range(gap):
            yield value


def _mixers_kernel(qkv_ref, gz_ref, gcol_ref, grow_ref, gnw_ref,
                   xbc_ref, sz_ref, dcol_ref, acol_ref, drow_ref, arow_ref, dskip_ref, snw_ref,
                   e64_ref, e128_ref, oa_ref, ob_ref,
                   s_ref, u_ref, wq_ref, at_ref, kdt_ref, cd_ref, h_ref):
    npairs = ROW_TILE // (GDN_PAIR * GDN_CHUNK)
    assert npairs == ROW_TILE // SSM_CHUNK

    @pl.when(pl.program_id(1) == 0)
    def _():
        s_ref[...] = jnp.zeros_like(s_ref)
        h_ref[...] = jnp.zeros_like(h_ref)

    def store_prepared(prepared):
        u, wq, attn, kdt, cdec = prepared
        for ch in range(GDN_CHAINS):
            u_ref[ch] = u[ch]
            wq_ref[ch] = wq[ch]
            kdt_ref[ch] = kdt[ch]
            cd_ref[ch] = cdec[ch]
        for pc in range(GDN_CHAINS // 2):
            at_ref[pc] = attn[pc]

    def recur(pair):
        for j in range(GDN_PAIR):
            yield from _gdn_recur(u_ref, wq_ref, at_ref, kdt_ref, cd_ref, s_ref, gz_ref, gnw_ref,
                                  oa_ref, pair, j)

    def ssd(n):
        return _ssd_chunk(xbc_ref, sz_ref, dcol_ref, acol_ref, drow_ref, arow_ref, dskip_ref,
                          snw_ref, e64_ref, e128_ref, ob_ref, h_ref, n)

    prepared, _ = _interleave(_gdn_prepare(qkv_ref, gcol_ref, grow_ref, 0), ssd(0))
    store_prepared(prepared)

    def body(pair, carry):
        prepared, _, _ = _interleave(_gdn_prepare(qkv_ref, gcol_ref, grow_ref, pair + 1),
                                     _spaced(recur(pair), 1), ssd(pair + 1))
        store_prepared(prepared)
        return carry

    lax.fori_loop(0, npairs - 1, body, 0)

    _interleave(recur(npairs - 1))


def _mixers(qkv, gz, gcol, grow, gdn_norm_w, xbc, sz, dcol, acol, drow, arow, d_skip, ssm_norm_w):
    bsz, seq, _ = gcol.shape
    rows = ROW_TILE
    nh2 = 2 * GDN_HEADS
    nh = SSM_HEADS
    c = GDN_CHUNK
    grow_c = grow[:, GDN_HEADS:].reshape(bsz, GDN_HEADS, seq // c, c).transpose(0, 2, 1, 3)
    grow_c = grow_c.reshape(bsz, seq // c, GDN_HEADS // 2, 2 * c)
    per_chunk = lambda a: a.reshape(bsz, nh, seq // SSM_CHUNK, SSM_CHUNK).transpose(0, 2, 1, 3)
    dskip = jnp.repeat(d_skip.astype(F32), SSM_HEADDIM).reshape(1, SSM_DIM)
    eye = jnp.tile(jnp.eye(nh, dtype=BF16), (SPLIT, 1))
    e64 = jnp.repeat(eye, SSM_HEADDIM, axis=1)
    e128 = jnp.repeat(eye, LANES, axis=1)

    def slab_spec(cols):
        return pl.BlockSpec((1, cols // LANES, rows // 2, LANES), lambda b, t: (b, 0, t, 0))

    def row_spec(cols):
        return pl.BlockSpec((1, rows, cols), lambda b, t: (b, t, 0))

    def per_chunk_spec(nrows, chunk):
        return pl.BlockSpec((1, rows // chunk, nrows, chunk), lambda b, t: (b, t, 0, 0))

    out_shape = jax.ShapeDtypeStruct((bsz, seq, D_MODEL), BF16)
    return pl.pallas_call(
        _mixers_kernel,
        grid=(bsz, seq // rows),
        in_specs=[
            slab_spec(3 * GDN_DIM), slab_spec(GDN_DIM), row_spec(nh2),
            pl.BlockSpec((1, rows // c, GDN_HEADS // 2, 2 * c), lambda b, t: (b, t, 0, 0)),
            _const_spec((1, GDN_DV)),
            slab_spec(SSM_XBC), slab_spec(SSM_DIM), row_spec(SPLIT * nh), row_spec(SPLIT * nh),
            per_chunk_spec(nh, SSM_CHUNK), per_chunk_spec(nh, SSM_CHUNK),
            _const_spec((1, SSM_DIM)), _const_spec((1, SSM_DIM)),
            _const_spec((SPLIT * nh, SSM_DIM)), _const_spec((SPLIT * nh, nh * LANES)),
        ],
        out_specs=[row_spec(GDN_DIM), row_spec(SSM_DIM)],
        out_shape=[out_shape, out_shape],
        scratch_shapes=[
            pltpu.VMEM((GDN_HEADS, GDN_DK, GDN_DV), F32),
            pltpu.VMEM((GDN_CHAINS, c, GDN_DV), F32),
            pltpu.VMEM((GDN_CHAINS, 2 * c, GDN_DK), BF16),
            pltpu.VMEM((GDN_CHAINS // 2, c, 2 * c), BF16),
            pltpu.VMEM((GDN_CHAINS, GDN_DK, c), BF16),
            pltpu.VMEM((GDN_CHAINS, SUBLANES, GDN_DV), F32),
            pltpu.VMEM((SSM_GROUPS, SSM_STATE, SSM_GROUP_DIM), F32),
        ],
        compiler_params=pltpu.CompilerParams(
            dimension_semantics=("arbitrary", "arbitrary"), vmem_limit_bytes=VMEM_LIMIT_BYTES),
        name="mixers",
    )(qkv, gz, gcol, grow_c, gdn_norm_w.reshape(1, GDN_DV).astype(F32),
      xbc, sz, dcol, acol, per_chunk(drow), per_chunk(arow), dskip,
      ssm_norm_w.reshape(1, SSM_DIM).astype(F32), e64, e128)


def _mem_kv_kernel(mem_ref, nw_ref, wk_ref, wv_ref, k_out, v_out):
    m = _rmsnorm(mem_ref[0], nw_ref[...]).astype(BF16)
    k_out[0] = jnp.dot(m, wk_ref[...], preferred_element_type=F32).astype(k_out.dtype)
    v_out[0] = jnp.dot(m, wv_ref[...], preferred_element_type=F32).astype(v_out.dtype)


def _mem_kv(mem, mem_norm_w, wk, wv):
    bsz, mlen, _ = mem.shape
    kv_shape = jax.ShapeDtypeStruct((bsz, mlen, D_MODEL), BF16)
    return pl.pallas_call(
        _mem_kv_kernel,
        grid=(bsz,),
        in_specs=[pl.BlockSpec((1, mlen, D_MODEL), lambda b: (b, 0, 0)),
                  _const_spec((1, D_MODEL)),
                  _const_spec((D_MODEL, D_MODEL)),
                  _const_spec((D_MODEL, D_MODEL))],
        out_specs=[pl.BlockSpec((1, mlen, D_MODEL), lambda b: (b, 0, 0))] * 2,
        out_shape=[kv_shape, kv_shape],
        compiler_params=pltpu.CompilerParams(
            dimension_semantics=("arbitrary",), vmem_limit_bytes=VMEM_LIMIT_BYTES),
        name="mem_kv",
    )(mem, mem_norm_w.reshape(1, D_MODEL), wk.astype(BF16), wv.astype(BF16))


def _tail_kernel(x_ref, oa_ref, ob_ref, k_ref, v_ref, wout_ref, n2_ref, wq_ref, wo_ref, n3_ref,
                 wup_ref, wdown_ref, nf_ref, y_ref):
    x = x_ref[0]
    x = x + jnp.dot(oa_ref[0], wout_ref[:GDN_DIM, :], preferred_element_type=F32)
    x = x + jnp.dot(ob_ref[0], wout_ref[GDN_DIM:, :], preferred_element_type=F32)

    h = _rmsnorm(x, n2_ref[...]).astype(BF16)
    q = jnp.dot(h, wq_ref[...], preferred_element_type=F32)
    scale = MEM_HEADDIM ** -0.5
    heads = []
    for hd in range(MEM_HEADS):
        hs = slice(hd * MEM_HEADDIM, (hd + 1) * MEM_HEADDIM)
        s = _mm_nt(q[:, hs], k_ref[0, :, hs]) * scale
        s = s - jnp.max(s, axis=-1, keepdims=True)
        p = jnp.exp(s)
        p = p / jnp.sum(p, axis=-1, keepdims=True)
        heads.append(_mm(p, v_ref[0, :, hs]))
    attn = jnp.concatenate(heads, axis=1)
    x = x + _mm(attn, wo_ref[...])

    h = _rmsnorm(x, n3_ref[...]).astype(BF16)
    acc = x
    for c0 in range(0, D_FF, FF_GROUP):
        u = jnp.maximum(jnp.dot(h, wup_ref[:, c0:c0 + FF_GROUP], preferred_element_type=F32), 0.0)
        acc = acc + _mm(u * u, wdown_ref[c0:c0 + FF_GROUP, :])
    y_ref[0] = _rmsnorm(acc, nf_ref[...])


def _tail(x, o_a, o_b, k_mem, v_mem, w_out, norm2_w, wq, wo, norm3_w, w_up, w_down, final_w):
    bsz, seq, _ = x.shape
    rows = min(TAIL_ROWS, seq)
    mlen = k_mem.shape[1]
    row_spec = pl.BlockSpec((1, rows, D_MODEL), lambda b, t: (b, t, 0))
    mem_spec = pl.BlockSpec((1, mlen, D_MODEL), lambda b, t: (b, 0, 0))
    vec = lambda w: w.reshape(1, D_MODEL).astype(F32)
    return pl.pallas_call(
        _tail_kernel,
        grid=(bsz, seq // rows),
        in_specs=[row_spec, row_spec, row_spec, mem_spec, mem_spec,
                  _const_spec((GDN_DIM + SSM_DIM, D_MODEL)), _const_spec((1, D_MODEL)),
                  _const_spec((D_MODEL, D_MODEL)), _const_spec((D_MODEL, D_MODEL)),
                  _const_spec((1, D_MODEL)), _const_spec((D_MODEL, D_FF)),
                  _const_spec((D_FF, D_MODEL)), _const_spec((1, D_MODEL))],
        out_specs=row_spec,
        out_shape=jax.ShapeDtypeStruct((bsz, seq, D_MODEL), x.dtype),
        compiler_params=pltpu.CompilerParams(
            dimension_semantics=("arbitrary", "arbitrary"), vmem_limit_bytes=VMEM_LIMIT_BYTES),
        name="tail",
    )(x, o_a, o_b, k_mem, v_mem, w_out.astype(BF16), vec(norm2_w), wq.astype(BF16),
      wo.astype(BF16), vec(norm3_w), w_up.astype(BF16), w_down.astype(BF16), vec(final_w))


def kernel(x, mem, norm1_w, w_in, gdn_conv_w, gdn_a_log, gdn_dt_bias, gdn_norm_w, ssm_conv_w,
           ssm_conv_b, ssm_a_log, ssm_dt_bias, ssm_d, ssm_norm_w, w_out, norm2_w, mem_norm_w,
           wq_mem, wk_mem, wv_mem, wo_mem, norm3_w, w_up, w_down, final_norm_w):
    qkv, gz, gcol, grow, sz, xbc, dcol, drow, acol, arow = _in_proj(
        x, norm1_w, w_in, gdn_conv_w, gdn_a_log, gdn_dt_bias, ssm_conv_w, ssm_conv_b, ssm_dt_bias,
        ssm_a_log)
    o_a, o_b = _mixers(qkv, gz, gcol, grow, gdn_norm_w, xbc, sz, dcol, acol, drow, arow, ssm_d,
                       ssm_norm_w)
    k_mem, v_mem = _mem_kv(mem, mem_norm_w, wk_mem, wv_mem)
    return _tail(x, o_a, o_b, k_mem, v_mem, w_out, norm2_w, wq_mem, wo_mem, norm3_w, w_up, w_down,
                 final_norm_w)
```

```python
import jax
import jax.numpy as jnp
from jax import lax
from jax.experimental import pallas as pl
from jax.experimental.pallas import tpu as pltpu

D_MODEL = 1024
EPS = 1e-6
CONV_K = 4
GDN_HEADS = 8
GDN_DK = 128
GDN_DV = 128
GDN_DIM = GDN_HEADS * GDN_DV
GDN_CHUNK = 64
SSM_DIM = D_MODEL
SSM_HEADDIM = 64
SSM_HEADS = SSM_DIM // SSM_HEADDIM
SSM_GROUPS = 2
SSM_HPG = SSM_HEADS // SSM_GROUPS
SSM_STATE = 128
SSM_CHUNK = 128
SSM_BC = SSM_GROUPS * SSM_STATE
SSM_XBC = SSM_DIM + 2 * SSM_BC
SSM_GROUP_DIM = SSM_DIM // SSM_GROUPS
MEM_HEADS = 4
MEM_HEADDIM = D_MODEL // MEM_HEADS
D_FF = 4 * D_MODEL

F32 = jnp.float32
BF16 = jnp.bfloat16

V7X_VMEM_BYTES = 64 * 1024 * 1024
VMEM_LIMIT_BYTES = V7X_VMEM_BYTES - 8 * 1024 * 1024
SUBLANES = 8
LANES = 128

ROW_TILE = 512
ROW_GROUPS = ROW_TILE // SUBLANES
assert ROW_GROUPS == GDN_CHUNK and SSM_CHUNK == 2 * GDN_CHUNK
TAIL_ROWS = 512
COL_GROUP = 512
FF_GROUP = 1024
HALO = (CONV_K - 1) * SUBLANES


def _mm(a, b):
    return jnp.dot(a.astype(BF16), b.astype(BF16), preferred_element_type=F32)


def _mm_nt(a, b):
    return lax.dot_general(a.astype(BF16), b.astype(BF16), (((1,), (1,)), ((), ())),
                           preferred_element_type=F32)


def _mm_tn(a, b):
    return lax.dot_general(a.astype(BF16), b.astype(BF16), (((0,), (0,)), ((), ())),
                           preferred_element_type=F32)


SPLIT = 3


def _split(x):
    hi = x.astype(BF16)
    rest = x - hi.astype(F32)
    mid = rest.astype(BF16)
    lo = (rest - mid.astype(F32)).astype(BF16)
    return hi, mid, lo


def _tri_dot_left(tri_wide, x):
    return jnp.dot(tri_wide, jnp.concatenate(_split(x), axis=0), preferred_element_type=F32)


def _tri_dot_right(x, tri_tall):
    return jnp.dot(jnp.concatenate(_split(x), axis=1), tri_tall, preferred_element_type=F32)


def _rmsnorm(x, w):
    return x * lax.rsqrt(jnp.mean(x * x, axis=-1, keepdims=True) + EPS) * w


def _sigmoid(x):
    return 1.0 / (1.0 + jnp.exp(-x))


def _silu(x):
    half = 0.5 * x
    return half * (1.0 + jnp.tanh(half))


def _softplus(x):
    return jnp.maximum(x, 0.0) + jnp.log(1.0 + jnp.exp(-jnp.abs(x)))


def _const_spec(shape):
    zeros = (0,) * len(shape)
    return pl.BlockSpec(shape, lambda *_: zeros, pipeline_mode=pl.Buffered(1))


def _time_rows(chunk, size):
    start = chunk * size
    return pl.ds(start if isinstance(start, int) else pl.multiple_of(start, size), size)


def _row_pair_words(y):
    return pltpu.bitcast(y.astype(BF16), jnp.int32)


def _chunk_pair(ref, lead, pair):
    words = ref[lead + (pl.ds(pair, ROW_GROUPS, stride=SUBLANES // 2), slice(None))]
    even = pltpu.bitcast(lax.shift_left(words, 16), F32)
    odd = pltpu.bitcast(lax.bitwise_and(words, jnp.int32(-65536)), F32)
    return even, odd


def _conv_silu(p, halo, w, bias):
    rows = p.shape[0]
    sub0 = lax.broadcasted_iota(jnp.int32, (SUBLANES, p.shape[1]), 0) == 0
    wrapped = []
    for j in range(CONV_K - 1):
        cur = p[rows - HALO + j * SUBLANES:rows - HALO + (j + 1) * SUBLANES]
        prev = halo[j * SUBLANES:(j + 1) * SUBLANES]
        wrapped.append(jnp.where(sub0, pltpu.roll(prev, 1, 0), pltpu.roll(cur, 1, 0)))
    acc = p * w[CONV_K - 1:CONV_K, :]
    if bias is not None:
        acc = acc + bias
    for k in range(1, CONV_K):
        shifted = jnp.concatenate(wrapped[CONV_K - 1 - k:] + [p[:rows - k * SUBLANES]], axis=0)
        acc = acc + shifted * w[CONV_K - 1 - k:CONV_K - k, :]
    return _silu(acc)


def _in_proj_kernel(x_ref, n1_ref, perm_ref, w_ref,
                    gconv_ref, sconv_ref, sconvb_ref,
                    galog_c_ref, gbias_c_ref, galog_r_ref, gbias_r_ref, dbias_c_ref, dbias_r_ref,
                    ssm_a_c_ref, ssm_a_r_ref,
                    qkv_out, gz_out, gcol_out, grow_out, sz_out, xbc_out, dcol_out, drow_out,
                    acol_out, arow_out, ghalo_ref, shalo_ref, ws_ref, wc_ref):
    rows = x_ref.shape[1]

    @pl.when(pl.program_id(1) == 0)
    def _():
        ghalo_ref[...] = jnp.zeros_like(ghalo_ref)
        shalo_ref[...] = jnp.zeros_like(shalo_ref)

    @pl.when(jnp.logical_and(pl.program_id(0) == 0, pl.program_id(1) == 0))
    def _():
        words = pltpu.bitcast(w_ref[:, 4 * GDN_DIM:], jnp.int32)
        lo = 2 * GDN_HEADS
        hi = lo + SSM_DIM + SSM_XBC
        ws_ref[...] = pltpu.bitcast(words[:, lo:hi], BF16)
        pad = jnp.zeros((words.shape[0], LANES - SPLIT * SSM_HEADS - lo), jnp.int32)
        wc_ref[...] = pltpu.bitcast(
            jnp.concatenate([words[:, hi:hi + SSM_HEADS]] * SPLIT + [words[:, :lo], pad], axis=1),
            BF16)

    h = _rmsnorm(x_ref[0], n1_ref[...]).astype(BF16)
    hp = jnp.dot(perm_ref[...], h, preferred_element_type=F32).astype(BF16)

    off_gz, off_sz, off_xbc = 3 * GDN_DIM, 0, SSM_DIM
    q_scale = GDN_DK ** -0.5
    for c0 in range(0, 3 * GDN_DIM, COL_GROUP):
        cs = slice(c0, c0 + COL_GROUP)
        p = jnp.dot(hp, w_ref[:, cs], preferred_element_type=F32)
        y = _conv_silu(p, ghalo_ref[:, cs], gconv_ref[:, cs], None)
        ghalo_ref[:, cs] = p[rows - HALO:]
        for j in range(COL_GROUP // LANES):
            yh = y[:, j * LANES:(j + 1) * LANES]
            if c0 < 2 * GDN_DIM:
                yh = yh * lax.rsqrt(jnp.sum(yh * yh, axis=-1, keepdims=True) + EPS)
                if c0 < GDN_DIM:
                    yh = yh * q_scale
            qkv_out[0, c0 // LANES + j] = _row_pair_words(yh)

    for c0 in range(0, GDN_DIM, COL_GROUP):
        gz = _silu(jnp.dot(hp, w_ref[:, off_gz + c0:off_gz + c0 + COL_GROUP],
                           preferred_element_type=F32))
        sz = _silu(jnp.dot(hp, ws_ref[:, off_sz + c0:off_sz + c0 + COL_GROUP],
                           preferred_element_type=F32))
        for j in range(COL_GROUP // LANES):
            gz_out[0, c0 // LANES + j] = _row_pair_words(gz[:, j * LANES:(j + 1) * LANES])
            sz_out[0, c0 // LANES + j] = _row_pair_words(sz[:, j * LANES:(j + 1) * LANES])

    for c0 in range(0, SSM_XBC, COL_GROUP):
        cs = slice(c0, c0 + COL_GROUP)
        p = jnp.dot(hp, ws_ref[:, off_xbc + c0:off_xbc + c0 + COL_GROUP],
                    preferred_element_type=F32)
        y = _conv_silu(p, shalo_ref[:, cs], sconv_ref[:, cs], sconvb_ref[:, cs])
        shalo_ref[:, cs] = p[rows - HALO:]
        for j in range(COL_GROUP // LANES):
            xbc_out[0, c0 // LANES + j] = _row_pair_words(y[:, j * LANES:(j + 1) * LANES])

    blk = SSM_CHUNK
    ri = lax.broadcasted_iota(jnp.int32, (blk, blk), 0)
    ci = lax.broadcasted_iota(jnp.int32, (blk, blk), 1)
    same_gdn_chunk = (ri // GDN_CHUNK) == (ci // GDN_CHUNK)
    wide = lambda m: jnp.concatenate([m.astype(BF16)] * SPLIT, axis=1)
    tall = lambda m: jnp.concatenate([m.astype(BF16)] * SPLIT, axis=0)
    tril_ssd = wide((ri >= ci).astype(F32))
    triu_ssd = tall((ri <= ci).astype(F32))
    tril_gdn = wide(jnp.logical_and(ri >= ci, same_gdn_chunk).astype(F32))
    triu_gdn = tall(jnp.logical_and(ri <= ci, same_gdn_chunk).astype(F32))

    n_dt = SPLIT * SSM_HEADS
    p_cols = jnp.dot(h, wc_ref[...], preferred_element_type=F32)
    p_rows = p_cols.T
    pg = p_cols[:, n_dt:n_dt + 2 * GDN_HEADS]
    gates = jnp.where(lax.broadcasted_iota(jnp.int32, pg.shape, 1) < GDN_HEADS, _sigmoid(pg),
                      -jnp.exp(galog_c_ref[...]) * _softplus(pg + gbias_c_ref[...]))
    pgt = p_rows[n_dt:n_dt + 2 * GDN_HEADS]
    gates_t = jnp.where(lax.broadcasted_iota(jnp.int32, pgt.shape, 0) < GDN_HEADS, _sigmoid(pgt),
                        -jnp.exp(galog_r_ref[...]) * _softplus(pgt + gbias_r_ref[...]))
    dt = _softplus(p_cols[:, :n_dt] + dbias_c_ref[...])
    dt_t = _softplus(p_rows[:SSM_HEADS] + dbias_r_ref[...])
    dcol_out[0] = dt
    drow_out[0] = dt_t
    a_col = dt * ssm_a_c_ref[...]
    a_row = dt_t * ssm_a_r_ref[...]
    lane = lax.broadcasted_iota(jnp.int32, (blk, 2 * GDN_HEADS), 1)
    sub = lax.broadcasted_iota(jnp.int32, (2 * GDN_HEADS, blk), 0)
    for r0 in range(0, rows, blk):
        rs = slice(r0, r0 + blk)
        g_blk = gates[rs]
        gcol_out[0, rs, :] = jnp.where(lane < GDN_HEADS, g_blk, _tri_dot_left(tril_gdn, g_blk))
        gt_blk = gates_t[:, rs]
        grow_out[0, :, rs] = jnp.where(sub < GDN_HEADS, gt_blk, _tri_dot_right(gt_blk, triu_gdn))
        acol_out[0, rs, :] = _tri_dot_left(tril_ssd, a_col[rs])
        arow_out[0, :, rs] = _tri_dot_right(a_row[:, rs], triu_ssd)


def _in_proj(x, norm1_w, w_in, gdn_conv_w, gdn_a_log, gdn_dt_bias, ssm_conv_w, ssm_conv_b,
             ssm_dt_bias, ssm_a_log):
    bsz, seq, _ = x.shape
    rows = ROW_TILE
    assert seq % rows == 0
    nh = GDN_HEADS
    off_gates = 4 * GDN_DIM
    off_sz = off_gates + 2 * nh
    off_dt = off_sz + SSM_DIM + SSM_XBC
    assert off_dt + SSM_HEADS == w_in.shape[1]
    w_all = w_in.astype(BF16)
    zeros8 = jnp.zeros((nh,), F32)
    galog16 = jnp.concatenate([zeros8, gdn_a_log.astype(F32)])
    gbias16 = jnp.concatenate([zeros8, gdn_dt_bias.astype(F32)])
    dbias = ssm_dt_bias.astype(F32)
    ssm_a = -jnp.exp(ssm_a_log.astype(F32))
    tile_row = jnp.arange(rows)
    src_time = ROW_GROUPS * (tile_row % SUBLANES) + tile_row // SUBLANES
    perm = (src_time[:, None] == jnp.arange(rows)[None, :]).astype(BF16)

    args = [
        x, norm1_w.reshape(1, D_MODEL), perm, w_all,
        gdn_conv_w, ssm_conv_w, ssm_conv_b.reshape(1, SSM_XBC),
        galog16.reshape(1, 2 * nh), gbias16.reshape(1, 2 * nh),
        galog16.reshape(2 * nh, 1), gbias16.reshape(2 * nh, 1),
        jnp.tile(dbias, SPLIT).reshape(1, SPLIT * SSM_HEADS), dbias.reshape(SSM_HEADS, 1),
        jnp.tile(ssm_a, SPLIT).reshape(1, SPLIT * SSM_HEADS), ssm_a.reshape(SSM_HEADS, 1),
    ]
    in_specs = [pl.BlockSpec((1, rows, D_MODEL), lambda b, t: (b, t, 0))]
    in_specs += [_const_spec(a.shape) for a in args[1:]]

    def slab_spec(cols):
        return pl.BlockSpec((1, cols // LANES, rows // 2, LANES), lambda b, t: (b, 0, t, 0))

    def slab_shape(cols):
        return jax.ShapeDtypeStruct((bsz, cols // LANES, seq // 2, LANES), jnp.int32)

    def row_spec(cols):
        return pl.BlockSpec((1, rows, cols), lambda b, t: (b, t, 0))

    def col_major_spec(nrows):
        return pl.BlockSpec((1, nrows, rows), lambda b, t: (b, 0, t))

    out_shape = [
        slab_shape(3 * GDN_DIM),
        slab_shape(GDN_DIM),
        jax.ShapeDtypeStruct((bsz, seq, 2 * nh), F32),
        jax.ShapeDtypeStruct((bsz, 2 * nh, seq), F32),
        slab_shape(SSM_DIM),
        slab_shape(SSM_XBC),
        jax.ShapeDtypeStruct((bsz, seq, SPLIT * SSM_HEADS), F32),
        jax.ShapeDtypeStruct((bsz, SSM_HEADS, seq), F32),
        jax.ShapeDtypeStruct((bsz, seq, SPLIT * SSM_HEADS), F32),
        jax.ShapeDtypeStruct((bsz, SSM_HEADS, seq), F32),
    ]
    out_specs = [slab_spec(3 * GDN_DIM), slab_spec(GDN_DIM), row_spec(2 * nh),
                 col_major_spec(2 * nh), slab_spec(SSM_DIM), slab_spec(SSM_XBC),
                 row_spec(SPLIT * SSM_HEADS), col_major_spec(SSM_HEADS),
                 row_spec(SPLIT * SSM_HEADS), col_major_spec(SSM_HEADS)]
    return pl.pallas_call(
        _in_proj_kernel,
        grid=(bsz, seq // rows),
        in_specs=in_specs,
        out_specs=out_specs,
        out_shape=out_shape,
        scratch_shapes=[pltpu.VMEM((HALO, 3 * GDN_DIM), F32),
                        pltpu.VMEM((HALO, SSM_XBC), F32),
                        pltpu.VMEM((D_MODEL, SSM_DIM + SSM_XBC), BF16),
                        pltpu.VMEM((D_MODEL, LANES), BF16)],
        compiler_params=pltpu.CompilerParams(
            dimension_semantics=("arbitrary", "arbitrary"), vmem_limit_bytes=VMEM_LIMIT_BYTES),
        name="in_proj",
    )(*args)


GDN_PAIR = 2
GDN_CHAINS = GDN_PAIR * GDN_HEADS


def _gdn_prepare(qkv_ref, gcol_ref, grow_ref, pair):
    c = GDN_CHUNK
    ri = lax.broadcasted_iota(jnp.int32, (c, 2 * c), 0)
    lane = lax.broadcasted_iota(jnp.int32, (c, 2 * c), 1)
    ci = lane % c
    low = lane < c
    causal = ri >= ci
    strict = ri > ci
    same16 = (ri // 16) == (ci // 16)
    same32 = (ri // 32) == (ci // 32)
    only32 = jnp.logical_and(same32, jnp.logical_not(same16))
    eye = (ri == ci).astype(F32)
    zeros_k = jnp.zeros((c, GDN_DK), BF16)
    zeros_uw = jnp.zeros((c, GDN_DV + GDN_DK), BF16)

    def blockdiag(x_p):
        return jnp.concatenate([jnp.where(low, x_p, 0.0), jnp.where(low, 0.0, x_p)],
                               axis=0).astype(BF16)

    def mm_packed(l_p, x_p):
        return jnp.dot(l_p.astype(BF16), blockdiag(x_p), preferred_element_type=F32)

    q, k, kbs, rhs, exp_g, kdt, cdec = [], [], [], [], [], [], []
    q_pair = [_chunk_pair(qkv_ref, (0, h), pair) for h in range(GDN_HEADS)]
    k_pair = [_chunk_pair(qkv_ref, (0, GDN_HEADS + h), pair) for h in range(GDN_HEADS)]
    v_pair = [_chunk_pair(qkv_ref, (0, 2 * GDN_HEADS + h), pair) for h in range(GDN_HEADS)]
    gram_lhs, gram_rhs, decay = [], [], []
    for j in range(GDN_PAIR):
        s = GDN_PAIR * pair + j
        gcol = gcol_ref[0, _time_rows(s, c), :]
        grow = grow_ref[0, s]
        gc_wide = []
        for h in range(GDN_HEADS):
            q.append(q_pair[h][j])
            k.append(k_pair[h][j])
            vv = v_pair[h][j]
            beta = jnp.broadcast_to(gcol[:, h:h + 1], (c, GDN_DK))
            gcw = jnp.broadcast_to(gcol[:, GDN_HEADS + h:GDN_HEADS + h + 1], (c, GDN_DK))
            g_last = gcw[c - 1:c, :]
            gc_wide.append(gcw)
            exp_g.append(jnp.exp(gcw))
            kbs.append(k[-1] * beta)
            rhs.append(jnp.concatenate([vv * beta, kbs[-1] * exp_g[-1]], axis=1).astype(BF16))
            kdt.append((k[-1] * jnp.exp(g_last - gcw)).T.astype(BF16))
            cdec.append(jnp.broadcast_to(jnp.exp(g_last), (SUBLANES, GDN_DV)))
            if h % 2 == 1:
                a, b = len(q) - 2, len(q) - 1
                gram_lhs.append(jnp.concatenate(
                    [jnp.concatenate([kbs[a], kbs[b]], axis=1),
                     jnp.concatenate([q[a], q[b]], axis=1)], axis=0).astype(BF16))
                gram_rhs.append(jnp.concatenate(
                    [jnp.concatenate([k[a].astype(BF16), zeros_k], axis=1),
                     jnp.concatenate([zeros_k, k[b].astype(BF16)], axis=1)], axis=0))
                gc_p = jnp.where(low, gc_wide[h - 1], gc_wide[h])
                gr_p = grow[h // 2:h // 2 + 1, :]
                decay.append(jnp.exp(jnp.where(causal, gc_p - gr_p, -jnp.inf)))
    gram = [lax.dot_general(l, r, (((1,), (1,)), ((), ())), preferred_element_type=F32)
            for l, r in zip(gram_lhs, gram_rhs)]
    yield None
    a_strict = [jnp.where(strict, g[:c] * d, 0.0) for g, d in zip(gram, decay)]
    attn = [(g[c:] * d).astype(BF16) for g, d in zip(gram, decay)]
    d16 = [jnp.where(same16, a, 0.0) for a in a_strict]
    inv = [eye - x for x in d16]
    xp = [mm_packed(x, x) for x in d16]
    yield None
    for step in range(3):
        inv = [i + mm_packed(i, x) for i, x in zip(inv, xp)]
        if step < 2:
            xp = [mm_packed(x, x) for x in xp]
        yield None
    for mask in (only32, jnp.logical_not(same32)):
        tmp = [mm_packed(i, jnp.where(mask, a, 0.0)) for i, a in zip(inv, a_strict)]
        yield None
        inv = [i - mm_packed(t, i) for i, t in zip(inv, tmp)]
        yield None
    sol = [jnp.dot(t.astype(BF16),
                   jnp.concatenate([jnp.concatenate([rhs[2 * i], zeros_uw], axis=1),
                                    jnp.concatenate([zeros_uw, rhs[2 * i + 1]], axis=1)], axis=0),
                   preferred_element_type=F32) for i, t in enumerate(inv)]
    width = GDN_DV + GDN_DK
    sol = [x[:, half * width:(half + 1) * width] for x in sol for half in range(2)]
    u = [x[:, :GDN_DV] for x in sol]
    wq = [jnp.concatenate([x[:, GDN_DV:], qq * e], axis=0).astype(BF16)
          for x, qq, e in zip(sol, q, exp_g)]
    yield u, wq, attn, kdt, cdec


def _gdn_recur(u_ref, wq_ref, at_ref, kdt_ref, cd_ref, s_ref, gz_ref, nw_ref, o_ref, pair, j):
    c = GDN_CHUNK
    s = GDN_PAIR * pair + j
    chains = [j * GDN_HEADS + h for h in range(GDN_HEADS)]
    state = [s_ref[h] for h in range(GDN_HEADS)]
    ws_qs = [jnp.dot(wq_ref[ch], st.astype(BF16), preferred_element_type=F32)
             for ch, st in zip(chains, state)]
    yield None
    v_new = [(u_ref[ch] - x[:c]).astype(BF16) for ch, x in zip(chains, ws_qs)]
    zeros_v = jnp.zeros((c, GDN_DV), BF16)
    o = []
    for p in range(GDN_HEADS // 2):
        vn = jnp.concatenate([jnp.concatenate([v_new[2 * p], zeros_v], axis=1),
                              jnp.concatenate([zeros_v, v_new[2 * p + 1]], axis=1)], axis=0)
        o_p = jnp.dot(at_ref[j * (GDN_HEADS // 2) + p], vn, preferred_element_type=F32)
        o.append(ws_qs[2 * p][c:] + o_p[:, :GDN_DV])
        o.append(ws_qs[2 * p + 1][c:] + o_p[:, GDN_DV:])
    for h, ch in enumerate(chains):
        s_ref[h] = (state[h] * cd_ref[ch][:1]
                    + jnp.dot(kdt_ref[ch], v_new[h], preferred_element_type=F32))
    rows = _time_rows(s, c)
    for h in range(GDN_HEADS):
        gate = _chunk_pair(gz_ref, (0, h), pair)[j]
        o_ref[0, rows, h * GDN_DV:(h + 1) * GDN_DV] = (
            _rmsnorm(o[h], nw_ref[...]) * gate).astype(o_ref.dtype)
    yield None


def _spread_lanes(cols, onehot):
    heads = cols.shape[1] // SPLIT
    replica = lax.broadcasted_iota(jnp.int32, cols.shape, 1) // heads
    hi, mid, lo = _split(cols)
    terms = jnp.where(replica == 0, hi, jnp.where(replica == 1, mid, lo))
    return jnp.dot(terms, onehot, preferred_element_type=F32)


def _ssd_chunk(xbc_ref, sz_ref, dcol_ref, acol_ref, drow_ref, arow_ref, dskip_ref, nw_ref,
               e64_ref, e128_ref, o_ref, h_ref, n):
    c = SSM_CHUNK
    sub_per_chunk = c // GDN_CHUNK
    x_slabs = SSM_DIM // LANES
    ri = lax.broadcasted_iota(jnp.int32, (c, c), 0)
    ci = lax.broadcasted_iota(jnp.int32, (c, c), 1)
    causal = ri >= ci
    pair_w = 2 * SSM_HEADDIM
    low_half = lax.broadcasted_iota(jnp.int32, (c, pair_w), 1) < SSM_HEADDIM
    high_half = jnp.logical_not(low_half)
    gd = SSM_GROUP_DIM
    pairs_per_group = SSM_HPG // 2
    groups = range(SSM_GROUPS)
    rs = _time_rows(n, c)
    s0 = n * sub_per_chunk

    def slab(ref, idx):
        return jnp.concatenate(_chunk_pair(ref, (0, idx), n), axis=0)

    dt_col = dcol_ref[0, rs, :]
    acs_col = acol_ref[0, rs, :]
    dt_row = drow_ref[0, n]
    acs_row = arow_ref[0, n]
    a_last = acs_col[c - 1:c, :]
    coef = jnp.exp(a_last - acs_col) * dt_col
    chunk_decay = jnp.broadcast_to(jnp.exp(a_last), (SUBLANES, a_last.shape[1]))
    acs_wide = _spread_lanes(acs_col, e128_ref[...])
    wide = _spread_lanes(jnp.concatenate([coef, jnp.exp(acs_col), chunk_decay], axis=0),
                         e64_ref[...])
    coef_wide, exp_a_wide, cd_wide = wide[:c], wide[c:2 * c], wide[2 * c:2 * c + 1]
    yield None
    bg = [slab(xbc_ref, x_slabs + g) for g in groups]
    cg = [slab(xbc_ref, x_slabs + SSM_GROUPS + g) for g in groups]
    cb = [_mm_nt(cg[g], bg[g]) for g in groups]
    h_prev = [h_ref[g] for g in groups]
    y_off = [_mm(cg[g], h_prev[g]) * exp_a_wide[:, g * gd:(g + 1) * gd] for g in groups]
    yield None
    x_pairs = [[slab(xbc_ref, g * pairs_per_group + jp) for jp in range(pairs_per_group)]
               for g in groups]
    x_g = [jnp.concatenate(x_pairs[g], axis=1) for g in groups]
    for g in groups:
        gs = slice(g * gd, (g + 1) * gd)
        h_ref[g] = h_prev[g] * cd_wide[:, gs] + _mm_tn(bg[g], x_g[g] * coef_wide[:, gs])
    yield None
    y_pairs = [[], []]
    for g in groups:
        for jp in range(pairs_per_group):
            hd = g * SSM_HPG + 2 * jp
            acc = y_off[g][:, jp * pair_w:(jp + 1) * pair_w]
            for half, mask in enumerate((low_half, high_half)):
                ac = acs_wide[:, (hd + half) * LANES:(hd + half + 1) * LANES]
                ar = acs_row[hd + half:hd + half + 1, :]
                seg = jnp.exp(jnp.where(causal, ac - ar, -jnp.inf))
                scores = seg * (cb[g] * dt_row[hd + half:hd + half + 1, :])
                acc = acc + _mm(scores, jnp.where(mask, x_pairs[g][jp], 0.0))
            y_pairs[g].append(acc)
            if jp % 2 == 1:
                yield None
    for g in groups:
        gs = slice(g * gd, (g + 1) * gd)
        y = jnp.concatenate(y_pairs[g], axis=1) + x_g[g] * dskip_ref[:, gs]
        gate = jnp.concatenate(
            [slab(sz_ref, g * pairs_per_group + jp) for jp in range(pairs_per_group)], axis=1)
        yg = y * gate
        yg = yg * lax.rsqrt(jnp.mean(yg * yg, axis=-1, keepdims=True) + EPS)
        o_ref[0, rs, gs] = (yg * nw_ref[:, gs]).astype(o_ref.dtype)
    yield None


def _interleave(*gens):
    last = [None] * len(gens)
    live = list(range(len(gens)))
    while live:
        for i in list(live):
            try:
                last[i] = next(gens[i])
            except StopIteration:
                live.remove(i)
    return last


def _spaced(gen, gap):
    for value in gen:
        yield value
        for _ in range(gap):
            yield value


def _mixers_kernel(qkv_ref, gz_ref, gcol_ref, grow_ref, gnw_ref,
                   xbc_ref, sz_ref, dcol_ref, acol_ref, drow_ref, arow_ref, dskip_ref, snw_ref,
                   e64_ref, e128_ref, oa_ref, ob_ref,
                   s_ref, u_ref, wq_ref, at_ref, kdt_ref, cd_ref, h_ref):
    npairs = ROW_TILE // (GDN_PAIR * GDN_CHUNK)
    assert npairs == ROW_TILE // SSM_CHUNK

    @pl.when(pl.program_id(1) == 0)
    def _():
        s_ref[...] = jnp.zeros_like(s_ref)
        h_ref[...] = jnp.zeros_like(h_ref)

    def store_prepared(prepared):
        u, wq, attn, kdt, cdec = prepared
        for ch in range(GDN_CHAINS):
            u_ref[ch] = u[ch]
            wq_ref[ch] = wq[ch]
            kdt_ref[ch] = kdt[ch]
            cd_ref[ch] = cdec[ch]
        for pc in range(GDN_CHAINS // 2):
            at_ref[pc] = attn[pc]

    def recur(pair):
        for j in range(GDN_PAIR):
            yield from _gdn_recur(u_ref, wq_ref, at_ref, kdt_ref, cd_ref, s_ref, gz_ref, gnw_ref,
                                  oa_ref, pair, j)

    def ssd(n):
        return _ssd_chunk(xbc_ref, sz_ref, dcol_ref, acol_ref, drow_ref, arow_ref, dskip_ref,
                          snw_ref, e64_ref, e128_ref, ob_ref, h_ref, n)

    prepared, _ = _interleave(_gdn_prepare(qkv_ref, gcol_ref, grow_ref, 0), ssd(0))
    store_prepared(prepared)

    def body(pair, carry):
        prepared, _, _ = _interleave(_gdn_prepare(qkv_ref, gcol_ref, grow_ref, pair + 1),
                                     _spaced(recur(pair), 1), ssd(pair + 1))
        store_prepared(prepared)
        return carry

    lax.fori_loop(0, npairs - 1, body, 0)

    _interleave(recur(npairs - 1))


def _mixers(qkv, gz, gcol, grow, gdn_norm_w, xbc, sz, dcol, acol, drow, arow, d_skip, ssm_norm_w):
    bsz, seq, _ = gcol.shape
    rows = ROW_TILE
    nh2 = 2 * GDN_HEADS
    nh = SSM_HEADS
    c = GDN_CHUNK
    grow_c = grow[:, GDN_HEADS:].reshape(bsz, GDN_HEADS, seq // c, c).transpose(0, 2, 1, 3)
    grow_c = grow_c.reshape(bsz, seq // c, GDN_HEADS // 2, 2 * c)
    per_chunk = lambda a: a.reshape(bsz, nh, seq // SSM_CHUNK, SSM_CHUNK).transpose(0, 2, 1, 3)
    dskip = jnp.repeat(d_skip.astype(F32), SSM_HEADDIM).reshape(1, SSM_DIM)
    eye = jnp.tile(jnp.eye(nh, dtype=BF16), (SPLIT, 1))
    e64 = jnp.repeat(eye, SSM_HEADDIM, axis=1)
    e128 = jnp.repeat(eye, LANES, axis=1)

    def slab_spec(cols):
        return pl.BlockSpec((1, cols // LANES, rows // 2, LANES), lambda b, t: (b, 0, t, 0))

    def row_spec(cols):
        return pl.BlockSpec((1, rows, cols), lambda b, t: (b, t, 0))

    def per_chunk_spec(nrows, chunk):
        return pl.BlockSpec((1, rows // chunk, nrows, chunk), lambda b, t: (b, t, 0, 0))

    out_shape = jax.ShapeDtypeStruct((bsz, seq, D_MODEL), BF16)
    return pl.pallas_call(
        _mixers_kernel,
        grid=(bsz, seq // rows),
        in_specs=[
            slab_spec(3 * GDN_DIM), slab_spec(GDN_DIM), row_spec(nh2),
            pl.BlockSpec((1, rows // c, GDN_HEADS // 2, 2 * c), lambda b, t: (b, t, 0, 0)),
            _const_spec((1, GDN_DV)),
            slab_spec(SSM_XBC), slab_spec(SSM_DIM), row_spec(SPLIT * nh), row_spec(SPLIT * nh),
            per_chunk_spec(nh, SSM_CHUNK), per_chunk_spec(nh, SSM_CHUNK),
            _const_spec((1, SSM_DIM)), _const_spec((1, SSM_DIM)),
            _const_spec((SPLIT * nh, SSM_DIM)), _const_spec((SPLIT * nh, nh * LANES)),
        ],
        out_specs=[row_spec(GDN_DIM), row_spec(SSM_DIM)],
        out_shape=[out_shape, out_shape],
        scratch_shapes=[
            pltpu.VMEM((GDN_HEADS, GDN_DK, GDN_DV), F32),
            pltpu.VMEM((GDN_CHAINS, c, GDN_DV), F32),
            pltpu.VMEM((GDN_CHAINS, 2 * c, GDN_DK), BF16),
            pltpu.VMEM((GDN_CHAINS // 2, c, 2 * c), BF16),
            pltpu.VMEM((GDN_CHAINS, GDN_DK, c), BF16),
            pltpu.VMEM((GDN_CHAINS, SUBLANES, GDN_DV), F32),
            pltpu.VMEM((SSM_GROUPS, SSM_STATE, SSM_GROUP_DIM), F32),
        ],
        compiler_params=pltpu.CompilerParams(
            dimension_semantics=("arbitrary", "arbitrary"), vmem_limit_bytes=VMEM_LIMIT_BYTES),
        name="mixers",
    )(qkv, gz, gcol, grow_c, gdn_norm_w.reshape(1, GDN_DV).astype(F32),
      xbc, sz, dcol, acol, per_chunk(drow), per_chunk(arow), dskip,
      ssm_norm_w.reshape(1, SSM_DIM).astype(F32), e64, e128)


def _mem_kv_kernel(mem_ref, nw_ref, wk_ref, wv_ref, k_out, v_out):
    m = _rmsnorm(mem_ref[0], nw_ref[...]).astype(BF16)
    k_out[0] = jnp.dot(m, wk_ref[...], preferred_element_type=F32).astype(k_out.dtype)
    v_out[0] = jnp.dot(m, wv_ref[...], preferred_element_type=F32).astype(v_out.dtype)


def _mem_kv(mem, mem_norm_w, wk, wv):
    bsz, mlen, _ = mem.shape
    kv_shape = jax.ShapeDtypeStruct((bsz, mlen, D_MODEL), BF16)
    return pl.pallas_call(
        _mem_kv_kernel,
        grid=(bsz,),
        in_specs=[pl.BlockSpec((1, mlen, D_MODEL), lambda b: (b, 0, 0)),
                  _const_spec((1, D_MODEL)),
                  _const_spec((D_MODEL, D_MODEL)),
                  _const_spec((D_MODEL, D_MODEL))],
        out_specs=[pl.BlockSpec((1, mlen, D_MODEL), lambda b: (b, 0, 0))] * 2,
        out_shape=[kv_shape, kv_shape],
        compiler_params=pltpu.CompilerParams(
            dimension_semantics=("arbitrary",), vmem_limit_bytes=VMEM_LIMIT_BYTES),
        name="mem_kv",
    )(mem, mem_norm_w.reshape(1, D_MODEL), wk.astype(BF16), wv.astype(BF16))


def _tail_kernel(x_ref, oa_ref, ob_ref, k_ref, v_ref, wout_ref, n2_ref, wq_ref, wo_ref, n3_ref,
                 wup_ref, wdown_ref, nf_ref, y_ref):
    x = x_ref[0]
    x = x + jnp.dot(oa_ref[0], wout_ref[:GDN_DIM, :], preferred_element_type=F32)
    x = x + jnp.dot(ob_ref[0], wout_ref[GDN_DIM:, :], preferred_element_type=F32)

    h = _rmsnorm(x, n2_ref[...]).astype(BF16)
    q = jnp.dot(h, wq_ref[...], preferred_element_type=F32)
    scale = MEM_HEADDIM ** -0.5
    heads = []
    for hd in range(MEM_HEADS):
        hs = slice(hd * MEM_HEADDIM, (hd + 1) * MEM_HEADDIM)
        s = _mm_nt(q[:, hs], k_ref[0, :, hs]) * scale
        s = s - jnp.max(s, axis=-1, keepdims=True)
        p = jnp.exp(s)
        p = p / jnp.sum(p, axis=-1, keepdims=True)
        heads.append(_mm(p, v_ref[0, :, hs]))
    attn = jnp.concatenate(heads, axis=1)
    x = x + _mm(attn, wo_ref[...])

    h = _rmsnorm(x, n3_ref[...]).astype(BF16)
    acc = x
    for c0 in range(0, D_FF, FF_GROUP):
        u = jnp.maximum(jnp.dot(h, wup_ref[:, c0:c0 + FF_GROUP], preferred_element_type=F32), 0.0)
        acc = acc + _mm(u * u, wdown_ref[c0:c0 + FF_GROUP, :])
    y_ref[0] = _rmsnorm(acc, nf_ref[...])


def _tail(x, o_a, o_b, k_mem, v_mem, w_out, norm2_w, wq, wo, norm3_w, w_up, w_down, final_w):
    bsz, seq, _ = x.shape
    rows = min(TAIL_ROWS, seq)
    mlen = k_mem.shape[1]
    row_spec = pl.BlockSpec((1, rows, D_MODEL), lambda b, t: (b, t, 0))
    mem_spec = pl.BlockSpec((1, mlen, D_MODEL), lambda b, t: (b, 0, 0))
    vec = lambda w: w.reshape(1, D_MODEL).astype(F32)
    return pl.pallas_call(
        _tail_kernel,
        grid=(bsz, seq // rows),
        in_specs=[row_spec, row_spec, row_spec, mem_spec, mem_spec,
                  _const_spec((GDN_DIM + SSM_DIM, D_MODEL)), _const_spec((1, D_MODEL)),
                  _const_spec((D_MODEL, D_MODEL)), _const_spec((D_MODEL, D_MODEL)),
                  _const_spec((1, D_MODEL)), _const_spec((D_MODEL, D_FF)),
                  _const_spec((D_FF, D_MODEL)), _const_spec((1, D_MODEL))],
        out_specs=row_spec,
        out_shape=jax.ShapeDtypeStruct((bsz, seq, D_MODEL), x.dtype),
        compiler_params=pltpu.CompilerParams(
            dimension_semantics=("arbitrary", "arbitrary"), vmem_limit_bytes=VMEM_LIMIT_BYTES),
        name="tail",
    )(x, o_a, o_b, k_mem, v_mem, w_out.astype(BF16), vec(norm2_w), wq.astype(BF16),
      wo.astype(BF16), vec(norm3_w), w_up.astype(BF16), w_down.astype(BF16), vec(final_w))


def kernel(x, mem, norm1_w, w_in, gdn_conv_w, gdn_a_log, gdn_dt_bias, gdn_norm_w, ssm_conv_w,
           ssm_conv_b, ssm_a_log, ssm_dt_bias, ssm_d, ssm_norm_w, w_out, norm2_w, mem_norm_w,
           wq_mem, wk_mem, wv_mem, wo_mem, norm3_w, w_up, w_down, final_norm_w):
    qkv, gz, gcol, grow, sz, xbc, dcol, drow, acol, arow = _in_proj(
        x, norm1_w, w_in, gdn_conv_w, gdn_a_log, gdn_dt_bias, ssm_conv_w, ssm_conv_b, ssm_dt_bias,
        ssm_a_log)
    o_a, o_b = _mixers(qkv, gz, gcol, grow, gdn_norm_w, xbc, sz, dcol, acol, drow, arow, ssm_d,
                       ssm_norm_w)
    k_mem, v_mem = _mem_kv(mem, mem_norm_w, wk_mem, wv_mem)
    return _tail(x, o_a, o_b, k_mem, v_mem, w_out, norm2_w, wq_mem, wo_mem, norm3_w, w_up, w_down,
                 final_norm_w)
```

```python
import jax
import jax.numpy as jnp
from jax import lax
from jax.experimental import pallas as pl
from jax.experimental.pallas import tpu as pltpu

D_MODEL = 1024
EPS = 1e-6
CONV_K = 4
GDN_HEADS = 8
GDN_DK = 128
GDN_DV = 128
GDN_DIM = GDN_HEADS * GDN_DV
GDN_CHUNK = 64
SSM_DIM = D_MODEL
SSM_HEADDIM = 64
SSM_HEADS = SSM_DIM // SSM_HEADDIM
SSM_GROUPS = 2
SSM_HPG = SSM_HEADS // SSM_GROUPS
SSM_STATE = 128
SSM_CHUNK = 128
SSM_BC = SSM_GROUPS * SSM_STATE
SSM_XBC = SSM_DIM + 2 * SSM_BC
SSM_GROUP_DIM = SSM_DIM // SSM_GROUPS
MEM_HEADS = 4
MEM_HEADDIM = D_MODEL // MEM_HEADS
D_FF = 4 * D_MODEL

F32 = jnp.float32
BF16 = jnp.bfloat16

V7X_VMEM_BYTES = 64 * 1024 * 1024
VMEM_LIMIT_BYTES = V7X_VMEM_BYTES - 8 * 1024 * 1024
SUBLANES = 8
LANES = 128

ROW_TILE = 512
ROW_GROUPS = ROW_TILE // SUBLANES
assert ROW_GROUPS == GDN_CHUNK and SSM_CHUNK == 2 * GDN_CHUNK
TAIL_ROWS = 512
COL_GROUP = 512
FF_GROUP = 1024
HALO = (CONV_K - 1) * SUBLANES


def _mm(a, b):
    return jnp.dot(a.astype(BF16), b.astype(BF16), preferred_element_type=F32)


def _mm_nt(a, b):
    return lax.dot_general(a.astype(BF16), b.astype(BF16), (((1,), (1,)), ((), ())),
                           preferred_element_type=F32)


def _mm_tn(a, b):
    return lax.dot_general(a.astype(BF16), b.astype(BF16), (((0,), (0,)), ((), ())),
                           preferred_element_type=F32)


SPLIT = 3


def _split(x):
    hi = x.astype(BF16)
    rest = x - hi.astype(F32)
    mid = rest.astype(BF16)
    lo = (rest - mid.astype(F32)).astype(BF16)
    return hi, mid, lo


def _tri_dot_left(tri_wide, x):
    return jnp.dot(tri_wide, jnp.concatenate(_split(x), axis=0), preferred_element_type=F32)


def _tri_dot_right(x, tri_tall):
    return jnp.dot(jnp.concatenate(_split(x), axis=1), tri_tall, preferred_element_type=F32)


def _rmsnorm(x, w):
    return x * lax.rsqrt(jnp.mean(x * x, axis=-1, keepdims=True) + EPS) * w


def _sigmoid(x):
    return 1.0 / (1.0 + jnp.exp(-x))


def _silu(x):
    half = 0.5 * x
    return half * (1.0 + jnp.tanh(half))


def _softplus(x):
    return jnp.maximum(x, 0.0) + jnp.log(1.0 + jnp.exp(-jnp.abs(x)))


def _const_spec(shape):
    zeros = (0,) * len(shape)
    return pl.BlockSpec(shape, lambda *_: zeros, pipeline_mode=pl.Buffered(1))


def _time_rows(chunk, size):
    start = chunk * size
    return pl.ds(start if isinstance(start, int) else pl.multiple_of(start, size), size)


def _row_pair_words(y):
    return pltpu.bitcast(y.astype(BF16), jnp.int32)


def _chunk_pair(ref, lead, pair):
    words = ref[lead + (pl.ds(pair, ROW_GROUPS, stride=SUBLANES // 2), slice(None))]
    even = pltpu.bitcast(lax.shift_left(words, 16), F32)
    odd = pltpu.bitcast(lax.bitwise_and(words, jnp.int32(-65536)), F32)
    return even, odd


def _conv_silu(p, halo, w, bias):
    rows = p.shape[0]
    sub0 = lax.broadcasted_iota(jnp.int32, (SUBLANES, p.shape[1]), 0) == 0
    wrapped = []
    for j in range(CONV_K - 1):
        cur = p[rows - HALO + j * SUBLANES:rows - HALO + (j + 1) * SUBLANES]
        prev = halo[j * SUBLANES:(j + 1) * SUBLANES]
        wrapped.append(jnp.where(sub0, pltpu.roll(prev, 1, 0), pltpu.roll(cur, 1, 0)))
    acc = p * w[CONV_K - 1:CONV_K, :]
    if bias is not None:
        acc = acc + bias
    for k in range(1, CONV_K):
        shifted = jnp.concatenate(wrapped[CONV_K - 1 - k:] + [p[:rows - k * SUBLANES]], axis=0)
        acc = acc + shifted * w[CONV_K - 1 - k:CONV_K - k, :]
    return _silu(acc)


def _in_proj_kernel(x_ref, n1_ref, perm_ref, w_ref, wc_ref,
                    gconv_ref, sconv_ref, sconvb_ref,
                    galog_c_ref, gbias_c_ref, galog_r_ref, gbias_r_ref, dbias_c_ref, dbias_r_ref,
                    ssm_a_c_ref, ssm_a_r_ref,
                    qkv_out, gz_out, gcol_out, grow_out, sz_out, xbc_out, dcol_out, drow_out,
                    acol_out, arow_out, ghalo_ref, shalo_ref, ws_ref):
    rows = x_ref.shape[1]

    @pl.when(pl.program_id(1) == 0)
    def _():
        ghalo_ref[...] = jnp.zeros_like(ghalo_ref)
        shalo_ref[...] = jnp.zeros_like(shalo_ref)

    @pl.when(jnp.logical_and(pl.program_id(0) == 0, pl.program_id(1) == 0))
    def _():
        words = pltpu.bitcast(w_ref[:, 4 * GDN_DIM:], jnp.int32)
        lo = 2 * GDN_HEADS
        ws_ref[...] = pltpu.bitcast(words[:, lo:lo + SSM_DIM + SSM_XBC], BF16)

    h = _rmsnorm(x_ref[0], n1_ref[...]).astype(BF16)
    hp = jnp.dot(perm_ref[...], h, preferred_element_type=F32).astype(BF16)

    off_gz, off_sz, off_xbc = 3 * GDN_DIM, 0, SSM_DIM
    q_scale = GDN_DK ** -0.5
    for c0 in range(0, 3 * GDN_DIM, COL_GROUP):
        cs = slice(c0, c0 + COL_GROUP)
        p = jnp.dot(hp, w_ref[:, cs], preferred_element_type=F32)
        y = _conv_silu(p, ghalo_ref[:, cs], gconv_ref[:, cs], None)
        ghalo_ref[:, cs] = p[rows - HALO:]
        for j in range(COL_GROUP // LANES):
            yh = y[:, j * LANES:(j + 1) * LANES]
            if c0 < 2 * GDN_DIM:
                yh = yh * lax.rsqrt(jnp.sum(yh * yh, axis=-1, keepdims=True) + EPS)
                if c0 < GDN_DIM:
                    yh = yh * q_scale
            qkv_out[0, c0 // LANES + j] = _row_pair_words(yh)

    for c0 in range(0, GDN_DIM, COL_GROUP):
        gz = _silu(jnp.dot(hp, w_ref[:, off_gz + c0:off_gz + c0 + COL_GROUP],
                           preferred_element_type=F32))
        sz = _silu(jnp.dot(hp, ws_ref[:, off_sz + c0:off_sz + c0 + COL_GROUP],
                           preferred_element_type=F32))
        for j in range(COL_GROUP // LANES):
            gz_out[0, c0 // LANES + j] = _row_pair_words(gz[:, j * LANES:(j + 1) * LANES])
            sz_out[0, c0 // LANES + j] = _row_pair_words(sz[:, j * LANES:(j + 1) * LANES])

    for c0 in range(0, SSM_XBC, COL_GROUP):
        cs = slice(c0, c0 + COL_GROUP)
        p = jnp.dot(hp, ws_ref[:, off_xbc + c0:off_xbc + c0 + COL_GROUP],
                    preferred_element_type=F32)
        y = _conv_silu(p, shalo_ref[:, cs], sconv_ref[:, cs], sconvb_ref[:, cs])
        shalo_ref[:, cs] = p[rows - HALO:]
        for j in range(COL_GROUP // LANES):
            xbc_out[0, c0 // LANES + j] = _row_pair_words(y[:, j * LANES:(j + 1) * LANES])

    blk = SSM_CHUNK
    ri = lax.broadcasted_iota(jnp.int32, (blk, blk), 0)
    ci = lax.broadcasted_iota(jnp.int32, (blk, blk), 1)
    same_gdn_chunk = (ri // GDN_CHUNK) == (ci // GDN_CHUNK)
    wide = lambda m: jnp.concatenate([m.astype(BF16)] * SPLIT, axis=1)
    tall = lambda m: jnp.concatenate([m.astype(BF16)] * SPLIT, axis=0)
    tril_ssd = wide((ri >= ci).astype(F32))
    triu_ssd = tall((ri <= ci).astype(F32))
    tril_gdn = wide(jnp.logical_and(ri >= ci, same_gdn_chunk).astype(F32))
    triu_gdn = tall(jnp.logical_and(ri <= ci, same_gdn_chunk).astype(F32))

    n_dt = SPLIT * SSM_HEADS
    p_cols = jnp.dot(h, wc_ref[...], preferred_element_type=F32)
    p_rows = p_cols.T
    pg = p_cols[:, n_dt:n_dt + 2 * GDN_HEADS]
    gates = jnp.where(lax.broadcasted_iota(jnp.int32, pg.shape, 1) < GDN_HEADS, _sigmoid(pg),
                      -jnp.exp(galog_c_ref[...]) * _softplus(pg + gbias_c_ref[...]))
    pgt = p_rows[n_dt:n_dt + 2 * GDN_HEADS]
    gates_t = jnp.where(lax.broadcasted_iota(jnp.int32, pgt.shape, 0) < GDN_HEADS, _sigmoid(pgt),
                        -jnp.exp(galog_r_ref[...]) * _softplus(pgt + gbias_r_ref[...]))
    dt = _softplus(p_cols[:, :n_dt] + dbias_c_ref[...])
    dt_t = _softplus(p_rows[:SSM_HEADS] + dbias_r_ref[...])
    dcol_out[0] = dt
    a_col = dt * ssm_a_c_ref[...]
    a_row = dt_t * ssm_a_r_ref[...]
    lane = lax.broadcasted_iota(jnp.int32, (blk, 2 * GDN_HEADS), 1)
    sub = lax.broadcasted_iota(jnp.int32, (2 * GDN_HEADS, blk), 0)
    for i, r0 in enumerate(range(0, rows, blk)):
        rs = slice(r0, r0 + blk)
        g_blk = gates[rs]
        gcol_out[0, rs, :] = jnp.where(lane < GDN_HEADS, g_blk, _tri_dot_left(tril_gdn, g_blk))
        gt_blk = gates_t[:, rs]
        grow_out[0, i] = jnp.where(sub < GDN_HEADS, gt_blk, _tri_dot_right(gt_blk, triu_gdn))
        drow_out[0, i] = dt_t[:, rs]
        acol_out[0, rs, :] = _tri_dot_left(tril_ssd, a_col[rs])
        arow_out[0, i] = _tri_dot_right(a_row[:, rs], triu_ssd)


def _in_proj(x, norm1_w, w_in, gdn_conv_w, gdn_a_log, gdn_dt_bias, ssm_conv_w, ssm_conv_b,
             ssm_dt_bias, ssm_a_log):
    bsz, seq, _ = x.shape
    rows = ROW_TILE
    assert seq % rows == 0
    nh = GDN_HEADS
    off_gates = 4 * GDN_DIM
    off_sz = off_gates + 2 * nh
    off_dt = off_sz + SSM_DIM + SSM_XBC
    assert off_dt + SSM_HEADS == w_in.shape[1]
    w_all = w_in.astype(BF16)
    w_gates = w_all[:, off_gates:off_sz]
    w_dt = w_all[:, off_dt:]
    zeros8 = jnp.zeros((nh,), F32)
    galog16 = jnp.concatenate([zeros8, gdn_a_log.astype(F32)])
    gbias16 = jnp.concatenate([zeros8, gdn_dt_bias.astype(F32)])
    dbias = ssm_dt_bias.astype(F32)
    ssm_a = -jnp.exp(ssm_a_log.astype(F32))
    tile_row = jnp.arange(rows)
    src_time = ROW_GROUPS * (tile_row % SUBLANES) + tile_row // SUBLANES
    perm = (src_time[:, None] == jnp.arange(rows)[None, :]).astype(BF16)

    args = [
        x, norm1_w.reshape(1, D_MODEL), perm, w_all,
        jnp.concatenate([jnp.tile(w_dt, (1, SPLIT)), w_gates,
                         jnp.zeros((D_MODEL, LANES - (SPLIT + 1) * SSM_HEADS), BF16)], axis=1),
        gdn_conv_w, ssm_conv_w, ssm_conv_b.reshape(1, SSM_XBC),
        galog16.reshape(1, 2 * nh), gbias16.reshape(1, 2 * nh),
        galog16.reshape(2 * nh, 1), gbias16.reshape(2 * nh, 1),
        jnp.tile(dbias, SPLIT).reshape(1, SPLIT * SSM_HEADS), dbias.reshape(SSM_HEADS, 1),
        jnp.tile(ssm_a, SPLIT).reshape(1, SPLIT * SSM_HEADS), ssm_a.reshape(SSM_HEADS, 1),
    ]
    in_specs = [pl.BlockSpec((1, rows, D_MODEL), lambda b, t: (b, t, 0))]
    in_specs += [_const_spec(a.shape) for a in args[1:]]

    def slab_spec(cols):
        return pl.BlockSpec((1, cols // LANES, rows // 2, LANES), lambda b, t: (b, 0, t, 0))

    def slab_shape(cols):
        return jax.ShapeDtypeStruct((bsz, cols // LANES, seq // 2, LANES), jnp.int32)

    def row_spec(cols):
        return pl.BlockSpec((1, rows, cols), lambda b, t: (b, t, 0))

    def row_layout_shape(nrows):
        return jax.ShapeDtypeStruct((bsz, seq // SSM_CHUNK, nrows, SSM_CHUNK), F32)

    def col_major_spec(nrows):
        return pl.BlockSpec((1, rows // SSM_CHUNK, nrows, SSM_CHUNK), lambda b, t: (b, t, 0, 0))

    out_shape = [
        slab_shape(3 * GDN_DIM),
        slab_shape(GDN_DIM),
        jax.ShapeDtypeStruct((bsz, seq, 2 * nh), F32),
        row_layout_shape(2 * nh),
        slab_shape(SSM_DIM),
        slab_shape(SSM_XBC),
        jax.ShapeDtypeStruct((bsz, seq, SPLIT * SSM_HEADS), F32),
        row_layout_shape(SSM_HEADS),
        jax.ShapeDtypeStruct((bsz, seq, SPLIT * SSM_HEADS), F32),
        row_layout_shape(SSM_HEADS),
    ]
    out_specs = [slab_spec(3 * GDN_DIM), slab_spec(GDN_DIM), row_spec(2 * nh),
                 col_major_spec(2 * nh), slab_spec(SSM_DIM), slab_spec(SSM_XBC),
                 row_spec(SPLIT * SSM_HEADS), col_major_spec(SSM_HEADS),
                 row_spec(SPLIT * SSM_HEADS), col_major_spec(SSM_HEADS)]
    return pl.pallas_call(
        _in_proj_kernel,
        grid=(bsz, seq // rows),
        in_specs=in_specs,
        out_specs=out_specs,
        out_shape=out_shape,
        scratch_shapes=[pltpu.VMEM((HALO, 3 * GDN_DIM), F32),
                        pltpu.VMEM((HALO, SSM_XBC), F32),
                        pltpu.VMEM((D_MODEL, SSM_DIM + SSM_XBC), BF16)],
        compiler_params=pltpu.CompilerParams(
            dimension_semantics=("arbitrary", "arbitrary"), vmem_limit_bytes=VMEM_LIMIT_BYTES),
        name="in_proj",
    )(*args)


GDN_PAIR = 2
GDN_CHAINS = GDN_PAIR * GDN_HEADS


def _gdn_prepare(qkv_ref, gcol_ref, grow_ref, pair):
    c = GDN_CHUNK
    ri = lax.broadcasted_iota(jnp.int32, (c, 2 * c), 0)
    lane = lax.broadcasted_iota(jnp.int32, (c, 2 * c), 1)
    ci = lane % c
    low = lane < c
    causal = ri >= ci
    strict = ri > ci
    same16 = (ri // 16) == (ci // 16)
    same32 = (ri // 32) == (ci // 32)
    only32 = jnp.logical_and(same32, jnp.logical_not(same16))
    eye = (ri == ci).astype(F32)
    zeros_k = jnp.zeros((c, GDN_DK), BF16)
    zeros_uw = jnp.zeros((c, GDN_DV + GDN_DK), BF16)

    def blockdiag(x_p):
        return jnp.concatenate([jnp.where(low, x_p, 0.0), jnp.where(low, 0.0, x_p)],
                               axis=0).astype(BF16)

    def mm_packed(l_p, x_p):
        return jnp.dot(l_p.astype(BF16), blockdiag(x_p), preferred_element_type=F32)

    q, k, kbs, rhs, exp_g, kdt, cdec = [], [], [], [], [], [], []
    q_pair = [_chunk_pair(qkv_ref, (0, h), pair) for h in range(GDN_HEADS)]
    k_pair = [_chunk_pair(qkv_ref, (0, GDN_HEADS + h), pair) for h in range(GDN_HEADS)]
    v_pair = [_chunk_pair(qkv_ref, (0, 2 * GDN_HEADS + h), pair) for h in range(GDN_HEADS)]
    gram_lhs, gram_rhs, decay = [], [], []
    g_rows = grow_ref[0, pair][GDN_HEADS:, :]
    g_swap = pltpu.roll(g_rows, c, 1)
    for j in range(GDN_PAIR):
        s = GDN_PAIR * pair + j
        gcol = gcol_ref[0, _time_rows(s, c), :]
        gc_wide = []
        for h in range(GDN_HEADS):
            q.append(q_pair[h][j])
            k.append(k_pair[h][j])
            vv = v_pair[h][j]
            beta = jnp.broadcast_to(gcol[:, h:h + 1], (c, GDN_DK))
            gcw = jnp.broadcast_to(gcol[:, GDN_HEADS + h:GDN_HEADS + h + 1], (c, GDN_DK))
            g_last = gcw[c - 1:c, :]
            gc_wide.append(gcw)
            exp_g.append(jnp.exp(gcw))
            kbs.append(k[-1] * beta)
            rhs.append(jnp.concatenate([vv * beta, kbs[-1] * exp_g[-1]], axis=1).astype(BF16))
            kdt.append((k[-1] * jnp.exp(g_last - gcw)).T.astype(BF16))
            cdec.append(jnp.broadcast_to(jnp.exp(g_last), (SUBLANES, GDN_DV)))
            if h % 2 == 1:
                a, b = len(q) - 2, len(q) - 1
                gram_lhs.append(jnp.concatenate(
                    [jnp.concatenate([kbs[a], kbs[b]], axis=1),
                     jnp.concatenate([q[a], q[b]], axis=1)], axis=0).astype(BF16))
                gram_rhs.append(jnp.concatenate(
                    [jnp.concatenate([k[a].astype(BF16), zeros_k], axis=1),
                     jnp.concatenate([zeros_k, k[b].astype(BF16)], axis=1)], axis=0))
                gc_p = jnp.where(low, gc_wide[h - 1], gc_wide[h])
                even_row = (g_rows if j == 0 else g_swap)[h - 1:h, :]
                odd_row = (g_swap if j == 0 else g_rows)[h:h + 1, :]
                gr_p = jnp.where(low[:1], even_row, odd_row)
                decay.append(jnp.exp(jnp.where(causal, gc_p - gr_p, -jnp.inf)))
    gram = [lax.dot_general(l, r, (((1,), (1,)), ((), ())), preferred_element_type=F32)
            for l, r in zip(gram_lhs, gram_rhs)]
    yield None
    a_strict = [jnp.where(strict, g[:c] * d, 0.0) for g, d in zip(gram, decay)]
    attn = [(g[c:] * d).astype(BF16) for g, d in zip(gram, decay)]
    d16 = [jnp.where(same16, a, 0.0) for a in a_strict]
    inv = [eye - x for x in d16]
    xp = [mm_packed(x, x) for x in d16]
    yield None
    for step in range(3):
        inv = [i + mm_packed(i, x) for i, x in zip(inv, xp)]
        if step < 2:
            xp = [mm_packed(x, x) for x in xp]
        yield None
    for mask in (only32, jnp.logical_not(same32)):
        tmp = [mm_packed(i, jnp.where(mask, a, 0.0)) for i, a in zip(inv, a_strict)]
        yield None
        inv = [i - mm_packed(t, i) for i, t in zip(inv, tmp)]
        yield None
    sol = [jnp.dot(t.astype(BF16),
                   jnp.concatenate([jnp.concatenate([rhs[2 * i], zeros_uw], axis=1),
                                    jnp.concatenate([zeros_uw, rhs[2 * i + 1]], axis=1)], axis=0),
                   preferred_element_type=F32) for i, t in enumerate(inv)]
    width = GDN_DV + GDN_DK
    sol = [x[:, half * width:(half + 1) * width] for x in sol for half in range(2)]
    u = [x[:, :GDN_DV] for x in sol]
    wq = [jnp.concatenate([x[:, GDN_DV:], qq * e], axis=0).astype(BF16)
          for x, qq, e in zip(sol, q, exp_g)]
    yield u, wq, attn, kdt, cdec


def _gdn_recur(u_ref, wq_ref, at_ref, kdt_ref, cd_ref, s_ref, gz_ref, nw_ref, o_ref, pair, j):
    c = GDN_CHUNK
    s = GDN_PAIR * pair + j
    chains = [j * GDN_HEADS + h for h in range(GDN_HEADS)]
    state = [s_ref[h] for h in range(GDN_HEADS)]
    ws_qs = [jnp.dot(wq_ref[ch], st.astype(BF16), preferred_element_type=F32)
             for ch, st in zip(chains, state)]
    yield None
    v_new = [(u_ref[ch] - x[:c]).astype(BF16) for ch, x in zip(chains, ws_qs)]
    zeros_v = jnp.zeros((c, GDN_DV), BF16)
    o = []
    for p in range(GDN_HEADS // 2):
        vn = jnp.concatenate([jnp.concatenate([v_new[2 * p], zeros_v], axis=1),
                              jnp.concatenate([zeros_v, v_new[2 * p + 1]], axis=1)], axis=0)
        o_p = jnp.dot(at_ref[j * (GDN_HEADS // 2) + p], vn, preferred_element_type=F32)
        o.append(ws_qs[2 * p][c:] + o_p[:, :GDN_DV])
        o.append(ws_qs[2 * p + 1][c:] + o_p[:, GDN_DV:])
    for h, ch in enumerate(chains):
        s_ref[h] = (state[h] * cd_ref[ch][:1]
                    + jnp.dot(kdt_ref[ch], v_new[h], preferred_element_type=F32))
    rows = _time_rows(s, c)
    for h in range(GDN_HEADS):
        gate = _chunk_pair(gz_ref, (0, h), pair)[j]
        o_ref[0, rows, h * GDN_DV:(h + 1) * GDN_DV] = (
            _rmsnorm(o[h], nw_ref[...]) * gate).astype(o_ref.dtype)
    yield None


def _spread_lanes(cols, onehot):
    heads = cols.shape[1] // SPLIT
    replica = lax.broadcasted_iota(jnp.int32, cols.shape, 1) // heads
    hi, mid, lo = _split(cols)
    terms = jnp.where(replica == 0, hi, jnp.where(replica == 1, mid, lo))
    return jnp.dot(terms, onehot, preferred_element_type=F32)


def _ssd_chunk(xbc_ref, sz_ref, dcol_ref, acol_ref, drow_ref, arow_ref, dskip_ref, nw_ref,
               e64_ref, e128_ref, o_ref, h_ref, n):
    c = SSM_CHUNK
    sub_per_chunk = c // GDN_CHUNK
    x_slabs = SSM_DIM // LANES
    ri = lax.broadcasted_iota(jnp.int32, (c, c), 0)
    ci = lax.broadcasted_iota(jnp.int32, (c, c), 1)
    causal = ri >= ci
    pair_w = 2 * SSM_HEADDIM
    low_half = lax.broadcasted_iota(jnp.int32, (c, pair_w), 1) < SSM_HEADDIM
    high_half = jnp.logical_not(low_half)
    gd = SSM_GROUP_DIM
    pairs_per_group = SSM_HPG // 2
    groups = range(SSM_GROUPS)
    rs = _time_rows(n, c)
    s0 = n * sub_per_chunk

    def slab(ref, idx):
        return jnp.concatenate(_chunk_pair(ref, (0, idx), n), axis=0)

    dt_col = dcol_ref[0, rs, :]
    acs_col = acol_ref[0, rs, :]
    dt_row = drow_ref[0, n]
    acs_row = arow_ref[0, n]
    a_last = acs_col[c - 1:c, :]
    coef = jnp.exp(a_last - acs_col) * dt_col
    chunk_decay = jnp.broadcast_to(jnp.exp(a_last), (SUBLANES, a_last.shape[1]))
    acs_wide = _spread_lanes(acs_col, e128_ref[...])
    wide = _spread_lanes(jnp.concatenate([coef, jnp.exp(acs_col), chunk_decay], axis=0),
                         e64_ref[...])
    coef_wide, exp_a_wide, cd_wide = wide[:c], wide[c:2 * c], wide[2 * c:2 * c + 1]
    yield None
    bg = [slab(xbc_ref, x_slabs + g) for g in groups]
    cg = [slab(xbc_ref, x_slabs + SSM_GROUPS + g) for g in groups]
    cb = [_mm_nt(cg[g], bg[g]) for g in groups]
    h_prev = [h_ref[g] for g in groups]
    y_off = [_mm(cg[g], h_prev[g]) * exp_a_wide[:, g * gd:(g + 1) * gd] for g in groups]
    yield None
    x_pairs = [[slab(xbc_ref, g * pairs_per_group + jp) for jp in range(pairs_per_group)]
               for g in groups]
    x_g = [jnp.concatenate(x_pairs[g], axis=1) for g in groups]
    for g in groups:
        gs = slice(g * gd, (g + 1) * gd)
        h_ref[g] = h_prev[g] * cd_wide[:, gs] + _mm_tn(bg[g], x_g[g] * coef_wide[:, gs])
    yield None
    y_pairs = [[], []]
    for g in groups:
        for jp in range(pairs_per_group):
            hd = g * SSM_HPG + 2 * jp
            acc = y_off[g][:, jp * pair_w:(jp + 1) * pair_w]
            for half, mask in enumerate((low_half, high_half)):
                ac = acs_wide[:, (hd + half) * LANES:(hd + half + 1) * LANES]
                ar = acs_row[hd + half:hd + half + 1, :]
                seg = jnp.exp(jnp.where(causal, ac - ar, -jnp.inf))
                scores = seg * (cb[g] * dt_row[hd + half:hd + half + 1, :])
                acc = acc + _mm(scores, jnp.where(mask, x_pairs[g][jp], 0.0))
            y_pairs[g].append(acc)
            if jp % 2 == 1:
                yield None
    for g in groups:
        gs = slice(g * gd, (g + 1) * gd)
        y = jnp.concatenate(y_pairs[g], axis=1) + x_g[g] * dskip_ref[:, gs]
        gate = jnp.concatenate(
            [slab(sz_ref, g * pairs_per_group + jp) for jp in range(pairs_per_group)], axis=1)
        yg = y * gate
        yg = yg * lax.rsqrt(jnp.mean(yg * yg, axis=-1, keepdims=True) + EPS)
        o_ref[0, rs, gs] = (yg * nw_ref[:, gs]).astype(o_ref.dtype)
    yield None


def _interleave(*gens):
    last = [None] * len(gens)
    live = list(range(len(gens)))
    while live:
        for i in list(live):
            try:
                last[i] = next(gens[i])
            except StopIteration:
                live.remove(i)
    return last


def _spaced(gen, gap):
    for value in gen:
        yield value
        for _ in range(gap):
            yield value


def _mixers_kernel(qkv_ref, gz_ref, gcol_ref, grow_ref, gnw_ref,
                   xbc_ref, sz_ref, dcol_ref, acol_ref, drow_ref, arow_ref, dskip_ref, snw_ref,
                   e64_ref, e128_ref, oa_ref, ob_ref,
                   s_ref, u_ref, wq_ref, at_ref, kdt_ref, cd_ref, h_ref):
    npairs = ROW_TILE // (GDN_PAIR * GDN_CHUNK)
    assert npairs == ROW_TILE // SSM_CHUNK

    @pl.when(pl.program_id(1) == 0)
    def _():
        s_ref[...] = jnp.zeros_like(s_ref)
        h_ref[...] = jnp.zeros_like(h_ref)

    def store_prepared(prepared):
        u, wq, attn, kdt, cdec = prepared
        for ch in range(GDN_CHAINS):
            u_ref[ch] = u[ch]
            wq_ref[ch] = wq[ch]
            kdt_ref[ch] = kdt[ch]
            cd_ref[ch] = cdec[ch]
        for pc in range(GDN_CHAINS // 2):
            at_ref[pc] = attn[pc]

    def recur(pair):
        for j in range(GDN_PAIR):
            yield from _gdn_recur(u_ref, wq_ref, at_ref, kdt_ref, cd_ref, s_ref, gz_ref, gnw_ref,
                                  oa_ref, pair, j)

    def ssd(n):
        return _ssd_chunk(xbc_ref, sz_ref, dcol_ref, acol_ref, drow_ref, arow_ref, dskip_ref,
                          snw_ref, e64_ref, e128_ref, ob_ref, h_ref, n)

    prepared, _ = _interleave(_gdn_prepare(qkv_ref, gcol_ref, grow_ref, 0), ssd(0))
    store_prepared(prepared)

    def body(pair, carry):
        prepared, _, _ = _interleave(_gdn_prepare(qkv_ref, gcol_ref, grow_ref, pair + 1),
                                     _spaced(recur(pair), 1), ssd(pair + 1))
        store_prepared(prepared)
        return carry

    lax.fori_loop(0, npairs - 1, body, 0)

    _interleave(recur(npairs - 1))


def _mixers(qkv, gz, gcol, grow, gdn_norm_w, xbc, sz, dcol, acol, drow, arow, d_skip, ssm_norm_w):
    bsz, seq, _ = gcol.shape
    rows = ROW_TILE
    nh2 = 2 * GDN_HEADS
    nh = SSM_HEADS
    c = GDN_CHUNK
    dskip = jnp.repeat(d_skip.astype(F32), SSM_HEADDIM).reshape(1, SSM_DIM)
    eye = jnp.tile(jnp.eye(nh, dtype=BF16), (SPLIT, 1))
    e64 = jnp.repeat(eye, SSM_HEADDIM, axis=1)
    e128 = jnp.repeat(eye, LANES, axis=1)

    def slab_spec(cols):
        return pl.BlockSpec((1, cols // LANES, rows // 2, LANES), lambda b, t: (b, 0, t, 0))

    def row_spec(cols):
        return pl.BlockSpec((1, rows, cols), lambda b, t: (b, t, 0))

    def per_chunk_spec(nrows, chunk):
        return pl.BlockSpec((1, rows // chunk, nrows, chunk), lambda b, t: (b, t, 0, 0))

    out_shape = jax.ShapeDtypeStruct((bsz, seq, D_MODEL), BF16)
    return pl.pallas_call(
        _mixers_kernel,
        grid=(bsz, seq // rows),
        in_specs=[
            slab_spec(3 * GDN_DIM), slab_spec(GDN_DIM), row_spec(nh2),
            per_chunk_spec(nh2, SSM_CHUNK), _const_spec((1, GDN_DV)),
            slab_spec(SSM_XBC), slab_spec(SSM_DIM), row_spec(SPLIT * nh), row_spec(SPLIT * nh),
            per_chunk_spec(nh, SSM_CHUNK), per_chunk_spec(nh, SSM_CHUNK),
            _const_spec((1, SSM_DIM)), _const_spec((1, SSM_DIM)),
            _const_spec((SPLIT * nh, SSM_DIM)), _const_spec((SPLIT * nh, nh * LANES)),
        ],
        out_specs=[row_spec(GDN_DIM), row_spec(SSM_DIM)],
        out_shape=[out_shape, out_shape],
        scratch_shapes=[
            pltpu.VMEM((GDN_HEADS, GDN_DK, GDN_DV), F32),
            pltpu.VMEM((GDN_CHAINS, c, GDN_DV), F32),
            pltpu.VMEM((GDN_CHAINS, 2 * c, GDN_DK), BF16),
            pltpu.VMEM((GDN_CHAINS // 2, c, 2 * c), BF16),
            pltpu.VMEM((GDN_CHAINS, GDN_DK, c), BF16),
            pltpu.VMEM((GDN_CHAINS, SUBLANES, GDN_DV), F32),
            pltpu.VMEM((SSM_GROUPS, SSM_STATE, SSM_GROUP_DIM), F32),
        ],
        compiler_params=pltpu.CompilerParams(
            dimension_semantics=("arbitrary", "arbitrary"), vmem_limit_bytes=VMEM_LIMIT_BYTES),
        name="mixers",
    )(qkv, gz, gcol, grow, gdn_norm_w.reshape(1, GDN_DV).astype(F32),
      xbc, sz, dcol, acol, drow, arow, dskip,
      ssm_norm_w.reshape(1, SSM_DIM).astype(F32), e64, e128)


def _mem_kv_kernel(mem_ref, nw_ref, wk_ref, wv_ref, k_out, v_out):
    m = _rmsnorm(mem_ref[0], nw_ref[...]).astype(BF16)
    k_out[0] = jnp.dot(m, wk_ref[...], preferred_element_type=F32).astype(k_out.dtype)
    v_out[0] = jnp.dot(m, wv_ref[...], preferred_element_type=F32).astype(v_out.dtype)


def _mem_kv(mem, mem_norm_w, wk, wv):
    bsz, mlen, _ = mem.shape
    kv_shape = jax.ShapeDtypeStruct((bsz, mlen, D_MODEL), BF16)
    return pl.pallas_call(
        _mem_kv_kernel,
        grid=(bsz,),
        in_specs=[pl.BlockSpec((1, mlen, D_MODEL), lambda b: (b, 0, 0)),
                  _const_spec((1, D_MODEL)),
                  _const_spec((D_MODEL, D_MODEL)),
                  _const_spec((D_MODEL, D_MODEL))],
        out_specs=[pl.BlockSpec((1, mlen, D_MODEL), lambda b: (b, 0, 0))] * 2,
        out_shape=[kv_shape, kv_shape],
        compiler_params=pltpu.CompilerParams(
            dimension_semantics=("arbitrary",), vmem_limit_bytes=VMEM_LIMIT_BYTES),
        name="mem_kv",
    )(mem, mem_norm_w.reshape(1, D_MODEL), wk.astype(BF16), wv.astype(BF16))


def _tail_kernel(x_ref, oa_ref, ob_ref, k_ref, v_ref, wout_ref, n2_ref, wq_ref, wo_ref, n3_ref,
                 wup_ref, wdown_ref, nf_ref, y_ref):
    x = x_ref[0]
    x = x + jnp.dot(oa_ref[0], wout_ref[:GDN_DIM, :], preferred_element_type=F32)
    x = x + jnp.dot(ob_ref[0], wout_ref[GDN_DIM:, :], preferred_element_type=F32)

    h = _rmsnorm(x, n2_ref[...]).astype(BF16)
    q = jnp.dot(h, wq_ref[...], preferred_element_type=F32)
    scale = MEM_HEADDIM ** -0.5
    heads = []
    for hd in range(MEM_HEADS):
        hs = slice(hd * MEM_HEADDIM, (hd + 1) * MEM_HEADDIM)
        s = _mm_nt(q[:, hs], k_ref[0, :, hs]) * scale
        s = s - jnp.max(s, axis=-1, keepdims=True)
        p = jnp.exp(s)
        p = p / jnp.sum(p, axis=-1, keepdims=True)
        heads.append(_mm(p, v_ref[0, :, hs]))
    attn = jnp.concatenate(heads, axis=1)
    x = x + _mm(attn, wo_ref[...])

    h = _rmsnorm(x, n3_ref[...]).astype(BF16)
    acc = x
    for c0 in range(0, D_FF, FF_GROUP):
        u = jnp.maximum(jnp.dot(h, wup_ref[:, c0:c0 + FF_GROUP], preferred_element_type=F32), 0.0)
        acc = acc + _mm(u * u, wdown_ref[c0:c0 + FF_GROUP, :])
    y_ref[0] = _rmsnorm(acc, nf_ref[...])


def _tail(x, o_a, o_b, k_mem, v_mem, w_out, norm2_w, wq, wo, norm3_w, w_up, w_down, final_w):
    bsz, seq, _ = x.shape
    rows = min(TAIL_ROWS, seq)
    mlen = k_mem.shape[1]
    row_spec = pl.BlockSpec((1, rows, D_MODEL), lambda b, t: (b, t, 0))
    mem_spec = pl.BlockSpec((1, mlen, D_MODEL), lambda b, t: (b, 0, 0))
    vec = lambda w: w.reshape(1, D_MODEL).astype(F32)
    return pl.pallas_call(
        _tail_kernel,
        grid=(bsz, seq // rows),
        in_specs=[row_spec, row_spec, row_spec, mem_spec, mem_spec,
                  _const_spec((GDN_DIM + SSM_DIM, D_MODEL)), _const_spec((1, D_MODEL)),
                  _const_spec((D_MODEL, D_MODEL)), _const_spec((D_MODEL, D_MODEL)),
                  _const_spec((1, D_MODEL)), _const_spec((D_MODEL, D_FF)),
                  _const_spec((D_FF, D_MODEL)), _const_spec((1, D_MODEL))],
        out_specs=row_spec,
        out_shape=jax.ShapeDtypeStruct((bsz, seq, D_MODEL), x.dtype),
        compiler_params=pltpu.CompilerParams(
            dimension_semantics=("arbitrary", "arbitrary"), vmem_limit_bytes=VMEM_LIMIT_BYTES),
        name="tail",
    )(x, o_a, o_b, k_mem, v_mem, w_out.astype(BF16), vec(norm2_w), wq.astype(BF16),
      wo.astype(BF16), vec(norm3_w), w_up.astype(BF16), w_down.astype(BF16), vec(final_w))


def kernel(x, mem, norm1_w, w_in, gdn_conv_w, gdn_a_log, gdn_dt_bias, gdn_norm_w, ssm_conv_w,
           ssm_conv_b, ssm_a_log, ssm_dt_bias, ssm_d, ssm_norm_w, w_out, norm2_w, mem_norm_w,
           wq_mem, wk_mem, wv_mem, wo_mem, norm3_w, w_up, w_down, final_norm_w):
    qkv, gz, gcol, grow, sz, xbc, dcol, drow, acol, arow = _in_proj(
        x, norm1_w, w_in, gdn_conv_w, gdn_a_log, gdn_dt_bias, ssm_conv_w, ssm_conv_b, ssm_dt_bias,
        ssm_a_log)
    o_a, o_b = _mixers(qkv, gz, gcol, grow, gdn_norm_w, xbc, sz, dcol, acol, drow, arow, ssm_d,
                       ssm_norm_w)
    k_mem, v_mem = _mem_kv(mem, mem_norm_w, wk_mem, wv_mem)
    return _tail(x, o_a, o_b, k_mem, v_mem, w_out, norm2_w, wq_mem, wo_mem, norm3_w, w_up, w_down,
                 final_norm_w)
```
